```python
import math
import jax, jax.numpy as jnp
from jax import lax
import numpy as np

D_MODEL = 2048
BATCH = 4
SEQ = 2048
DEPTH = 2

N_MIXERS = 2
PLE_DIM = 256
D_FF = ((8 * D_MODEL // 3 + 255) // 256) * 256
EPS = 1e-6

GM_WIDTH = D_MODEL
GM_CHUNK = 128
GM_HEAD_DIM = 128
GM_HEADS = GM_WIDTH // GM_HEAD_DIM

S5_WIDTH = D_MODEL
S5_GROUP = 16
S5_GROUPS = S5_WIDTH // S5_GROUP
S5_STATE = 64
DT_MIN = 1e-3
DT_MAX = 1e-1

N_A = (DEPTH + 1) // 2
N_B = DEPTH // 2

kernel_name = "hybrid_gmlp_s5_interleaved_trunk"


def _rms_norm(x, g):
    xf = x.astype(jnp.float32)
    y = xf * lax.rsqrt(jnp.mean(xf * xf, axis=-1, keepdims=True) + EPS)
    return (y * g.astype(jnp.float32)).astype(x.dtype)


def _layer_norm(x, g, b):
    xf = x.astype(jnp.float32)
    mu = jnp.mean(xf, axis=-1, keepdims=True)
    xc = xf - mu
    y = xc * lax.rsqrt(jnp.mean(xc * xc, axis=-1, keepdims=True) + EPS)
    return (y * g.astype(jnp.float32) + b.astype(jnp.float32)).astype(x.dtype)


def _gmlp_mixer(h, w_in, ln_g, ln_b, w_s, b_s, w_out):
    bsz, seq, _ = h.shape
    z = jax.nn.gelu(h @ w_in)
    u, v = jnp.split(z, 2, axis=-1)
    v = _layer_norm(v, ln_g, ln_b)
    vc = v.reshape(bsz, seq // GM_CHUNK, GM_CHUNK, GM_HEADS, GM_HEAD_DIM)
    causal = jnp.tril(jnp.ones((GM_CHUNK, GM_CHUNK), dtype=bool))
    w_causal = jnp.where(causal, w_s, jnp.zeros_like(w_s)).astype(v.dtype)
    sv = jnp.einsum('hts,bnshd->bnthd', w_causal, vc) + b_s.T[:, :, None].astype(v.dtype)
    return (u * sv.reshape(bsz, seq, GM_WIDTH)) @ w_out


def _lin_combine(left, right):
    a_l, b_l = left
    a_r, b_r = right
    return a_r * a_l, a_r * b_l + b_r


def _s5_mixer(h, w_in, a_re, a_im, log_dt, b_re, b_im, c_re, c_im, d, w_out):
    f32 = jnp.float32
    bsz, seq, _ = h.shape
    u = (h @ w_in).astype(f32)
    lam = lax.complex(a_re.astype(f32), a_im.astype(f32))
    dt = jnp.exp(log_dt.astype(f32))[:, None]
    lam_bar = jnp.exp(lam * dt)
    b_bar = ((lam_bar - 1.0) / lam)[..., None] * lax.complex(b_re.astype(f32), b_im.astype(f32))
    ug = u.reshape(bsz, seq, S5_GROUPS, S5_GROUP)
    bu = lax.complex(jnp.einsum('blgc,gpc->blgp', ug, jnp.real(b_bar)),
                     jnp.einsum('blgc,gpc->blgp', ug, jnp.imag(b_bar)))
    a = jnp.broadcast_to(lam_bar, (1, seq) + lam_bar.shape)
    _, s = lax.associative_scan(_lin_combine, (a, bu), axis=1)
    y = (jnp.einsum('blgp,gcp->blgc', jnp.real(s), c_re.astype(f32))
         - jnp.einsum('blgp,gcp->blgc', jnp.imag(s), c_im.astype(f32)))
    y = y.reshape(bsz, seq, S5_WIDTH) + d.astype(f32) * u
    g = jax.nn.gelu(y).astype(h.dtype)
    val, gate = jnp.split(g @ w_out, 2, axis=-1)
    return val * jax.nn.sigmoid(gate)


def _swiglu(h, w1, w3, w2):
    return (jax.nn.silu(h @ w1) * (h @ w3)) @ w2


def _normal(k, shape, scale):
    return jax.random.normal(k, shape, jnp.float32) * scale


def setup_inputs(seed: int = 0) -> dict:
    key = jax.random.key(seed)
    ks = jax.random.split(key, 32)
    D = D_MODEL
    x = _normal(ks[0], (BATCH, SEQ, D), 1.0)
    p = _normal(ks[1], (DEPTH, BATCH, SEQ, PLE_DIM), 1.0)
    norm_mix = 1.0 + _normal(ks[2], (DEPTH, D), 0.02)
    norm_ffn = 1.0 + _normal(ks[3], (DEPTH, D), 0.02)
    norm_ple = 1.0 + _normal(ks[4], (DEPTH, D), 0.02)
    norm_final = 1.0 + _normal(ks[5], (D,), 0.02)
    gm_w_in = _normal(ks[6], (N_A, D, 2 * GM_WIDTH), D ** -0.5)
    gm_ln_g = 1.0 + _normal(ks[7], (N_A, GM_WIDTH), 0.02)
    gm_ln_b = _normal(ks[8], (N_A, GM_WIDTH), 0.02)
    gm_w_s = _normal(ks[9], (N_A, GM_HEADS, GM_CHUNK, GM_CHUNK), GM_CHUNK ** -0.5)
    gm_b_s = 1.0 + _normal(ks[10], (N_A, GM_HEADS, GM_CHUNK), 0.02)
    gm_w_out = _normal(ks[11], (N_A, GM_WIDTH, D), GM_WIDTH ** -0.5)
    s5_w_in = _normal(ks[12], (N_B, D, S5_WIDTH), D ** -0.5)
    s5_a_re = -0.5 + _normal(ks[13], (N_B, S5_GROUPS, S5_STATE), 0.01)
    n_idx = jnp.arange(S5_STATE, dtype=jnp.float32)
    s5_a_im = math.pi * n_idx + _normal(ks[14], (N_B, S5_GROUPS, S5_STATE), 0.01)
    s5_log_dt = jax.random.uniform(ks[15], (N_B, S5_GROUPS), jnp.float32,
                                   math.log(DT_MIN), math.log(DT_MAX))
    b_scale = (S5_GROUP ** -0.5) / math.sqrt(2.0)
    s5_b_re = _normal(ks[16], (N_B, S5_GROUPS, S5_STATE, S5_GROUP), b_scale)
    s5_b_im = _normal(ks[17], (N_B, S5_GROUPS, S5_STATE, S5_GROUP), b_scale)
    c_scale = (S5_STATE ** -0.5) / math.sqrt(2.0)
    s5_c_re = _normal(ks[18], (N_B, S5_GROUPS, S5_GROUP, S5_STATE), c_scale)
    s5_c_im = _normal(ks[19], (N_B, S5_GROUPS, S5_GROUP, S5_STATE), c_scale)
    s5_d = _normal(ks[20], (N_B, S5_WIDTH), 1.0)
    s5_w_out = _normal(ks[21], (N_B, S5_WIDTH, 2 * D), S5_WIDTH ** -0.5)
    ffn_w1 = _normal(ks[22], (DEPTH, D, D_FF), D ** -0.5)
    ffn_w3 = _normal(ks[23], (DEPTH, D, D_FF), D ** -0.5)
    ffn_w2 = _normal(ks[24], (DEPTH, D_FF, D), D_FF ** -0.5)
    ple_w_gate = _normal(ks[25], (DEPTH, D, D), D ** -0.5)
    ple_w_proj = _normal(ks[26], (DEPTH, PLE_DIM, D), PLE_DIM ** -0.5)
    return {
        "x": x, "p": p,
        "norm_mix": norm_mix, "norm_ffn": norm_ffn, "norm_ple": norm_ple, "norm_final": norm_final,
        "gm_w_in": gm_w_in, "gm_ln_g": gm_ln_g, "gm_ln_b": gm_ln_b,
        "gm_w_s": gm_w_s, "gm_b_s": gm_b_s, "gm_w_out": gm_w_out,
        "s5_w_in": s5_w_in, "s5_a_re": s5_a_re, "s5_a_im": s5_a_im, "s5_log_dt": s5_log_dt,
        "s5_b_re": s5_b_re, "s5_b_im": s5_b_im, "s5_c_re": s5_c_re, "s5_c_im": s5_c_im,
        "s5_d": s5_d, "s5_w_out": s5_w_out,
        "ffn_w1": ffn_w1, "ffn_w3": ffn_w3, "ffn_w2": ffn_w2,
        "ple_w_gate": ple_w_gate, "ple_w_proj": ple_w_proj,
    }


def reference(x, p, norm_mix, norm_ffn, norm_ple, norm_final,
              gm_w_in, gm_ln_g, gm_ln_b, gm_w_s, gm_b_s, gm_w_out,
              s5_w_in, s5_a_re, s5_a_im, s5_log_dt, s5_b_re, s5_b_im, s5_c_re, s5_c_im,
              s5_d, s5_w_out,
              ffn_w1, ffn_w3, ffn_w2, ple_w_gate, ple_w_proj):
    for i in range(DEPTH):
        j = i // N_MIXERS
        h = _rms_norm(x, norm_mix[i])
        if i % N_MIXERS == 0:
            x = x + _gmlp_mixer(h, gm_w_in[j], gm_ln_g[j], gm_ln_b[j], gm_w_s[j], gm_b_s[j], gm_w_out[j])
        else:
            x = x + _s5_mixer(h, s5_w_in[j], s5_a_re[j], s5_a_im[j], s5_log_dt[j],
                              s5_b_re[j], s5_b_im[j], s5_c_re[j], s5_c_im[j], s5_d[j], s5_w_out[j])
        x = x + _swiglu(_rms_norm(x, norm_ffn[i]), ffn_w1[i], ffn_w3[i], ffn_w2[i])
        gate = jax.nn.sigmoid(_rms_norm(x, norm_ple[i]) @ ple_w_gate[i])
        x = x + gate * (p[i] @ ple_w_proj[i])
    return _rms_norm(x, norm_final)
```

```python
import functools
import math

import jax
import jax.numpy as jnp
from jax.experimental import pallas as pl
from jax.experimental.pallas import tpu as pltpu

F32 = jnp.float32
BF16 = jnp.bfloat16

D_MODEL = 2048
PLE_DIM = 256
EPS = 1e-6

GM_CHUNK = 128
GM_HEAD_DIM = 128
GM_HEADS = D_MODEL // GM_HEAD_DIM

S5_GROUP = 16
S5_GROUPS = D_MODEL // S5_GROUP
S5_STATE = 64
S5_T = 16
S5_TC = S5_T * S5_GROUP
S5_RI = 2 * S5_STATE
S5_GB = 8
S5_SCAN_STEPS = 7

VMEM_LIMIT = 52 * 1024 * 1024


def _params(*sem):
    return pltpu.CompilerParams(dimension_semantics=sem, vmem_limit_bytes=VMEM_LIMIT)


def _rms(xf, gain):
    ms = jnp.mean(xf * xf, axis=-1, keepdims=True)
    return xf * jax.lax.rsqrt(ms + EPS) * gain


def _dot(a, b):
    return jnp.dot(a, b, preferred_element_type=F32)


def _rms_mm_kernel(x_ref, g_ref, w_ref, o_ref, h_scr, *, act):
    @pl.when(pl.program_id(1) == 0)
    def _():
        h_scr[...] = _rms(x_ref[...], g_ref[...]).astype(BF16)

    acc = _dot(h_scr[...], w_ref[...])
    if act == "gelu":
        acc = jax.nn.gelu(acc)
    o_ref[...] = acc.astype(o_ref.dtype)


def _rms_mm(x, gain, w, *, act, tm, tn):
    m, k = x.shape
    n = w.shape[1]
    return pl.pallas_call(
        functools.partial(_rms_mm_kernel, act=act),
        grid=(m // tm, n // tn),
        in_specs=[
            pl.BlockSpec((tm, k), lambda i, j: (i, 0)),
            pl.BlockSpec((1, k), lambda i, j: (0, 0)),
            pl.BlockSpec((k, tn), lambda i, j: (0, j)),
        ],
        out_specs=pl.BlockSpec((tm, tn), lambda i, j: (i, j)),
        out_shape=jax.ShapeDtypeStruct((m, n), BF16),
        scratch_shapes=[pltpu.VMEM((tm, k), BF16)],
        compiler_params=_params("parallel", "arbitrary"),
    )(x, gain.reshape(1, k), w)


def _sgu_kernel(u_ref, v_ref, lg_ref, lb_ref, ws_ref, bs_ref, o_ref, *, chunks):
    row = jax.lax.broadcasted_iota(jnp.int32, (GM_CHUNK, GM_CHUNK), 0)
    col = jax.lax.broadcasted_iota(jnp.int32, (GM_CHUNK, GM_CHUNK), 1)
    causal = row >= col
    for c in range(chunks):
        rows = slice(c * GM_CHUNK, (c + 1) * GM_CHUNK)
        v = v_ref[rows, :].astype(F32)
        mu = jnp.mean(v, axis=-1, keepdims=True)
        vc = v - mu
        var = jnp.mean(vc * vc, axis=-1, keepdims=True)
        vn = (vc * jax.lax.rsqrt(var + EPS) * lg_ref[...] + lb_ref[...]).astype(BF16)
        for h in range(GM_HEADS):
            cols = slice(h * GM_HEAD_DIM, (h + 1) * GM_HEAD_DIM)
            w = jnp.where(causal, ws_ref[h], jnp.zeros_like(ws_ref[h]))
            sv = _dot(w, vn[:, cols]) + bs_ref[h]
            o_ref[rows, cols] = (u_ref[rows, cols].astype(F32) * sv).astype(o_ref.dtype)


def _sgu(z, ln_g, ln_b, w_s, b_s, *, chunks):
    m = z.shape[0]
    w = D_MODEL
    tm = chunks * GM_CHUNK
    bs = jnp.broadcast_to(b_s[:, :, None], (GM_HEADS, GM_CHUNK, GM_HEAD_DIM)).astype(F32)
    return pl.pallas_call(
        functools.partial(_sgu_kernel, chunks=chunks),
        grid=(m // tm,),
        in_specs=[
            pl.BlockSpec((tm, w), lambda i: (i, 0)),
            pl.BlockSpec((tm, w), lambda i: (i, 1)),
            pl.BlockSpec((1, w), lambda i: (0, 0)),
            pl.BlockSpec((1, w), lambda i: (0, 0)),
            pl.BlockSpec((GM_HEADS, GM_CHUNK, GM_CHUNK), lambda i: (0, 0, 0)),
            pl.BlockSpec((GM_HEADS, GM_CHUNK, GM_HEAD_DIM), lambda i: (0, 0, 0)),
        ],
        out_specs=pl.BlockSpec((tm, w), lambda i: (i, 0)),
        out_shape=jax.ShapeDtypeStruct((m, w), BF16),
        compiler_params=_params("parallel"),
    )(z, z, ln_g.reshape(1, w), ln_b.reshape(1, w), w_s.astype(BF16), bs)


def _mm_res_kernel(a_ref, w_ref, r_ref, o_ref):
    o_ref[...] = r_ref[...] + _dot(a_ref[...], w_ref[...])


def _mm_res(a, w, res, *, tm, tn):
    m, k = a.shape
    n = w.shape[1]
    return pl.pallas_call(
        _mm_res_kernel,
        grid=(m // tm, n // tn),
        in_specs=[
            pl.BlockSpec((tm, k), lambda i, j: (i, 0)),
            pl.BlockSpec((k, tn), lambda i, j: (0, j)),
            pl.BlockSpec((tm, tn), lambda i, j: (i, j)),
        ],
        out_specs=pl.BlockSpec((tm, tn), lambda i, j: (i, j)),
        out_shape=jax.ShapeDtypeStruct((m, n), F32),
        compiler_params=_params("parallel", "parallel"),
    )(a, w, res)


def _mm_glu_res_kernel(a_ref, wv_ref, wg_ref, r_ref, o_ref):
    a = a_ref[...]
    val = _dot(a, wv_ref[...])
    gate = _dot(a, wg_ref[...])
    o_ref[...] = r_ref[...] + val * jax.nn.sigmoid(gate)


def _mm_glu_res(a, w, res, *, tm, tn):
    m, k = a.shape
    n = w.shape[1] // 2
    nj = n // tn
    return pl.pallas_call(
        _mm_glu_res_kernel,
        grid=(m // tm, nj),
        in_specs=[
            pl.BlockSpec((tm, k), lambda i, j: (i, 0)),
            pl.BlockSpec((k, tn), lambda i, j: (0, j)),
            pl.BlockSpec((k, tn), lambda i, j: (0, j + nj)),
            pl.BlockSpec((tm, tn), lambda i, j: (i, j)),
        ],
        out_specs=pl.BlockSpec((tm, tn), lambda i, j: (i, j)),
        out_shape=jax.ShapeDtypeStruct((m, n), F32),
        compiler_params=_params("parallel", "parallel"),
    )(a, w, w, res)


def _ffn_kernel(x_ref, g_ref, w1_ref, w3_ref, w2_ref, o_ref, h_scr, acc_scr):
    f = pl.program_id(1)

    @pl.when(f == 0)
    def _():
        h_scr[...] = _rms(x_ref[...], g_ref[...]).astype(BF16)
        acc_scr[...] = jnp.zeros_like(acc_scr)

    h = h_scr[...]
    a = jax.nn.silu(_dot(h, w1_ref[...])) * _dot(h, w3_ref[...])
    acc_scr[...] += _dot(a.astype(BF16), w2_ref[...])

    @pl.when(f == pl.num_programs(1) - 1)
    def _():
        o_ref[...] = x_ref[...] + acc_scr[...]


def _ffn(x, gain, w1, w3, w2, *, tm, tf):
    m, d = x.shape
    ff = w1.shape[1]
    return pl.pallas_call(
        _ffn_kernel,
        grid=(m // tm, ff // tf),
        in_specs=[
            pl.BlockSpec((tm, d), lambda i, f: (i, 0)),
            pl.BlockSpec((1, d), lambda i, f: (0, 0)),
            pl.BlockSpec((d, tf), lambda i, f: (0, f)),
            pl.BlockSpec((d, tf), lambda i, f: (0, f)),
            pl.BlockSpec((tf, d), lambda i, f: (f, 0)),
        ],
        out_specs=pl.BlockSpec((tm, d), lambda i, f: (i, 0)),
        out_shape=jax.ShapeDtypeStruct((m, d), F32),
        scratch_shapes=[pltpu.VMEM((tm, d), BF16), pltpu.VMEM((tm, d), F32)],
        compiler_params=_params("parallel", "arbitrary"),
    )(x, gain.reshape(1, d), w1, w3, w2)


def _ple_kernel(x_ref, g_ref, p_ref, wg_ref, wp_ref, o_ref, h_scr, *, tn):
    j = pl.program_id(1)

    @pl.when(j == 0)
    def _():
        h_scr[...] = _rms(x_ref[...], g_ref[...]).astype(BF16)

    gate = jax.nn.sigmoid(_dot(h_scr[...], wg_ref[...]))
    proj = _dot(p_ref[...].astype(BF16), wp_ref[...])
    xs = x_ref[:, pl.ds(pl.multiple_of(j * tn, tn), tn)]
    o_ref[...] = xs + gate * proj


def _ple(x, gain, p, wg, wp, *, tm, tn):
    m, d = x.shape
    return pl.pallas_call(
        functools.partial(_ple_kernel, tn=tn),
        grid=(m // tm, d // tn),
        in_specs=[
            pl.BlockSpec((tm, d), lambda i, j: (i, 0)),
            pl.BlockSpec((1, d), lambda i, j: (0, 0)),
            pl.BlockSpec((tm, PLE_DIM), lambda i, j: (i, 0)),
            pl.BlockSpec((d, tn), lambda i, j: (0, j)),
            pl.BlockSpec((PLE_DIM, tn), lambda i, j: (0, j)),
        ],
        out_specs=pl.BlockSpec((tm, tn), lambda i, j: (i, j)),
        out_shape=jax.ShapeDtypeStruct((m, d), F32),
        scratch_shapes=[pltpu.VMEM((tm, d), BF16)],
        compiler_params=_params("parallel", "arbitrary"),
    )(x, gain.reshape(1, d), p, wg, wp)


def _rms_kernel(x_ref, g_ref, o_ref):
    o_ref[...] = _rms(x_ref[...], g_ref[...])


def _rms_out(x, gain, *, tm):
    m, d = x.shape
    return pl.pallas_call(
        _rms_kernel,
        grid=(m // tm,),
        in_specs=[pl.BlockSpec((tm, d), lambda i: (i, 0)),
                  pl.BlockSpec((1, d), lambda i: (0, 0))],
        out_specs=pl.BlockSpec((tm, d), lambda i: (i, 0)),
        out_shape=jax.ShapeDtypeStruct((m, d), F32),
        compiler_params=_params("parallel"),
    )(x, gain.reshape(1, d))


def _s5_kernel(x_ref, mcat_ref, mout_ref, sc_ref, o_ref, *, chunks_per_seq):
    rows = x_ref.shape[1]
    n_idx = jax.lax.broadcasted_iota(jnp.int32, (rows, S5_RI), 0) % chunks_per_seq
    for gi in range(S5_GB):
        r = _dot(x_ref[gi], mcat_ref[gi])
        y = r[:, :S5_TC]
        s = r[:, S5_TC:]
        sc = sc_ref[gi]
        for k in range(S5_SCAN_STEPS):
            d = 1 << k
            sh = jnp.where(n_idx >= d, pltpu.roll(s, d, axis=0), 0.0)
            s = s + sc[k:k + 1, :] * sh + sc[8 + k:9 + k, :] * pltpu.roll(sh, S5_STATE, axis=1)
        s_prev = jnp.where(n_idx >= 1, pltpu.roll(s, 1, axis=0), 0.0)
        y = y + _dot(s_prev.astype(BF16), mout_ref[gi])
        o_ref[gi] = jax.nn.gelu(y).astype(o_ref.dtype)


def _s5_core(xg, mcat, mout, sc, *, chunks_per_seq):
    g, rows, _ = xg.shape
    return pl.pallas_call(
        functools.partial(_s5_kernel, chunks_per_seq=chunks_per_seq),
        grid=(g // S5_GB,),
        in_specs=[
            pl.BlockSpec((S5_GB, rows, S5_TC), lambda i: (i, 0, 0)),
            pl.BlockSpec((S5_GB, S5_TC, S5_TC + S5_RI), lambda i: (i, 0, 0)),
            pl.BlockSpec((S5_GB, S5_RI, S5_TC), lambda i: (i, 0, 0)),
            pl.BlockSpec((S5_GB, 16, S5_RI), lambda i: (i, 0, 0)),
        ],
        out_specs=pl.BlockSpec((S5_GB, rows, S5_TC), lambda i: (i, 0, 0)),
        out_shape=jax.ShapeDtypeStruct((g, rows, S5_TC), BF16),
        compiler_params=_params("parallel"),
    )(xg, mcat, mout, sc)


def _s5_tables(a_re, a_im, log_dt, b_re, b_im, c_re, c_im, d):
    hi = jax.lax.Precision.HIGHEST
    a_re, a_im = a_re.astype(F32), a_im.astype(F32)
    dt = jnp.exp(log_dt.astype(F32))[:, None]
    mag = jnp.exp(a_re * dt)
    lb_re, lb_im = mag * jnp.cos(a_im * dt), mag * jnp.sin(a_im * dt)
    den = a_re * a_re + a_im * a_im
    q_re = ((lb_re - 1.0) * a_re + lb_im * a_im) / den
    q_im = (lb_im * a_re - (lb_re - 1.0) * a_im) / den
    b_re, b_im = b_re.astype(F32), b_im.astype(F32)
    bb_re = q_re[..., None] * b_re - q_im[..., None] * b_im
    bb_im = q_re[..., None] * b_im + q_im[..., None] * b_re
    ks = jnp.arange(S5_T + 1, dtype=F32)[:, None, None]
    pmag = jnp.exp(ks * (a_re * dt))
    pk_re = pmag * jnp.cos(ks * (a_im * dt))
    pk_im = pmag * jnp.sin(ks * (a_im * dt))
    e_re = pk_re[:S5_T, :, :, None] * bb_re - pk_im[:S5_T, :, :, None] * bb_im
    e_im = pk_re[:S5_T, :, :, None] * bb_im + pk_im[:S5_T, :, :, None] * bb_re
    c_re, c_im = c_re.astype(F32), c_im.astype(F32)
    taps = (jnp.einsum('gcp,kgpd->gkcd', c_re, e_re, precision=hi)
            - jnp.einsum('gcp,kgpd->gkcd', c_im, e_im, precision=hi))
    t_idx = jnp.arange(S5_T)
    lag = t_idx[None, :] - t_idx[:, None]
    m_intra = jnp.where((lag >= 0)[None, :, :, None, None],
                        taps[:, jnp.clip(lag, 0, S5_T - 1)], 0.0)
    m_intra = m_intra.transpose(0, 1, 4, 2, 3).reshape(S5_GROUPS, S5_TC, S5_TC)
    skip = jnp.tile(d.astype(F32).reshape(S5_GROUPS, 1, S5_GROUP), (1, S5_T, 1)).reshape(S5_GROUPS, S5_TC)
    m_intra = m_intra + skip[:, :, None] * jnp.eye(S5_TC, dtype=F32)
    m_in = jnp.concatenate([e_re[::-1], e_im[::-1]], axis=2)
    m_in = m_in.transpose(1, 0, 3, 2).reshape(S5_GROUPS, S5_TC, S5_RI)
    mcat = jnp.concatenate([m_intra, m_in], axis=2).astype(BF16)
    w_re = c_re[None] * pk_re[1:, :, None, :] - c_im[None] * pk_im[1:, :, None, :]
    w_im = c_re[None] * pk_im[1:, :, None, :] + c_im[None] * pk_re[1:, :, None, :]
    mout = jnp.concatenate([w_re, -w_im], axis=3)
    mout = mout.transpose(1, 3, 0, 2).reshape(S5_GROUPS, S5_RI, S5_TC).astype(BF16)
    f_re, f_im = pk_re[S5_T], pk_im[S5_T]
    rows_a, rows_b = [], []
    for _ in range(S5_SCAN_STEPS):
        rows_a.append(jnp.concatenate([f_re, f_re], axis=1))
        rows_b.append(jnp.concatenate([-f_im, f_im], axis=1))
        f_re, f_im = f_re * f_re - f_im * f_im, 2.0 * f_re * f_im
    zero = jnp.zeros_like(rows_a[0])
    sc = jnp.stack(rows_a + [zero] + rows_b + [zero], axis=1)
    return mcat, mout, sc


def _s5_mixer(x2d, gain, w_in, tables, w_out, *, bsz, seq):
    m = bsz * seq
    n_chunks = seq // S5_T
    u = _rms_mm(x2d, gain, w_in, act=None, tm=512, tn=1024)
    xg = u.reshape(bsz, n_chunks, S5_T, S5_GROUPS, S5_GROUP)
    xg = xg.transpose(3, 0, 1, 2, 4).reshape(S5_GROUPS, bsz * n_chunks, S5_TC)
    yg = _s5_core(xg, *tables, chunks_per_seq=n_chunks)
    y = yg.reshape(S5_GROUPS, bsz, n_chunks, S5_T, S5_GROUP)
    y = y.transpose(1, 2, 3, 0, 4).reshape(m, D_MODEL)
    return _mm_glu_res(y, w_out, x2d, tm=1024, tn=512)


def kernel(x, p, norm_mix, norm_ffn, norm_ple, norm_final, gm_w_in, gm_ln_g, gm_ln_b, gm_w_s, gm_b_s, gm_w_out, s5_w_in, s5_a_re, s5_a_im, s5_log_dt, s5_b_re, s5_b_im, s5_c_re, s5_c_im, s5_d, s5_w_out, ffn_w1, ffn_w3, ffn_w2, ple_w_gate, ple_w_proj):
    bsz, seq, d = x.shape
    depth = p.shape[0]
    m = bsz * seq
    xs = x.reshape(m, d)
    for i in range(depth):
        j = i // 2
        if i % 2 == 0:
            z = _rms_mm(xs, norm_mix[i], gm_w_in[j].astype(BF16), act="gelu", tm=512, tn=1024)
            g = _sgu(z, gm_ln_g[j], gm_ln_b[j], gm_w_s[j], gm_b_s[j], chunks=2)
            xs = _mm_res(g, gm_w_out[j].astype(BF16), xs, tm=1024, tn=1024)
        else:
            tables = _s5_tables(s5_a_re[j], s5_a_im[j], s5_log_dt[j], s5_b_re[j], s5_b_im[j],
                                s5_c_re[j], s5_c_im[j], s5_d[j])
            xs = _s5_mixer(xs, norm_mix[i], s5_w_in[j].astype(BF16), tables,
                           s5_w_out[j].astype(BF16), bsz=bsz, seq=seq)
        xs = _ffn(xs, norm_ffn[i], ffn_w1[i].astype(BF16), ffn_w3[i].astype(BF16),
                  ffn_w2[i].astype(BF16), tm=512, tf=512)
        xs = _ple(xs, norm_ple[i], p[i].reshape(m, PLE_DIM), ple_w_gate[i].astype(BF16),
                  ple_w_proj[i].astype(BF16), tm=512, tn=1024)
    return _rms_out(xs, norm_final, tm=512).reshape(bsz, seq, d)
```

```python
import functools

import jax
import jax.numpy as jnp
from jax.experimental import pallas as pl
from jax.experimental.pallas import tpu as pltpu

F32 = jnp.float32
BF16 = jnp.bfloat16

D_MODEL = 2048
PLE_DIM = 256
EPS = 1e-6
LANES = 128

GM_CHUNK = 128
GM_HEAD_DIM = 128
GM_HEADS = D_MODEL // GM_HEAD_DIM

S5_GROUP = 16
S5_GROUPS = D_MODEL // S5_GROUP
S5_STATE = 64
S5_T = 16
S5_TC = S5_T * S5_GROUP
S5_RI = 2 * S5_STATE
S5_GB = LANES // S5_GROUP
S5_TILES = D_MODEL // LANES
S5_SCAN_STEPS = 7

W_CHUNKS = 16
VMEM_LIMIT = 54 * 1024 * 1024


def _params(*sem):
    return pltpu.CompilerParams(dimension_semantics=sem, vmem_limit_bytes=VMEM_LIMIT)


def _rms(xf, gain):
    ms = jnp.mean(xf * xf, axis=-1, keepdims=True)
    return xf * jax.lax.rsqrt(ms + EPS) * gain


def _dot(a, b):
    return jnp.dot(a, b, preferred_element_type=F32)


def _rms_mm_gelu_kernel(x_ref, g_ref, w_ref, o_ref, h_scr):
    @pl.when(pl.program_id(1) == 0)
    def _():
        h_scr[...] = _rms(x_ref[...], g_ref[...]).astype(BF16)

    acc = _dot(h_scr[...], w_ref[...].astype(BF16))
    o_ref[...] = jax.nn.gelu(acc).astype(o_ref.dtype)


def _rms_mm_gelu(x, gain, w, layer, *, tm, tn):
    m, k = x.shape
    n = w.shape[2]
    return pl.pallas_call(
        _rms_mm_gelu_kernel,
        grid=(m // tm, n // tn),
        in_specs=[
            pl.BlockSpec((tm, k), lambda i, j: (i, 0)),
            pl.BlockSpec((1, k), lambda i, j: (0, 0)),
            pl.BlockSpec((None, k, tn), lambda i, j: (layer, 0, j)),
        ],
        out_specs=pl.BlockSpec((tm, tn), lambda i, j: (i, j)),
        out_shape=jax.ShapeDtypeStruct((m, n), BF16),
        scratch_shapes=[pltpu.VMEM((tm, k), BF16)],
        compiler_params=_params("parallel", "arbitrary"),
        name="gm_in",
    )(x, gain.reshape(1, k), w)


def _sgu_kernel(u_ref, v_ref, lg_ref, lb_ref, ws_ref, bs_ref, o_ref, *, chunks):
    row = jax.lax.broadcasted_iota(jnp.int32, (GM_CHUNK, GM_CHUNK), 0)
    col = jax.lax.broadcasted_iota(jnp.int32, (GM_CHUNK, GM_CHUNK), 1)
    causal = row >= col
    for c in range(chunks):
        rows = slice(c * GM_CHUNK, (c + 1) * GM_CHUNK)
        v = v_ref[rows, :].astype(F32)
        mu = jnp.mean(v, axis=-1, keepdims=True)
        vc = v - mu
        var = jnp.mean(vc * vc, axis=-1, keepdims=True)
        vn = (vc * jax.lax.rsqrt(var + EPS) * lg_ref[...] + lb_ref[...]).astype(BF16)
        for h in range(GM_HEADS):
            cols = slice(h * GM_HEAD_DIM, (h + 1) * GM_HEAD_DIM)
            w = jnp.where(causal, ws_ref[h], 0.0).astype(BF16)
            sv = _dot(w, vn[:, cols]) + bs_ref[h]
            o_ref[rows, cols] = (u_ref[rows, cols].astype(F32) * sv).astype(o_ref.dtype)


def _sgu(z, ln_g, ln_b, w_s, b_s, layer, *, chunks):
    m = z.shape[0]
    w = D_MODEL
    tm = chunks * GM_CHUNK
    bs = jnp.broadcast_to(b_s[layer][:, :, None], (GM_HEADS, GM_CHUNK, GM_HEAD_DIM)).astype(F32)
    return pl.pallas_call(
        functools.partial(_sgu_kernel, chunks=chunks),
        grid=(m // tm,),
        in_specs=[
            pl.BlockSpec((tm, w), lambda i: (i, 0)),
            pl.BlockSpec((tm, w), lambda i: (i, 1)),
            pl.BlockSpec((1, w), lambda i: (layer, 0)),
            pl.BlockSpec((1, w), lambda i: (layer, 0)),
            pl.BlockSpec((None, GM_HEADS, GM_CHUNK, GM_CHUNK), lambda i: (layer, 0, 0, 0)),
            pl.BlockSpec((GM_HEADS, GM_CHUNK, GM_HEAD_DIM), lambda i: (0, 0, 0)),
        ],
        out_specs=pl.BlockSpec((tm, w), lambda i: (i, 0)),
        out_shape=jax.ShapeDtypeStruct((m, w), BF16),
        compiler_params=_params("parallel"),
        name="sgu",
    )(z, z, ln_g, ln_b, w_s, bs)


def _resident_call(body, *, name, m, tm, weights, row_ins, consts, outs):
    nw, nr, nc, no = len(weights), len(row_ins), len(consts), len(outs)

    def kern(*refs):
        w_refs = refs[:nw]
        r_refs = refs[nw:nw + nr]
        c_refs = refs[nw + nr:nw + nr + nc]
        o_refs = refs[nw + nr + nc:nw + nr + nc + no]
        wb_refs = refs[nw + nr + nc + no:]
        s = pl.program_id(0)

        @pl.when(s < W_CHUNKS)
        def _():
            for w_ref, wb_ref in zip(w_refs, wb_refs):
                rows = w_ref.shape[0]
                wb_ref[pl.ds(pl.multiple_of(s * rows, rows), rows), :] = w_ref[...].astype(BF16)

        @pl.when(s >= W_CHUNKS)
        def _():
            body(wb_refs, r_refs, c_refs, o_refs)

    def tile(s):
        return jnp.maximum(s - W_CHUNKS, 0)

    in_specs, args, scratch = [], [], []
    for arr, layer in weights:
        _, k, n = arr.shape
        in_specs.append(pl.BlockSpec(
            (None, k // W_CHUNKS, n),
            lambda s, layer=layer: (layer, jnp.minimum(s, W_CHUNKS - 1), 0)))
        args.append(arr)
        scratch.append(pltpu.VMEM((k, n), BF16))
    for arr, block, index_fn in row_ins:
        in_specs.append(pl.BlockSpec(block, lambda s, f=index_fn: f(tile(s))))
        args.append(arr)
    for arr in consts:
        in_specs.append(pl.BlockSpec((1, arr.shape[1]), lambda s: (0, 0)))
        args.append(arr)
    out_specs = [pl.BlockSpec(block, lambda s, f=index_fn: f(tile(s))) for _, _, block, index_fn in outs]
    out_shape = [jax.ShapeDtypeStruct(shape, dtype) for shape, dtype, _, _ in outs]
    return pl.pallas_call(
        kern,
        grid=(W_CHUNKS + m // tm,),
        in_specs=in_specs,
        out_specs=out_specs,
        out_shape=out_shape,
        scratch_shapes=scratch,
        compiler_params=_params("arbitrary"),
        name=name,
    )(*args)


def _rows(tm, n):
    return (tm, n), (lambda i: (i, 0))


def _mm_res_norm(a, w, layer, res, gain, *, tm):
    m, n = res.shape

    def body(wb, r, c, o):
        xn = r[1][...] + _dot(r[0][...], wb[0][...])
        o[0][...] = xn
        o[1][...] = _rms(xn, c[0][...]).astype(BF16)

    blk_a, idx = _rows(tm, a.shape[1])
    blk_x, _ = _rows(tm, n)
    return _resident_call(
        body, name="mm_res_norm", m=m, tm=tm, weights=[(w, layer)],
        row_ins=[(a, blk_a, idx), (res, blk_x, idx)], consts=[gain.reshape(1, n)],
        outs=[((m, n), F32, blk_x, idx), ((m, n), BF16, blk_x, idx)])


def _ffn_down(a, w2, layer, res, *, tm):
    m, n = res.shape

    def body(wb, r, c, o):
        o[0][...] = r[1][...] + _dot(r[0][...], wb[0][...])

    blk_a, idx = _rows(tm, a.shape[1])
    blk_x, _ = _rows(tm, n)
    return _resident_call(
        body, name="ffn_down", m=m, tm=tm, weights=[(w2, layer)],
        row_ins=[(a, blk_a, idx), (res, blk_x, idx)], consts=[],
        outs=[((m, n), F32, blk_x, idx)])[0]


def _ple(x, gain, p, wg, wp, layer, final_gain, *, tm):
    m, n = x.shape

    def body(wb, r, c, o):
        xf = r[0][...]
        h = _rms(xf, c[0][...]).astype(BF16)
        gate = jax.nn.sigmoid(_dot(h, wb[0][...]))
        proj = _dot(r[1][...].astype(BF16), wb[1][...])
        xn = xf + gate * proj
        if final_gain is not None:
            xn = _rms(xn, c[1][...])
        o[0][...] = xn

    blk_x, idx = _rows(tm, n)
    blk_p, _ = _rows(tm, PLE_DIM)
    consts = [gain.reshape(1, n)] + ([] if final_gain is None else [final_gain.reshape(1, n)])
    return _resident_call(
        body, name="ple", m=m, tm=tm, weights=[(wg, layer), (wp, layer)],
        row_ins=[(x, blk_x, idx), (p, (None,) + blk_p, lambda i: (layer, i, 0))], consts=consts,
        outs=[((m, n), F32, blk_x, idx)])[0]


def _s5_in(x, gain, w, layer, *, tm):
    m, n = x.shape

    def body(wb, r, c, o):
        h = _rms(r[0][...], c[0][...]).astype(BF16)
        u = _dot(h, wb[0][...])
        for l in range(S5_TILES):
            o[0][l] = u[:, l * LANES:(l + 1) * LANES].astype(BF16)

    blk_x, idx = _rows(tm, n)
    return _resident_call(
        body, name="s5_in", m=m, tm=tm, weights=[(w, layer)],
        row_ins=[(x, blk_x, idx)], consts=[gain.reshape(1, n)],
        outs=[((S5_TILES, m, LANES), BF16, (S5_TILES, tm, LANES), lambda i: (0, i, 0))])[0]


def _s5_out(y_tiles, w, layer, res, gain, *, tm):
    m, n = res.shape

    def body(wb, r, c, o):
        y = jnp.concatenate([r[0][l] for l in range(S5_TILES)], axis=1)
        vg = _dot(y, wb[0][...])
        xn = r[1][...] + vg[:, :n] * jax.nn.sigmoid(vg[:, n:])
        o[0][...] = xn
        o[1][...] = _rms(xn, c[0][...]).astype(BF16)

    blk_x, idx = _rows(tm, n)
    return _resident_call(
        body, name="s5_out", m=m, tm=tm, weights=[(w, layer)],
        row_ins=[(y_tiles, (S5_TILES, tm, LANES), lambda i: (0, i, 0)), (res, blk_x, idx)],
        consts=[gain.reshape(1, n)],
        outs=[((m, n), F32, blk_x, idx), ((m, n), BF16, blk_x, idx)])


def _ffn_up_kernel(h_ref, w1_ref, w3_ref, o_ref, w1_scr, w3_scr):
    @pl.when(pl.program_id(1) == 0)
    def _():
        w1_scr[...] = w1_ref[...].astype(BF16)
        w3_scr[...] = w3_ref[...].astype(BF16)

    h = h_ref[...]
    o_ref[...] = (jax.nn.silu(_dot(h, w1_scr[...])) * _dot(h, w3_scr[...])).astype(o_ref.dtype)


def _ffn_up(h, w1, w3, layer, *, tm, tf):
    m, d = h.shape
    ff = w1.shape[2]
    return pl.pallas_call(
        _ffn_up_kernel,
        grid=(ff // tf, m // tm),
        in_specs=[
            pl.BlockSpec((tm, d), lambda f, i: (i, 0)),
            pl.BlockSpec((None, d, tf), lambda f, i: (layer, 0, f)),
            pl.BlockSpec((None, d, tf), lambda f, i: (layer, 0, f)),
        ],
        out_specs=pl.BlockSpec((tm, tf), lambda f, i: (i, f)),
        out_shape=jax.ShapeDtypeStruct((m, ff), BF16),
        scratch_shapes=[pltpu.VMEM((d, tf), BF16), pltpu.VMEM((d, tf), BF16)],
        compiler_params=_params("arbitrary", "arbitrary"),
        name="ffn_up",
    )(h, w1, w3)


def _piece_transpose(v):
    piece = jax.lax.broadcasted_iota(jnp.int32, v[0].shape, 1) // S5_GROUP
    for delta in (4, 2, 1):
        keep = (piece & delta) == 0
        shift = delta * S5_GROUP
        new = list(v)
        for i in range(S5_GB):
            if i & delta == 0:
                a, b = v[i], v[i + delta]
                new[i] = jnp.where(keep, a, pltpu.roll(b, shift, axis=1))
                new[i + delta] = jnp.where(keep, pltpu.roll(a, LANES - shift, axis=1), b)
        v = new
    return v


def _s5_kernel(x_ref, mcat_ref, mout_ref, sc_ref, o_ref, *, chunks_per_seq):
    rows = x_ref.shape[1]
    n_idx = jax.lax.broadcasted_iota(jnp.int32, (rows, S5_RI), 0) % chunks_per_seq
    halves = S5_T // S5_GB
    xin = [_piece_transpose([x_ref[0, :, (S5_GB * h + i) * LANES:(S5_GB * h + i + 1) * LANES].astype(F32)
                             for i in range(S5_GB)]) for h in range(halves)]
    ys = []
    for gi in range(S5_GB):
        xg = jnp.concatenate([xin[h][gi] for h in range(halves)], axis=1).astype(BF16)
        r = _dot(xg, mcat_ref[gi])
        y = r[:, :S5_TC]
        s = r[:, S5_TC:]
        sc = sc_ref[gi]
        for k in range(S5_SCAN_STEPS):
            d = 1 << k
            sh = jnp.where(n_idx >= d, pltpu.roll(s, d, axis=0), 0.0)
            s = s + sc[k:k + 1, :] * sh + sc[8 + k:9 + k, :] * pltpu.roll(sh, S5_STATE, axis=1)
        s_prev = jnp.where(n_idx >= 1, pltpu.roll(s, 1, axis=0), 0.0)
        y = y + _dot(s_prev.astype(BF16), mout_ref[gi])
        ys.append(jax.nn.gelu(y))
    for h in range(halves):
        out = _piece_transpose([ys[gi][:, h * LANES:(h + 1) * LANES] for gi in range(S5_GB)])
        for i in range(S5_GB):
            t = S5_GB * h + i
            o_ref[0, :, t * LANES:(t + 1) * LANES] = out[i].astype(o_ref.dtype)


def _s5_core(x_tiles, mcat, mout, sc, *, chunks_per_seq):
    tiles, rows, width = x_tiles.shape
    return pl.pallas_call(
        functools.partial(_s5_kernel, chunks_per_seq=chunks_per_seq),
        grid=(tiles,),
        in_specs=[
            pl.BlockSpec((1, rows, width), lambda i: (i, 0, 0)),
            pl.BlockSpec((S5_GB, S5_TC, S5_TC + S5_RI), lambda i: (i, 0, 0)),
            pl.BlockSpec((S5_GB, S5_RI, S5_TC), lambda i: (i, 0, 0)),
            pl.BlockSpec((S5_GB, 16, S5_RI), lambda i: (i, 0, 0)),
        ],
        out_specs=pl.BlockSpec((1, rows, width), lambda i: (i, 0, 0)),
        out_shape=jax.ShapeDtypeStruct((tiles, rows, width), BF16),
        compiler_params=_params("parallel"),
        name="s5_core",
    )(x_tiles, mcat, mout, sc)


def _s5_tables(a_re, a_im, log_dt, b_re, b_im, c_re, c_im, d):
    hi = jax.lax.Precision.HIGHEST
    a_re, a_im = a_re.astype(F32), a_im.astype(F32)
    dt = jnp.exp(log_dt.astype(F32))[:, None]
    mag = jnp.exp(a_re * dt)
    lb_re, lb_im = mag * jnp.cos(a_im * dt), mag * jnp.sin(a_im * dt)
    den = a_re * a_re + a_im * a_im
    q_re = ((lb_re - 1.0) * a_re + lb_im * a_im) / den
    q_im = (lb_im * a_re - (lb_re - 1.0) * a_im) / den
    b_re, b_im = b_re.astype(F32), b_im.astype(F32)
    bb_re = q_re[..., None] * b_re - q_im[..., None] * b_im
    bb_im = q_re[..., None] * b_im + q_im[..., None] * b_re
    ks = jnp.arange(S5_T + 1, dtype=F32)[:, None, None]
    pmag = jnp.exp(ks * (a_re * dt))
    pk_re = pmag * jnp.cos(ks * (a_im * dt))
    pk_im = pmag * jnp.sin(ks * (a_im * dt))
    e_re = pk_re[:S5_T, :, :, None] * bb_re - pk_im[:S5_T, :, :, None] * bb_im
    e_im = pk_re[:S5_T, :, :, None] * bb_im + pk_im[:S5_T, :, :, None] * bb_re
    c_re, c_im = c_re.astype(F32), c_im.astype(F32)
    taps = (jnp.einsum('gcp,kgpd->gkcd', c_re, e_re, precision=hi)
            - jnp.einsum('gcp,kgpd->gkcd', c_im, e_im, precision=hi))
    ext = jnp.concatenate([taps, jnp.zeros((S5_GROUPS, S5_T + 1, S5_GROUP, S5_GROUP), F32)], axis=1)
    toe = jnp.tile(ext, (1, S5_T, 1, 1))[:, :2 * S5_T * S5_T]
    toe = toe.reshape(S5_GROUPS, S5_T, 2 * S5_T, S5_GROUP, S5_GROUP)[:, :, :S5_T]
    m_intra = toe.transpose(0, 1, 4, 2, 3).reshape(S5_GROUPS, S5_TC, S5_TC)
    skip = jnp.tile(d.astype(F32).reshape(S5_GROUPS, 1, S5_GROUP), (1, S5_T, 1)).reshape(S5_GROUPS, S5_TC)
    m_intra = m_intra + skip[:, :, None] * jnp.eye(S5_TC, dtype=F32)
    m_in = jnp.concatenate([e_re[::-1], e_im[::-1]], axis=2)
    m_in = m_in.transpose(1, 0, 3, 2).reshape(S5_GROUPS, S5_TC, S5_RI)
    mcat = jnp.concatenate([m_intra, m_in], axis=2).astype(BF16)
    w_re = c_re[None] * pk_re[1:, :, None, :] - c_im[None] * pk_im[1:, :, None, :]
    w_im = c_re[None] * pk_im[1:, :, None, :] + c_im[None] * pk_re[1:, :, None, :]
    mout = jnp.concatenate([w_re, -w_im], axis=3)
    mout = mout.transpose(1, 3, 0, 2).reshape(S5_GROUPS, S5_RI, S5_TC).astype(BF16)
    f_re, f_im = pk_re[S5_T], pk_im[S5_T]
    rows_a, rows_b = [], []
    for _ in range(S5_SCAN_STEPS):
        rows_a.append(jnp.concatenate([f_re, f_re], axis=1))
        rows_b.append(jnp.concatenate([-f_im, f_im], axis=1))
        f_re, f_im = f_re * f_re - f_im * f_im, 2.0 * f_re * f_im
    zero = jnp.zeros_like(rows_a[0])
    sc = jnp.stack(rows_a + [zero] + rows_b + [zero], axis=1)
    return mcat, mout, sc


def kernel(x, p, norm_mix, norm_ffn, norm_ple, norm_final, gm_w_in, gm_ln_g, gm_ln_b, gm_w_s, gm_b_s, gm_w_out, s5_w_in, s5_a_re, s5_a_im, s5_log_dt, s5_b_re, s5_b_im, s5_c_re, s5_c_im, s5_d, s5_w_out, ffn_w1, ffn_w3, ffn_w2, ple_w_gate, ple_w_proj):
    bsz, seq, d = x.shape
    depth = p.shape[0]
    m = bsz * seq
    n_chunks = seq // S5_T
    xs = x.reshape(m, d)
    ps = p.reshape(depth, m, PLE_DIM)
    for i in range(depth):
        j = i // 2
        if i % 2 == 0:
            z = _rms_mm_gelu(xs, norm_mix[i], gm_w_in, j, tm=1024, tn=512)
            g = _sgu(z, gm_ln_g, gm_ln_b, gm_w_s, gm_b_s, j, chunks=2)
            xs, hn = _mm_res_norm(g, gm_w_out, j, xs, norm_ffn[i], tm=512)
        else:
            tables = _s5_tables(s5_a_re[j], s5_a_im[j], s5_log_dt[j], s5_b_re[j], s5_b_im[j],
                                s5_c_re[j], s5_c_im[j], s5_d[j])
            u = _s5_in(xs, norm_mix[i], s5_w_in, j, tm=512)
            u = u.reshape(S5_TILES, bsz * n_chunks, S5_T * LANES)
            y = _s5_core(u, *tables, chunks_per_seq=n_chunks)
            y = y.reshape(S5_TILES, m, LANES)
            xs, hn = _s5_out(y, s5_w_out, j, xs, norm_ffn[i], tm=256)
        a = _ffn_up(hn, ffn_w1, ffn_w3, i, tm=1024, tf=512)
        xs = _ffn_down(a, ffn_w2, i, xs, tm=256)
        xs = _ple(xs, norm_ple[i], ps, ple_w_gate, ple_w_proj, i,
                  norm_final if i == depth - 1 else None, tm=512)
    return xs.reshape(bsz, seq, d)
```

```python
import functools

import jax
import jax.numpy as jnp
from jax.experimental import pallas as pl
from jax.experimental.pallas import tpu as pltpu

F32 = jnp.float32
BF16 = jnp.bfloat16

D_MODEL = 2048
PLE_DIM = 256
EPS = 1e-6
LANES = 128

GM_CHUNK = 128
GM_HEAD_DIM = 128
GM_HEADS = D_MODEL // GM_HEAD_DIM

S5_GROUP = 16
S5_GROUPS = D_MODEL // S5_GROUP
S5_STATE = 64
S5_T = 16
S5_TC = S5_T * S5_GROUP
S5_RI = 2 * S5_STATE
S5_GB = LANES // S5_GROUP
S5_TILES = D_MODEL // LANES
S5_SCAN_STEPS = 7

W_CHUNKS = 16
SUB_ROWS = 256
VMEM_LIMIT = 56 * 1024 * 1024


def _params(*sem):
    return pltpu.CompilerParams(dimension_semantics=sem, vmem_limit_bytes=VMEM_LIMIT)


def _rms(xf, gain):
    ms = jnp.mean(xf * xf, axis=-1, keepdims=True)
    return xf * jax.lax.rsqrt(ms + EPS) * gain


def _dot(a, b):
    return jnp.dot(a, b, preferred_element_type=F32)


def _sub_blocks(tm):
    sub = min(tm, SUB_ROWS)
    return [slice(r, r + sub) for r in range(0, tm, sub)]


def _sgu_kernel(u_ref, v_ref, lg_ref, lb_ref, ws_ref, bs_ref, o_ref, *, chunks):
    row = jax.lax.broadcasted_iota(jnp.int32, (GM_CHUNK, GM_CHUNK), 0)
    col = jax.lax.broadcasted_iota(jnp.int32, (GM_CHUNK, GM_CHUNK), 1)
    causal = row >= col
    for c in range(chunks):
        rows = slice(c * GM_CHUNK, (c + 1) * GM_CHUNK)
        v = v_ref[rows, :].astype(F32)
        mu = jnp.mean(v, axis=-1, keepdims=True)
        vc = v - mu
        var = jnp.mean(vc * vc, axis=-1, keepdims=True)
        vn = (vc * jax.lax.rsqrt(var + EPS) * lg_ref[...] + lb_ref[...]).astype(BF16)
        for h in range(GM_HEADS):
            cols = slice(h * GM_HEAD_DIM, (h + 1) * GM_HEAD_DIM)
            w = jnp.where(causal, ws_ref[h], 0.0).astype(BF16)
            sv = _dot(w, vn[:, cols]) + bs_ref[h]
            o_ref[rows, cols] = (u_ref[rows, cols].astype(F32) * sv).astype(o_ref.dtype)


def _sgu(z, ln_g, ln_b, w_s, b_s, layer, *, chunks):
    m = z.shape[0]
    w = D_MODEL
    tm = chunks * GM_CHUNK
    bs = jnp.broadcast_to(b_s[layer][:, :, None], (GM_HEADS, GM_CHUNK, GM_HEAD_DIM)).astype(F32)
    return pl.pallas_call(
        functools.partial(_sgu_kernel, chunks=chunks),
        grid=(m // tm,),
        in_specs=[
            pl.BlockSpec((tm, w), lambda i: (i, 0)),
            pl.BlockSpec((tm, w), lambda i: (i, 1)),
            pl.BlockSpec((1, w), lambda i: (layer, 0)),
            pl.BlockSpec((1, w), lambda i: (layer, 0)),
            pl.BlockSpec((None, GM_HEADS, GM_CHUNK, GM_CHUNK), lambda i: (layer, 0, 0, 0)),
            pl.BlockSpec((GM_HEADS, GM_CHUNK, GM_HEAD_DIM), lambda i: (0, 0, 0)),
        ],
        out_specs=pl.BlockSpec((tm, w), lambda i: (i, 0)),
        out_shape=jax.ShapeDtypeStruct((m, w), BF16),
        compiler_params=_params("parallel"),
        name="sgu",
    )(z, z, ln_g, ln_b, w_s, bs)


def _resident_call(body, *, name, m, tm, weights, row_ins, consts, outs, scratch=()):
    nw, nr, nc, no, ns = len(weights), len(row_ins), len(consts), len(outs), len(scratch)

    def kern(*refs):
        w_refs = refs[:nw]
        r_refs = refs[nw:nw + nr]
        c_refs = refs[nw + nr:nw + nr + nc]
        o_refs = refs[nw + nr + nc:nw + nr + nc + no]
        wb_refs = refs[nw + nr + nc + no:nw + nr + nc + no + nw]
        s_refs = refs[nw + nr + nc + no + nw:]
        s = pl.program_id(0)

        @pl.when(s < W_CHUNKS)
        def _():
            for w_ref, wb_ref in zip(w_refs, wb_refs):
                rows = w_ref.shape[0]
                wb_ref[pl.ds(pl.multiple_of(s * rows, rows), rows), :] = w_ref[...].astype(BF16)

        @pl.when(s >= W_CHUNKS)
        def _():
            body(wb_refs, r_refs, c_refs, o_refs, s_refs)

    def tile(s):
        return jnp.maximum(s - W_CHUNKS, 0)

    in_specs, args, wscratch = [], [], []
    for arr, layer in weights:
        _, k, n = arr.shape
        in_specs.append(pl.BlockSpec(
            (None, k // W_CHUNKS, n),
            lambda s, layer=layer: (layer, jnp.minimum(s, W_CHUNKS - 1), 0)))
        args.append(arr)
        wscratch.append(pltpu.VMEM((k, n), BF16))
    for arr, block, index_fn in row_ins:
        in_specs.append(pl.BlockSpec(block, lambda s, f=index_fn: f(tile(s))))
        args.append(arr)
    for arr in consts:
        in_specs.append(pl.BlockSpec((1, arr.shape[1]), lambda s: (0, 0)))
        args.append(arr)
    out_specs = [pl.BlockSpec(block, lambda s, f=index_fn: f(tile(s))) for _, _, block, index_fn in outs]
    out_shape = [jax.ShapeDtypeStruct(shape, dtype) for shape, dtype, _, _ in outs]
    return pl.pallas_call(
        kern,
        grid=(W_CHUNKS + m // tm,),
        in_specs=in_specs,
        out_specs=out_specs,
        out_shape=out_shape,
        scratch_shapes=wscratch + list(scratch),
        compiler_params=_params("arbitrary"),
        name=name,
    )(*args)


def _rows(tm, n):
    return (tm, n), (lambda i: (i, 0))


def _gm_in(x, gain, w, layer, *, tm):
    m, k = x.shape
    n = w.shape[2]

    def body(wb, r, c, o, scr):
        for rs in _sub_blocks(tm):
            h = _rms(r[0][rs, :], c[0][...]).astype(BF16)
            o[0][rs, :] = jax.nn.gelu(_dot(h, wb[0][...])).astype(BF16)

    blk_x, idx = _rows(tm, k)
    blk_o, _ = _rows(tm, n)
    return _resident_call(
        body, name="gm_in", m=m, tm=tm, weights=[(w, layer)],
        row_ins=[(x, blk_x, idx)], consts=[gain.reshape(1, k)],
        outs=[((m, n), BF16, blk_o, idx)])[0]


def _mm_res_norm(a, w, layer, res, gain, *, tm):
    m, n = res.shape

    def body(wb, r, c, o, scr):
        for rs in _sub_blocks(tm):
            xn = r[1][rs, :] + _dot(r[0][rs, :], wb[0][...])
            o[0][rs, :] = xn
            o[1][rs, :] = _rms(xn, c[0][...]).astype(BF16)

    blk_a, idx = _rows(tm, a.shape[1])
    blk_x, _ = _rows(tm, n)
    return _resident_call(
        body, name="mm_res_norm", m=m, tm=tm, weights=[(w, layer)],
        row_ins=[(a, blk_a, idx), (res, blk_x, idx)], consts=[gain.reshape(1, n)],
        outs=[((m, n), F32, blk_x, idx), ((m, n), BF16, blk_x, idx)])


def _ffn_down(a, w2, layer, res, *, tm):
    m, n = res.shape

    def body(wb, r, c, o, scr):
        for rs in _sub_blocks(tm):
            o[0][rs, :] = r[1][rs, :] + _dot(r[0][rs, :], wb[0][...])

    blk_a, idx = _rows(tm, a.shape[1])
    blk_x, _ = _rows(tm, n)
    return _resident_call(
        body, name="ffn_down", m=m, tm=tm, weights=[(w2, layer)],
        row_ins=[(a, blk_a, idx), (res, blk_x, idx)], consts=[],
        outs=[((m, n), F32, blk_x, idx)])[0]


def _ple(x, gain, p, wg, wp, layer, final_gain, *, tm):
    m, n = x.shape

    def body(wb, r, c, o, scr):
        for rs in _sub_blocks(tm):
            xf = r[0][rs, :]
            h = _rms(xf, c[0][...]).astype(BF16)
            gate = jax.nn.sigmoid(_dot(h, wb[0][...]))
            proj = _dot(r[1][rs, :].astype(BF16), wb[1][...])
            xn = xf + gate * proj
            if final_gain is not None:
                xn = _rms(xn, c[1][...])
            o[0][rs, :] = xn

    blk_x, idx = _rows(tm, n)
    blk_p, _ = _rows(tm, PLE_DIM)
    consts = [gain.reshape(1, n)] + ([] if final_gain is None else [final_gain.reshape(1, n)])
    return _resident_call(
        body, name="ple", m=m, tm=tm, weights=[(wg, layer), (wp, layer)],
        row_ins=[(x, blk_x, idx), (p, (None,) + blk_p, lambda i: (layer, i, 0))], consts=consts,
        outs=[((m, n), F32, blk_x, idx)])[0]


def _s5_in(x, gain, w, layer, *, tm):
    m, n = x.shape
    tc = tm // S5_T

    def body(wb, r, c, o, scr):
        for rs in _sub_blocks(tm):
            h = _rms(r[0][rs, :], c[0][...]).astype(BF16)
            u = _dot(h, wb[0][...])
            for l in range(S5_TILES):
                scr[0][l, rs, :] = u[:, l * LANES:(l + 1) * LANES]
        for t in range(S5_T):
            for l in range(S5_TILES):
                o[0][l, :, t * LANES:(t + 1) * LANES] = scr[0][l, pl.ds(t, tc, stride=S5_T), :].astype(BF16)

    blk_x, idx = _rows(tm, n)
    return _resident_call(
        body, name="s5_in", m=m, tm=tm, weights=[(w, layer)],
        row_ins=[(x, blk_x, idx)], consts=[gain.reshape(1, n)],
        outs=[((S5_TILES, m // S5_T, S5_T * LANES), BF16, (S5_TILES, tc, S5_T * LANES), lambda i: (0, i, 0))],
        scratch=[pltpu.VMEM((S5_TILES, tm, LANES), F32)])[0]


def _s5_out(y_tiles, w, layer, res, gain, *, tm):
    m, n = res.shape
    tc = tm // S5_T

    def body(wb, r, c, o, scr):
        for t in range(S5_T):
            for l in range(S5_TILES):
                scr[0][l, pl.ds(t, tc, stride=S5_T), :] = r[0][l, :, t * LANES:(t + 1) * LANES].astype(F32)
        for rs in _sub_blocks(tm):
            y = jnp.concatenate([scr[0][l, rs, :] for l in range(S5_TILES)], axis=1).astype(BF16)
            vg = _dot(y, wb[0][...])
            xn = r[1][rs, :] + vg[:, :n] * jax.nn.sigmoid(vg[:, n:])
            o[0][rs, :] = xn
            o[1][rs, :] = _rms(xn, c[0][...]).astype(BF16)

    blk_x, idx = _rows(tm, n)
    return _resident_call(
        body, name="s5_out", m=m, tm=tm, weights=[(w, layer)],
        row_ins=[(y_tiles, (S5_TILES, tc, S5_T * LANES), lambda i: (0, i, 0)), (res, blk_x, idx)],
        consts=[gain.reshape(1, n)],
        outs=[((m, n), F32, blk_x, idx), ((m, n), BF16, blk_x, idx)],
        scratch=[pltpu.VMEM((S5_TILES, tm, LANES), F32)])


def _ffn_up_kernel(h_ref, w1_ref, w3_ref, o_ref, w1_scr, w3_scr, *, tm):
    @pl.when(pl.program_id(1) == 0)
    def _():
        w1_scr[...] = w1_ref[...].astype(BF16)
        w3_scr[...] = w3_ref[...].astype(BF16)

    for rs in _sub_blocks(tm):
        h = h_ref[rs, :]
        o_ref[rs, :] = (jax.nn.silu(_dot(h, w1_scr[...])) * _dot(h, w3_scr[...])).astype(o_ref.dtype)


def _ffn_up(h, w1, w3, layer, *, tm, tf):
    m, d = h.shape
    ff = w1.shape[2]
    return pl.pallas_call(
        functools.partial(_ffn_up_kernel, tm=tm),
        grid=(ff // tf, m // tm),
        in_specs=[
            pl.BlockSpec((tm, d), lambda f, i: (i, 0)),
            pl.BlockSpec((None, d, tf), lambda f, i: (layer, 0, f)),
            pl.BlockSpec((None, d, tf), lambda f, i: (layer, 0, f)),
        ],
        out_specs=pl.BlockSpec((tm, tf), lambda f, i: (i, f)),
        out_shape=jax.ShapeDtypeStruct((m, ff), BF16),
        scratch_shapes=[pltpu.VMEM((d, tf), BF16), pltpu.VMEM((d, tf), BF16)],
        compiler_params=_params("arbitrary", "arbitrary"),
        name="ffn_up",
    )(h, w1, w3)


def _piece_transpose(v):
    piece = jax.lax.broadcasted_iota(jnp.int32, v[0].shape, 1) // S5_GROUP
    for delta in (4, 2, 1):
        keep = (piece & delta) == 0
        shift = delta * S5_GROUP
        new = list(v)
        for i in range(S5_GB):
            if i & delta == 0:
                a, b = v[i], v[i + delta]
                new[i] = jnp.where(keep, a, pltpu.roll(b, shift, axis=1))
                new[i + delta] = jnp.where(keep, pltpu.roll(a, LANES - shift, axis=1), b)
        v = new
    return v


def _toeplitz_rows(strip, s):
    lo, hi = strip[:, :LANES], strip[:, LANES:]
    lane = jax.lax.broadcasted_iota(jnp.int32, lo.shape, 1)
    shift = (s * S5_GROUP) % LANES
    if s == 0:
        return strip
    if s * S5_GROUP < LANES:
        lo_r, hi_r = pltpu.roll(lo, shift, axis=1), pltpu.roll(hi, shift, axis=1)
        return jnp.concatenate([jnp.where(lane >= shift, lo_r, 0.0),
                                jnp.where(lane >= shift, hi_r, lo_r)], axis=1)
    lo_r = lo if shift == 0 else pltpu.roll(lo, shift, axis=1)
    return jnp.concatenate([jnp.zeros_like(lo), jnp.where(lane >= shift, lo_r, 0.0)], axis=1)


def _s5_kernel(x_ref, strip_ref, min_ref, mout_ref, sc_ref, o_ref, mi_scr, *, chunks_per_seq):
    rows = x_ref.shape[1]
    n_idx = jax.lax.broadcasted_iota(jnp.int32, (rows, S5_RI), 0) % chunks_per_seq
    halves = S5_T // S5_GB
    for gi in range(S5_GB):
        strip = strip_ref[gi]
        for s in range(S5_T):
            mi_scr[gi, s * S5_GROUP:(s + 1) * S5_GROUP, :] = _toeplitz_rows(strip, s).astype(BF16)
    xin = [_piece_transpose([x_ref[0, :, (S5_GB * h + i) * LANES:(S5_GB * h + i + 1) * LANES].astype(F32)
                             for i in range(S5_GB)]) for h in range(halves)]
    ys = []
    for gi in range(S5_GB):
        xg = jnp.concatenate([xin[h][gi] for h in range(halves)], axis=1).astype(BF16)
        y = _dot(xg, mi_scr[gi])
        s = _dot(xg, min_ref[gi])
        sc = sc_ref[gi]
        for k in range(S5_SCAN_STEPS):
            d = 1 << k
            sh = jnp.where(n_idx >= d, pltpu.roll(s, d, axis=0), 0.0)
            s = s + sc[k:k + 1, :] * sh + sc[8 + k:9 + k, :] * pltpu.roll(sh, S5_STATE, axis=1)
        s_prev = jnp.where(n_idx >= 1, pltpu.roll(s, 1, axis=0), 0.0)
        y = y + _dot(s_prev.astype(BF16), mout_ref[gi])
        ys.append(jax.nn.gelu(y))
    for h in range(halves):
        out = _piece_transpose([ys[gi][:, h * LANES:(h + 1) * LANES] for gi in range(S5_GB)])
        for i in range(S5_GB):
            t = S5_GB * h + i
            o_ref[0, :, t * LANES:(t + 1) * LANES] = out[i].astype(o_ref.dtype)


def _s5_core(x_tiles, strip, m_in, m_out, sc, *, chunks_per_seq):
    tiles, rows, width = x_tiles.shape
    return pl.pallas_call(
        functools.partial(_s5_kernel, chunks_per_seq=chunks_per_seq),
        grid=(tiles,),
        in_specs=[
            pl.BlockSpec((1, rows, width), lambda i: (i, 0, 0)),
            pl.BlockSpec((S5_GB, S5_GROUP, S5_TC), lambda i: (i, 0, 0)),
            pl.BlockSpec((S5_GB, S5_TC, S5_RI), lambda i: (i, 0, 0)),
            pl.BlockSpec((S5_GB, S5_RI, S5_TC), lambda i: (i, 0, 0)),
            pl.BlockSpec((S5_GB, 16, S5_RI), lambda i: (i, 0, 0)),
        ],
        out_specs=pl.BlockSpec((1, rows, width), lambda i: (i, 0, 0)),
        out_shape=jax.ShapeDtypeStruct((tiles, rows, width), BF16),
        scratch_shapes=[pltpu.VMEM((S5_GB, S5_TC, S5_TC), BF16)],
        compiler_params=_params("parallel"),
        name="s5_core",
    )(x_tiles, strip, m_in, m_out, sc)


def _s5_tables(a_re, a_im, log_dt, b_re, b_im, c_re, c_im, d):
    hi = jax.lax.Precision.HIGHEST
    a_re, a_im = a_re.astype(F32), a_im.astype(F32)
    dt = jnp.exp(log_dt.astype(F32))[:, None]
    mag = jnp.exp(a_re * dt)
    lb_re, lb_im = mag * jnp.cos(a_im * dt), mag * jnp.sin(a_im * dt)
    den = a_re * a_re + a_im * a_im
    q_re = ((lb_re - 1.0) * a_re + lb_im * a_im) / den
    q_im = (lb_im * a_re - (lb_re - 1.0) * a_im) / den
    b_re = b_re.astype(F32).transpose(0, 2, 1)
    b_im = b_im.astype(F32).transpose(0, 2, 1)
    bb_re = q_re[:, None, :] * b_re - q_im[:, None, :] * b_im
    bb_im = q_re[:, None, :] * b_im + q_im[:, None, :] * b_re
    ks = jnp.arange(S5_T + 1, dtype=F32)[:, None, None]
    pmag = jnp.exp(ks * (a_re * dt))
    pk_re = pmag * jnp.cos(ks * (a_im * dt))
    pk_im = pmag * jnp.sin(ks * (a_im * dt))
    e_re = pk_re[:S5_T, :, None, :] * bb_re - pk_im[:S5_T, :, None, :] * bb_im
    e_im = pk_re[:S5_T, :, None, :] * bb_im + pk_im[:S5_T, :, None, :] * bb_re
    c_re, c_im = c_re.astype(F32), c_im.astype(F32)
    taps = (jnp.einsum('gcp,kgdp->gdkc', c_re, e_re, precision=hi)
            - jnp.einsum('gcp,kgdp->gdkc', c_im, e_im, precision=hi))
    skip = d.astype(F32).reshape(S5_GROUPS, 1, S5_GROUP) * jnp.eye(S5_GROUP, dtype=F32)
    taps = taps.at[:, :, 0, :].add(skip)
    strip = taps.reshape(S5_GROUPS, S5_GROUP, S5_TC)
    m_in = jnp.concatenate([e_re[::-1], e_im[::-1]], axis=3)
    m_in = m_in.transpose(1, 0, 2, 3).reshape(S5_GROUPS, S5_TC, S5_RI).astype(BF16)
    ct_re, ct_im = c_re.transpose(0, 2, 1), c_im.transpose(0, 2, 1)
    pt_re, pt_im = pk_re[1:].transpose(1, 2, 0), pk_im[1:].transpose(1, 2, 0)
    w_re = ct_re[:, :, None, :] * pt_re[:, :, :, None] - ct_im[:, :, None, :] * pt_im[:, :, :, None]
    w_im = ct_re[:, :, None, :] * pt_im[:, :, :, None] + ct_im[:, :, None, :] * pt_re[:, :, :, None]
    m_out = jnp.concatenate([w_re, -w_im], axis=1).reshape(S5_GROUPS, S5_RI, S5_TC).astype(BF16)
    f_re, f_im = pk_re[S5_T], pk_im[S5_T]
    rows_a, rows_b = [], []
    for _ in range(S5_SCAN_STEPS):
        rows_a.append(jnp.concatenate([f_re, f_re], axis=1))
        rows_b.append(jnp.concatenate([-f_im, f_im], axis=1))
        f_re, f_im = f_re * f_re - f_im * f_im, 2.0 * f_re * f_im
    zero = jnp.zeros_like(rows_a[0])
    sc = jnp.stack(rows_a + [zero] + rows_b + [zero], axis=1)
    return strip, m_in, m_out, sc


def kernel(x, p, norm_mix, norm_ffn, norm_ple, norm_final, gm_w_in, gm_ln_g, gm_ln_b, gm_w_s, gm_b_s, gm_w_out, s5_w_in, s5_a_re, s5_a_im, s5_log_dt, s5_b_re, s5_b_im, s5_c_re, s5_c_im, s5_d, s5_w_out, ffn_w1, ffn_w3, ffn_w2, ple_w_gate, ple_w_proj):
    bsz, seq, d = x.shape
    depth = p.shape[0]
    m = bsz * seq
    xs = x.reshape(m, d)
    ps = p.reshape(depth, m, PLE_DIM)
    for i in range(depth):
        j = i // 2
        if i % 2 == 0:
            z = _gm_in(xs, norm_mix[i], gm_w_in, j, tm=512)
            g = _sgu(z, gm_ln_g, gm_ln_b, gm_w_s, gm_b_s, j, chunks=2)
            xs, hn = _mm_res_norm(g, gm_w_out, j, xs, norm_ffn[i], tm=512)
        else:
            tables = _s5_tables(s5_a_re[j], s5_a_im[j], s5_log_dt[j], s5_b_re[j], s5_b_im[j],
                                s5_c_re[j], s5_c_im[j], s5_d[j])
            u = _s5_in(xs, norm_mix[i], s5_w_in, j, tm=512)
            y = _s5_core(u, *tables, chunks_per_seq=seq // S5_T)
            xs, hn = _s5_out(y, s5_w_out, j, xs, norm_ffn[i], tm=256)
        a = _ffn_up(hn, ffn_w1, ffn_w3, i, tm=1024, tf=512)
        xs = _ffn_down(a, ffn_w2, i, xs, tm=256)
        xs = _ple(xs, norm_ple[i], ps, ple_w_gate, ple_w_proj, i,
                  norm_final if i == depth - 1 else None, tm=512)
    return xs.reshape(bsz, seq, d)
```

```python
import functools

import jax
import jax.numpy as jnp
from jax.experimental import pallas as pl
from jax.experimental.pallas import tpu as pltpu

F32 = jnp.float32
BF16 = jnp.bfloat16

D_MODEL = 2048
PLE_DIM = 256
EPS = 1e-6
LANES = 128
BF16_ROWS = 16

GM_CHUNK = 128
GM_HEAD_DIM = 128
GM_HEADS = D_MODEL // GM_HEAD_DIM

S5_GROUP = 16
S5_GROUPS = D_MODEL // S5_GROUP
S5_STATE = 64
S5_T = 16
S5_TC = S5_T * S5_GROUP
S5_RI = 2 * S5_STATE
S5_GB = LANES // S5_GROUP
S5_TILES = D_MODEL // LANES
S5_SCAN_STEPS = 7

W_CHUNKS = 16
SUB_ROWS = 256
VMEM_LIMIT = 56 * 1024 * 1024


def _params(*sem):
    return pltpu.CompilerParams(dimension_semantics=sem, vmem_limit_bytes=VMEM_LIMIT)


def _rms(xf, gain):
    ms = jnp.mean(xf * xf, axis=-1, keepdims=True)
    return xf * jax.lax.rsqrt(ms + EPS) * gain


def _dot(a, b):
    return jnp.dot(a, b, preferred_element_type=F32)


def _sub_blocks(tm):
    sub = min(tm, SUB_ROWS)
    return [slice(r, r + sub) for r in range(0, tm, sub)]


def _cast_rows(k, steps):
    rows = BF16_ROWS
    while k % rows or k // rows > steps:
        rows += BF16_ROWS
    return rows


def _cast_specs(casts, steps, step_of):
    ins, outs, shapes = [], [], []
    for arr, layer in casts:
        _, k, n = arr.shape
        rows = _cast_rows(k, steps)
        last = k // rows - 1
        ins.append(pl.BlockSpec(
            (None, rows, n), lambda *g, layer=layer, last=last: (layer, jnp.minimum(step_of(*g), last), 0)))
        outs.append(pl.BlockSpec((rows, n), lambda *g, last=last: (jnp.minimum(step_of(*g), last), 0)))
        shapes.append(jax.ShapeDtypeStruct((k, n), BF16))
    return ins, outs, shapes


def _run_casts(ci_refs, co_refs):
    for ci, co in zip(ci_refs, co_refs):
        co[...] = ci[...].astype(BF16)


def _sgu_kernel(u_ref, v_ref, lg_ref, lb_ref, ws_ref, bs_ref, o_ref, *, chunks):
    row = jax.lax.broadcasted_iota(jnp.int32, (GM_CHUNK, GM_CHUNK), 0)
    col = jax.lax.broadcasted_iota(jnp.int32, (GM_CHUNK, GM_CHUNK), 1)
    causal = row >= col
    for c in range(chunks):
        rows = slice(c * GM_CHUNK, (c + 1) * GM_CHUNK)
        v = v_ref[rows, :].astype(F32)
        mu = jnp.mean(v, axis=-1, keepdims=True)
        vc = v - mu
        var = jnp.mean(vc * vc, axis=-1, keepdims=True)
        vn = (vc * jax.lax.rsqrt(var + EPS) * lg_ref[...] + lb_ref[...]).astype(BF16)
        for h in range(GM_HEADS):
            cols = slice(h * GM_HEAD_DIM, (h + 1) * GM_HEAD_DIM)
            w = jnp.where(causal, ws_ref[h], 0.0).astype(BF16)
            sv = _dot(w, vn[:, cols]) + bs_ref[h]
            o_ref[rows, cols] = (u_ref[rows, cols].astype(F32) * sv).astype(o_ref.dtype)


def _sgu(z, ln_g, ln_b, w_s, b_s, layer, *, chunks):
    m = z.shape[0]
    w = D_MODEL
    tm = chunks * GM_CHUNK
    bs = jnp.broadcast_to(b_s[layer][:, :, None], (GM_HEADS, GM_CHUNK, GM_HEAD_DIM)).astype(F32)
    return pl.pallas_call(
        functools.partial(_sgu_kernel, chunks=chunks),
        grid=(m // tm,),
        in_specs=[
            pl.BlockSpec((tm, w), lambda i: (i, 0)),
            pl.BlockSpec((tm, w), lambda i: (i, 1)),
            pl.BlockSpec((1, w), lambda i: (layer, 0)),
            pl.BlockSpec((1, w), lambda i: (layer, 0)),
            pl.BlockSpec((None, GM_HEADS, GM_CHUNK, GM_CHUNK), lambda i: (layer, 0, 0, 0)),
            pl.BlockSpec((GM_HEADS, GM_CHUNK, GM_HEAD_DIM), lambda i: (0, 0, 0)),
        ],
        out_specs=pl.BlockSpec((tm, w), lambda i: (i, 0)),
        out_shape=jax.ShapeDtypeStruct((m, w), BF16),
        compiler_params=_params("parallel"),
        name="sgu",
    )(z, z, ln_g, ln_b, w_s, bs)


def _resident_call(body, *, name, m, tm, row_ins, consts, outs, staged=(), weights=(), casts=(), scratch=()):
    ns, nw, nk, nr, nc, no = len(staged), len(weights), len(casts), len(row_ins), len(consts), len(outs)
    lead = W_CHUNKS if ns else 0
    steps = lead + m // tm

    def kern(*refs):
        pos = 0

        def take(n):
            nonlocal pos
            pos += n
            return refs[pos - n:pos]

        st_refs, w_refs, ci_refs, r_refs, c_refs = take(ns), take(nw), take(nk), take(nr), take(nc)
        o_refs, co_refs, sb_refs, s_refs = take(no), take(nk), take(ns), take(len(scratch))
        s = pl.program_id(0)
        _run_casts(ci_refs, co_refs)

        def run():
            body(list(sb_refs) + list(w_refs), r_refs, c_refs, o_refs, s_refs)

        if lead:
            @pl.when(s < lead)
            def _():
                for st_ref, sb_ref in zip(st_refs, sb_refs):
                    rows = st_ref.shape[0]
                    sb_ref[pl.ds(pl.multiple_of(s * rows, rows), rows), :] = st_ref[...].astype(BF16)

            pl.when(s >= lead)(run)
        else:
            run()

    def tile(s):
        return jnp.maximum(s - lead, 0)

    in_specs, args, sb_scratch = [], [], []
    for arr, layer in staged:
        _, k, n = arr.shape
        in_specs.append(pl.BlockSpec(
            (None, k // W_CHUNKS, n),
            lambda s, layer=layer: (layer, jnp.minimum(s, W_CHUNKS - 1), 0)))
        args.append(arr)
        sb_scratch.append(pltpu.VMEM((k, n), BF16))
    for arr in weights:
        in_specs.append(pl.BlockSpec(arr.shape, lambda s: (0, 0), pipeline_mode=pl.Buffered(1)))
        args.append(arr)
    cast_in, cast_out, cast_shape = _cast_specs(casts, steps, lambda s: s)
    in_specs += cast_in
    args += [arr for arr, _ in casts]
    for arr, block, index_fn in row_ins:
        in_specs.append(pl.BlockSpec(block, lambda s, f=index_fn: f(tile(s))))
        args.append(arr)
    for arr in consts:
        in_specs.append(pl.BlockSpec((1, arr.shape[1]), lambda s: (0, 0)))
        args.append(arr)
    out_specs = [pl.BlockSpec(block, lambda s, f=index_fn: f(tile(s))) for _, _, block, index_fn in outs]
    out_shape = [jax.ShapeDtypeStruct(shape, dtype) for shape, dtype, _, _ in outs]
    return pl.pallas_call(
        kern,
        grid=(steps,),
        in_specs=in_specs,
        out_specs=out_specs + cast_out,
        out_shape=out_shape + cast_shape,
        scratch_shapes=sb_scratch + list(scratch),
        compiler_params=_params("arbitrary"),
        name=name,
    )(*args)


def _rows(tm, n):
    return (tm, n), (lambda i: (i, 0))


def _gm_in(x, gain, w, layer, casts, *, tm):
    m, k = x.shape
    n = w.shape[2]

    def body(wb, r, c, o, scr):
        for rs in _sub_blocks(tm):
            h = _rms(r[0][rs, :], c[0][...]).astype(BF16)
            o[0][rs, :] = jax.nn.gelu(_dot(h, wb[0][...])).astype(BF16)

    blk_x, idx = _rows(tm, k)
    blk_o, _ = _rows(tm, n)
    return _resident_call(
        body, name="gm_in", m=m, tm=tm, staged=[(w, layer)], casts=casts,
        row_ins=[(x, blk_x, idx)], consts=[gain.reshape(1, k)],
        outs=[((m, n), BF16, blk_o, idx)])


def _mm_res_norm(a, wb16, res, gain, *, tm):
    m, n = res.shape

    def body(wb, r, c, o, scr):
        for rs in _sub_blocks(tm):
            xn = r[1][rs, :] + _dot(r[0][rs, :], wb[0][...])
            o[0][rs, :] = xn
            o[1][rs, :] = _rms(xn, c[0][...]).astype(BF16)

    blk_a, idx = _rows(tm, a.shape[1])
    blk_x, _ = _rows(tm, n)
    return _resident_call(
        body, name="mm_res_norm", m=m, tm=tm, weights=[wb16],
        row_ins=[(a, blk_a, idx), (res, blk_x, idx)], consts=[gain.reshape(1, n)],
        outs=[((m, n), F32, blk_x, idx), ((m, n), BF16, blk_x, idx)])


def _ffn_down(a, w2b16, res, casts, *, tm):
    m, n = res.shape

    def body(wb, r, c, o, scr):
        for rs in _sub_blocks(tm):
            o[0][rs, :] = r[1][rs, :] + _dot(r[0][rs, :], wb[0][...])

    blk_a, idx = _rows(tm, a.shape[1])
    blk_x, _ = _rows(tm, n)
    return _resident_call(
        body, name="ffn_down", m=m, tm=tm, weights=[w2b16], casts=casts,
        row_ins=[(a, blk_a, idx), (res, blk_x, idx)], consts=[],
        outs=[((m, n), F32, blk_x, idx)])


def _ple(x, gain, p, layer, wgb16, wpb16, final_gain, casts, *, tm):
    m, n = x.shape

    def body(wb, r, c, o, scr):
        for rs in _sub_blocks(tm):
            xf = r[0][rs, :]
            h = _rms(xf, c[0][...]).astype(BF16)
            gate = jax.nn.sigmoid(_dot(h, wb[0][...]))
            proj = _dot(r[1][rs, :].astype(BF16), wb[1][...])
            xn = xf + gate * proj
            if final_gain is not None:
                xn = _rms(xn, c[1][...])
            o[0][rs, :] = xn

    blk_x, idx = _rows(tm, n)
    blk_p, _ = _rows(tm, PLE_DIM)
    consts = [gain.reshape(1, n)] + ([] if final_gain is None else [final_gain.reshape(1, n)])
    return _resident_call(
        body, name="ple", m=m, tm=tm, weights=[wgb16, wpb16], casts=casts,
        row_ins=[(x, blk_x, idx), (p, (None,) + blk_p, lambda i: (layer, i, 0))], consts=consts,
        outs=[((m, n), F32, blk_x, idx)])


def _s5_in(x, gain, wb16, *, tm):
    m, n = x.shape
    tc = tm // S5_T

    def body(wb, r, c, o, scr):
        for rs in _sub_blocks(tm):
            h = _rms(r[0][rs, :], c[0][...]).astype(BF16)
            u = _dot(h, wb[0][...])
            for l in range(S5_TILES):
                scr[0][l, rs, :] = u[:, l * LANES:(l + 1) * LANES]
            sub = rs.stop - rs.start
            cs = slice(rs.start // S5_T, rs.stop // S5_T)
            for t in range(S5_T):
                for l in range(S5_TILES):
                    o[0][l, cs, t * LANES:(t + 1) * LANES] = (
                        scr[0][l, pl.ds(rs.start + t, sub // S5_T, stride=S5_T), :].astype(BF16))

    blk_x, idx = _rows(tm, n)
    return _resident_call(
        body, name="s5_in", m=m, tm=tm, weights=[wb16],
        row_ins=[(x, blk_x, idx)], consts=[gain.reshape(1, n)],
        outs=[((S5_TILES, m // S5_T, S5_T * LANES), BF16, (S5_TILES, tc, S5_T * LANES), lambda i: (0, i, 0))],
        scratch=[pltpu.VMEM((S5_TILES, tm, LANES), F32)])[0]


def _s5_out(y_tiles, wb16, res, gain, *, tm):
    m, n = res.shape
    tc = tm // S5_T

    def body(wb, r, c, o, scr):
        for rs in _sub_blocks(tm):
            sub = rs.stop - rs.start
            cs = slice(rs.start // S5_T, rs.stop // S5_T)
            for t in range(S5_T):
                for l in range(S5_TILES):
                    scr[0][l, pl.ds(rs.start + t, sub // S5_T, stride=S5_T), :] = (
                        r[0][l, cs, t * LANES:(t + 1) * LANES].astype(F32))
            y = jnp.concatenate([scr[0][l, rs, :] for l in range(S5_TILES)], axis=1).astype(BF16)
            vg = _dot(y, wb[0][...])
            xn = r[1][rs, :] + vg[:, :n] * jax.nn.sigmoid(vg[:, n:])
            o[0][rs, :] = xn
            o[1][rs, :] = _rms(xn, c[0][...]).astype(BF16)

    blk_x, idx = _rows(tm, n)
    return _resident_call(
        body, name="s5_out", m=m, tm=tm, weights=[wb16],
        row_ins=[(y_tiles, (S5_TILES, tc, S5_T * LANES), lambda i: (0, i, 0)), (res, blk_x, idx)],
        consts=[gain.reshape(1, n)],
        outs=[((m, n), F32, blk_x, idx), ((m, n), BF16, blk_x, idx)],
        scratch=[pltpu.VMEM((S5_TILES, tm, LANES), F32)])


def _ffn_up_kernel(*refs, tm, n_casts):
    h_ref, w1_ref, w3_ref = refs[:3]
    ci_refs = refs[3:3 + n_casts]
    o_ref = refs[3 + n_casts]
    co_refs = refs[4 + n_casts:4 + 2 * n_casts]
    w1_scr, w3_scr = refs[4 + 2 * n_casts:]
    _run_casts(ci_refs, co_refs)

    @pl.when(pl.program_id(1) == 0)
    def _():
        w1_scr[...] = w1_ref[...].astype(BF16)
        w3_scr[...] = w3_ref[...].astype(BF16)

    for rs in _sub_blocks(tm):
        h = h_ref[rs, :]
        o_ref[rs, :] = (jax.nn.silu(_dot(h, w1_scr[...])) * _dot(h, w3_scr[...])).astype(o_ref.dtype)


def _ffn_up(h, w1, w3, layer, casts, *, tm, tf):
    m, d = h.shape
    ff = w1.shape[2]
    ni = m // tm
    cast_in, cast_out, cast_shape = _cast_specs(casts, (ff // tf) * ni, lambda f, i: f * ni + i)
    return pl.pallas_call(
        functools.partial(_ffn_up_kernel, tm=tm, n_casts=len(casts)),
        grid=(ff // tf, ni),
        in_specs=[
            pl.BlockSpec((tm, d), lambda f, i: (i, 0)),
            pl.BlockSpec((None, d, tf), lambda f, i: (layer, 0, f)),
            pl.BlockSpec((None, d, tf), lambda f, i: (layer, 0, f)),
        ] + cast_in,
        out_specs=[pl.BlockSpec((tm, tf), lambda f, i: (i, f))] + cast_out,
        out_shape=[jax.ShapeDtypeStruct((m, ff), BF16)] + cast_shape,
        scratch_shapes=[pltpu.VMEM((d, tf), BF16), pltpu.VMEM((d, tf), BF16)],
        compiler_params=_params("arbitrary", "arbitrary"),
        name="ffn_up",
    )(h, w1, w3, *[arr for arr, _ in casts])


def _piece_transpose(v):
    piece = jax.lax.broadcasted_iota(jnp.int32, v[0].shape, 1) // S5_GROUP
    for delta in (4, 2, 1):
        keep = (piece & delta) == 0
        shift = delta * S5_GROUP
        new = list(v)
        for i in range(S5_GB):
            if i & delta == 0:
                a, b = v[i], v[i + delta]
                new[i] = jnp.where(keep, a, pltpu.roll(b, shift, axis=1))
                new[i + delta] = jnp.where(keep, pltpu.roll(a, LANES - shift, axis=1), b)
        v = new
    return v


def _toeplitz_rows(strip, s):
    lo, hi = strip[:, :LANES], strip[:, LANES:]
    lane = jax.lax.broadcasted_iota(jnp.int32, lo.shape, 1)
    shift = (s * S5_GROUP) % LANES
    if s == 0:
        return strip
    if s * S5_GROUP < LANES:
        lo_r, hi_r = pltpu.roll(lo, shift, axis=1), pltpu.roll(hi, shift, axis=1)
        return jnp.concatenate([jnp.where(lane >= shift, lo_r, 0.0),
                                jnp.where(lane >= shift, hi_r, lo_r)], axis=1)
    lo_r = lo if shift == 0 else pltpu.roll(lo, shift, axis=1)
    return jnp.concatenate([jnp.zeros_like(lo), jnp.where(lane >= shift, lo_r, 0.0)], axis=1)


def _s5_kernel(x_ref, strip_ref, min_ref, mout_ref, sc_ref, o_ref, mi_scr, *, chunks_per_seq):
    rows = x_ref.shape[1]
    n_idx = jax.lax.broadcasted_iota(jnp.int32, (rows, S5_RI), 0) % chunks_per_seq
    halves = S5_T // S5_GB
    for gi in range(S5_GB):
        strip = strip_ref[gi]
        for s in range(S5_T):
            mi_scr[gi, s * S5_GROUP:(s + 1) * S5_GROUP, :] = _toeplitz_rows(strip, s).astype(BF16)
    xin = [_piece_transpose([x_ref[0, :, (S5_GB * h + i) * LANES:(S5_GB * h + i + 1) * LANES].astype(F32)
                             for i in range(S5_GB)]) for h in range(halves)]
    ys = []
    for gi in range(S5_GB):
        xg = jnp.concatenate([xin[h][gi] for h in range(halves)], axis=1).astype(BF16)
        y = _dot(xg, mi_scr[gi])
        s = _dot(xg, min_ref[gi])
        sc = sc_ref[gi]
        for k in range(S5_SCAN_STEPS):
            d = 1 << k
            sh = jnp.where(n_idx >= d, pltpu.roll(s, d, axis=0), 0.0)
            s = s + sc[k:k + 1, :] * sh + sc[8 + k:9 + k, :] * pltpu.roll(sh, S5_STATE, axis=1)
        s_prev = jnp.where(n_idx >= 1, pltpu.roll(s, 1, axis=0), 0.0)
        y = y + _dot(s_prev.astype(BF16), mout_ref[gi])
        ys.append(jax.nn.gelu(y))
    for h in range(halves):
        out = _piece_transpose([ys[gi][:, h * LANES:(h + 1) * LANES] for gi in range(S5_GB)])
        for i in range(S5_GB):
            t = S5_GB * h + i
            o_ref[0, :, t * LANES:(t + 1) * LANES] = out[i].astype(o_ref.dtype)


def _s5_core(x_tiles, strip, m_in, m_out, sc, *, chunks_per_seq):
    tiles, rows, width = x_tiles.shape
    return pl.pallas_call(
        functools.partial(_s5_kernel, chunks_per_seq=chunks_per_seq),
        grid=(tiles,),
        in_specs=[
            pl.BlockSpec((1, rows, width), lambda i: (i, 0, 0)),
            pl.BlockSpec((S5_GB, S5_GROUP, S5_TC), lambda i: (i, 0, 0)),
            pl.BlockSpec((S5_GB, S5_TC, S5_RI), lambda i: (i, 0, 0)),
            pl.BlockSpec((S5_GB, S5_RI, S5_TC), lambda i: (i, 0, 0)),
            pl.BlockSpec((S5_GB, 16, S5_RI), lambda i: (i, 0, 0)),
        ],
        out_specs=pl.BlockSpec((1, rows, width), lambda i: (i, 0, 0)),
        out_shape=jax.ShapeDtypeStruct((tiles, rows, width), BF16),
        scratch_shapes=[pltpu.VMEM((S5_GB, S5_TC, S5_TC), BF16)],
        compiler_params=_params("parallel"),
        name="s5_core",
    )(x_tiles, strip, m_in, m_out, sc)


def _s5_tables(a_re, a_im, log_dt, b_re, b_im, c_re, c_im, d):
    hi = jax.lax.Precision.HIGHEST
    a_re, a_im = a_re.astype(F32), a_im.astype(F32)
    dt = jnp.exp(log_dt.astype(F32))[:, None]
    mag = jnp.exp(a_re * dt)
    lb_re, lb_im = mag * jnp.cos(a_im * dt), mag * jnp.sin(a_im * dt)
    den = a_re * a_re + a_im * a_im
    q_re = ((lb_re - 1.0) * a_re + lb_im * a_im) / den
    q_im = (lb_im * a_re - (lb_re - 1.0) * a_im) / den
    b_re = b_re.astype(F32).transpose(0, 2, 1)
    b_im = b_im.astype(F32).transpose(0, 2, 1)
    bb_re = q_re[:, None, :] * b_re - q_im[:, None, :] * b_im
    bb_im = q_re[:, None, :] * b_im + q_im[:, None, :] * b_re
    ks = jnp.arange(S5_T + 1, dtype=F32)[:, None, None]
    pmag = jnp.exp(ks * (a_re * dt))
    pk_re = pmag * jnp.cos(ks * (a_im * dt))
    pk_im = pmag * jnp.sin(ks * (a_im * dt))
    e_re = pk_re[:S5_T, :, None, :] * bb_re - pk_im[:S5_T, :, None, :] * bb_im
    e_im = pk_re[:S5_T, :, None, :] * bb_im + pk_im[:S5_T, :, None, :] * bb_re
    c_re, c_im = c_re.astype(F32), c_im.astype(F32)
    taps = (jnp.einsum('gcp,kgdp->gdkc', c_re, e_re, precision=hi)
            - jnp.einsum('gcp,kgdp->gdkc', c_im, e_im, precision=hi))
    skip = d.astype(F32).reshape(S5_GROUPS, 1, S5_GROUP) * jnp.eye(S5_GROUP, dtype=F32)
    taps = taps.at[:, :, 0, :].add(skip)
    strip = taps.reshape(S5_GROUPS, S5_GROUP, S5_TC)
    m_in = jnp.concatenate([e_re[::-1], e_im[::-1]], axis=3)
    m_in = m_in.transpose(1, 0, 2, 3).reshape(S5_GROUPS, S5_TC, S5_RI).astype(BF16)
    ct_re, ct_im = c_re.transpose(0, 2, 1), c_im.transpose(0, 2, 1)
    pt_re, pt_im = pk_re[1:].transpose(1, 2, 0), pk_im[1:].transpose(1, 2, 0)
    w_re = ct_re[:, :, None, :] * pt_re[:, :, :, None] - ct_im[:, :, None, :] * pt_im[:, :, :, None]
    w_im = ct_re[:, :, None, :] * pt_im[:, :, :, None] + ct_im[:, :, None, :] * pt_re[:, :, :, None]
    m_out = jnp.concatenate([w_re, -w_im], axis=1).reshape(S5_GROUPS, S5_RI, S5_TC).astype(BF16)
    f_re, f_im = pk_re[S5_T], pk_im[S5_T]
    rows_a, rows_b = [], []
    for _ in range(S5_SCAN_STEPS):
        rows_a.append(jnp.concatenate([f_re, f_re], axis=1))
        rows_b.append(jnp.concatenate([-f_im, f_im], axis=1))
        f_re, f_im = f_re * f_re - f_im * f_im, 2.0 * f_re * f_im
    zero = jnp.zeros_like(rows_a[0])
    sc = jnp.stack(rows_a + [zero] + rows_b + [zero], axis=1)
    return strip, m_in, m_out, sc


def kernel(x, p, norm_mix, norm_ffn, norm_ple, norm_final, gm_w_in, gm_ln_g, gm_ln_b, gm_w_s, gm_b_s, gm_w_out, s5_w_in, s5_a_re, s5_a_im, s5_log_dt, s5_b_re, s5_b_im, s5_c_re, s5_c_im, s5_d, s5_w_out, ffn_w1, ffn_w3, ffn_w2, ple_w_gate, ple_w_proj):
    bsz, seq, d = x.shape
    depth = p.shape[0]
    m = bsz * seq
    xs = x.reshape(m, d)
    ps = p.reshape(depth, m, PLE_DIM)
    s5_in_b16 = s5_out_b16 = None
    for i in range(depth):
        j = i // 2
        if i % 2 == 0:
            z, gm_out_b16 = _gm_in(xs, norm_mix[i], gm_w_in, j, [(gm_w_out, j)], tm=512)
            g = _sgu(z, gm_ln_g, gm_ln_b, gm_w_s, gm_b_s, j, chunks=2)
            xs, hn = _mm_res_norm(g, gm_out_b16, xs, norm_ffn[i], tm=512)
        else:
            tables = _s5_tables(s5_a_re[j], s5_a_im[j], s5_log_dt[j], s5_b_re[j], s5_b_im[j],
                                s5_c_re[j], s5_c_im[j], s5_d[j])
            u = _s5_in(xs, norm_mix[i], s5_in_b16, tm=512)
            y = _s5_core(u, *tables, chunks_per_seq=seq // S5_T)
            xs, hn = _s5_out(y, s5_out_b16, xs, norm_ffn[i], tm=256)
        a, w2_b16, wg_b16, wp_b16 = _ffn_up(
            hn, ffn_w1, ffn_w3, i, [(ffn_w2, i), (ple_w_gate, i), (ple_w_proj, i)], tm=2048, tf=512)
        next_s5 = i + 1 < depth and (i + 1) % 2 == 1
        xs, *cast = _ffn_down(a, w2_b16, xs, [(s5_w_in, (i + 1) // 2)] if next_s5 else [], tm=256)
        if next_s5:
            s5_in_b16 = cast[0]
        xs, *cast = _ple(xs, norm_ple[i], ps, i, wg_b16, wp_b16,
                         norm_final if i == depth - 1 else None,
                         [(s5_w_out, (i + 1) // 2)] if next_s5 else [], tm=512)
        if next_s5:
            s5_out_b16 = cast[0]
    return xs.reshape(bsz, seq, d)
```

```python
import functools

import jax
import jax.numpy as jnp
from jax.experimental import pallas as pl
from jax.experimental.pallas import tpu as pltpu

F32 = jnp.float32
BF16 = jnp.bfloat16

D_MODEL = 2048
PLE_DIM = 256
EPS = 1e-6
LANES = 128
BF16_ROWS = 16

GM_CHUNK = 128
GM_HEAD_DIM = 128
GM_HEADS = D_MODEL // GM_HEAD_DIM

S5_GROUP = 16
S5_GROUPS = D_MODEL // S5_GROUP
S5_STATE = 64
S5_T = 16
S5_TC = S5_T * S5_GROUP
S5_RI = 2 * S5_STATE
S5_GB = LANES // S5_GROUP
S5_TILES = D_MODEL // LANES
S5_SCAN_STEPS = 7

W_CHUNKS = 16
SUB_ROWS = 256
VMEM_LIMIT = 56 * 1024 * 1024


def _params(*sem):
    return pltpu.CompilerParams(dimension_semantics=sem, vmem_limit_bytes=VMEM_LIMIT)


def _rms(xf, gain):
    ms = jnp.mean(xf * xf, axis=-1, keepdims=True)
    return xf * jax.lax.rsqrt(ms + EPS) * gain


def _dot(a, b):
    return jnp.dot(a, b, preferred_element_type=F32)


def _sub_blocks(tm):
    sub = min(tm, SUB_ROWS)
    return [slice(r, r + sub) for r in range(0, tm, sub)]


def _cast_rows(k, steps):
    rows = BF16_ROWS
    while k % rows or k // rows > steps:
        rows += BF16_ROWS
    return rows


def _cast_specs(casts, steps, step_of):
    ins, outs, shapes = [], [], []
    for arr, layer in casts:
        _, k, n = arr.shape
        rows = _cast_rows(k, steps)
        last = k // rows - 1
        ins.append(pl.BlockSpec(
            (None, rows, n), lambda *g, layer=layer, last=last: (layer, jnp.minimum(step_of(*g), last), 0)))
        outs.append(pl.BlockSpec((rows, n), lambda *g, last=last: (jnp.minimum(step_of(*g), last), 0)))
        shapes.append(jax.ShapeDtypeStruct((k, n), BF16))
    return ins, outs, shapes


def _run_casts(ci_refs, co_refs):
    for ci, co in zip(ci_refs, co_refs):
        co[...] = ci[...].astype(BF16)


def _cast_kernel(*refs):
    half = len(refs) // 2
    _run_casts(refs[:half], refs[half:])


def _cast_call(casts):
    cast_in, cast_out, cast_shape = _cast_specs(casts, W_CHUNKS, lambda s: s)
    return pl.pallas_call(
        _cast_kernel,
        grid=(W_CHUNKS,),
        in_specs=cast_in,
        out_specs=cast_out,
        out_shape=cast_shape,
        compiler_params=_params("arbitrary"),
        name="cast_weights",
    )(*[arr for arr, _ in casts])


def _sgu_kernel(u_ref, v_ref, lg_ref, lb_ref, ws_ref, bs_ref, o_ref, *, chunks):
    row = jax.lax.broadcasted_iota(jnp.int32, (GM_CHUNK, GM_CHUNK), 0)
    col = jax.lax.broadcasted_iota(jnp.int32, (GM_CHUNK, GM_CHUNK), 1)
    causal = row >= col
    for c in range(chunks):
        rows = slice(c * GM_CHUNK, (c + 1) * GM_CHUNK)
        v = v_ref[rows, :].astype(F32)
        mu = jnp.mean(v, axis=-1, keepdims=True)
        vc = v - mu
        var = jnp.mean(vc * vc, axis=-1, keepdims=True)
        vn = (vc * jax.lax.rsqrt(var + EPS) * lg_ref[...] + lb_ref[...]).astype(BF16)
        for h in range(GM_HEADS):
            cols = slice(h * GM_HEAD_DIM, (h + 1) * GM_HEAD_DIM)
            w = jnp.where(causal, ws_ref[h], 0.0).astype(BF16)
            sv = _dot(w, vn[:, cols]) + bs_ref[h]
            o_ref[rows, cols] = (u_ref[rows, cols].astype(F32) * sv).astype(o_ref.dtype)


def _sgu(z, ln_g, ln_b, w_s, b_s, layer, *, chunks):
    m = z.shape[0]
    w = D_MODEL
    tm = chunks * GM_CHUNK
    bs = jnp.broadcast_to(b_s[layer][:, :, None], (GM_HEADS, GM_CHUNK, GM_HEAD_DIM)).astype(F32)
    return pl.pallas_call(
        functools.partial(_sgu_kernel, chunks=chunks),
        grid=(m // tm,),
        in_specs=[
            pl.BlockSpec((tm, w), lambda i: (i, 0)),
            pl.BlockSpec((tm, w), lambda i: (i, 1)),
            pl.BlockSpec((1, w), lambda i: (layer, 0)),
            pl.BlockSpec((1, w), lambda i: (layer, 0)),
            pl.BlockSpec((None, GM_HEADS, GM_CHUNK, GM_CHUNK), lambda i: (layer, 0, 0, 0)),
            pl.BlockSpec((GM_HEADS, GM_CHUNK, GM_HEAD_DIM), lambda i: (0, 0, 0)),
        ],
        out_specs=pl.BlockSpec((tm, w), lambda i: (i, 0)),
        out_shape=jax.ShapeDtypeStruct((m, w), BF16),
        compiler_params=_params("parallel"),
        name="sgu",
    )(z, z, ln_g, ln_b, w_s, bs)


def _resident_call(body, *, name, m, tm, weights, row_ins, consts, outs, casts=(), scratch=()):
    nw, nk, nr, nc, no = len(weights), len(casts), len(row_ins), len(consts), len(outs)
    steps = m // tm

    def kern(*refs):
        pos = 0

        def take(n):
            nonlocal pos
            pos += n
            return refs[pos - n:pos]

        w_refs, ci_refs, r_refs, c_refs = take(nw), take(nk), take(nr), take(nc)
        o_refs, co_refs, s_refs = take(no), take(nk), take(len(scratch))
        _run_casts(ci_refs, co_refs)
        body(w_refs, r_refs, c_refs, o_refs, s_refs)

    in_specs, args = [], []
    for arr in weights:
        in_specs.append(pl.BlockSpec(arr.shape, lambda s: (0, 0), pipeline_mode=pl.Buffered(1)))
        args.append(arr)
    cast_in, cast_out, cast_shape = _cast_specs(casts, steps, lambda s: s)
    in_specs += cast_in
    args += [arr for arr, _ in casts]
    for arr, block, index_fn in row_ins:
        in_specs.append(pl.BlockSpec(block, index_fn))
        args.append(arr)
    for arr in consts:
        in_specs.append(pl.BlockSpec((1, arr.shape[1]), lambda s: (0, 0)))
        args.append(arr)
    out_specs = [pl.BlockSpec(block, index_fn) for _, _, block, index_fn in outs]
    out_shape = [jax.ShapeDtypeStruct(shape, dtype) for shape, dtype, _, _ in outs]
    return pl.pallas_call(
        kern,
        grid=(steps,),
        in_specs=in_specs,
        out_specs=out_specs + cast_out,
        out_shape=out_shape + cast_shape,
        scratch_shapes=list(scratch),
        compiler_params=_params("arbitrary"),
        name=name,
    )(*args)


def _rows(tm, n):
    return (tm, n), (lambda i: (i, 0))


def _gm_in(x, gain, wb16, *, tm):
    m, k = x.shape
    n = wb16.shape[1]

    def body(wb, r, c, o, scr):
        for rs in _sub_blocks(tm):
            h = _rms(r[0][rs, :], c[0][...]).astype(BF16)
            o[0][rs, :] = jax.nn.gelu(_dot(h, wb[0][...])).astype(BF16)

    blk_x, idx = _rows(tm, k)
    blk_o, _ = _rows(tm, n)
    return _resident_call(
        body, name="gm_in", m=m, tm=tm, weights=[wb16],
        row_ins=[(x, blk_x, idx)], consts=[gain.reshape(1, k)],
        outs=[((m, n), BF16, blk_o, idx)])[0]


def _mm_res_norm(a, wb16, res, gain, *, tm):
    m, n = res.shape

    def body(wb, r, c, o, scr):
        for rs in _sub_blocks(tm):
            xn = r[1][rs, :] + _dot(r[0][rs, :], wb[0][...])
            o[0][rs, :] = xn
            o[1][rs, :] = _rms(xn, c[0][...]).astype(BF16)

    blk_a, idx = _rows(tm, a.shape[1])
    blk_x, _ = _rows(tm, n)
    return _resident_call(
        body, name="mm_res_norm", m=m, tm=tm, weights=[wb16],
        row_ins=[(a, blk_a, idx), (res, blk_x, idx)], consts=[gain.reshape(1, n)],
        outs=[((m, n), F32, blk_x, idx), ((m, n), BF16, blk_x, idx)])


def _ffn_down(a, w2b16, res, casts, *, tm):
    m, n = res.shape

    def body(wb, r, c, o, scr):
        for rs in _sub_blocks(tm):
            o[0][rs, :] = r[1][rs, :] + _dot(r[0][rs, :], wb[0][...])

    blk_a, idx = _rows(tm, a.shape[1])
    blk_x, _ = _rows(tm, n)
    return _resident_call(
        body, name="ffn_down", m=m, tm=tm, weights=[w2b16], casts=casts,
        row_ins=[(a, blk_a, idx), (res, blk_x, idx)], consts=[],
        outs=[((m, n), F32, blk_x, idx)])


def _ple(x, gain, p, layer, wgb16, wpb16, final_gain, casts, *, tm):
    m, n = x.shape

    def body(wb, r, c, o, scr):
        for rs in _sub_blocks(tm):
            xf = r[0][rs, :]
            h = _rms(xf, c[0][...]).astype(BF16)
            gate = jax.nn.sigmoid(_dot(h, wb[0][...]))
            proj = _dot(r[1][rs, :].astype(BF16), wb[1][...])
            xn = xf + gate * proj
            if final_gain is not None:
                xn = _rms(xn, c[1][...])
            o[0][rs, :] = xn

    blk_x, idx = _rows(tm, n)
    blk_p, _ = _rows(tm, PLE_DIM)
    consts = [gain.reshape(1, n)] + ([] if final_gain is None else [final_gain.reshape(1, n)])
    return _resident_call(
        body, name="ple", m=m, tm=tm, weights=[wgb16, wpb16], casts=casts,
        row_ins=[(x, blk_x, idx), (p, (None,) + blk_p, lambda i: (layer, i, 0))], consts=consts,
        outs=[((m, n), F32, blk_x, idx)])


def _s5_in(x, gain, wb16, *, tm):
    m, n = x.shape
    tc = tm // S5_T

    def body(wb, r, c, o, scr):
        for rs in _sub_blocks(tm):
            h = _rms(r[0][rs, :], c[0][...]).astype(BF16)
            u = _dot(h, wb[0][...])
            for l in range(S5_TILES):
                scr[0][l, rs, :] = u[:, l * LANES:(l + 1) * LANES]
            sub = rs.stop - rs.start
            cs = slice(rs.start // S5_T, rs.stop // S5_T)
            for t in range(S5_T):
                for l in range(S5_TILES):
                    o[0][l, cs, t * LANES:(t + 1) * LANES] = (
                        scr[0][l, pl.ds(rs.start + t, sub // S5_T, stride=S5_T), :].astype(BF16))

    blk_x, idx = _rows(tm, n)
    return _resident_call(
        body, name="s5_in", m=m, tm=tm, weights=[wb16],
        row_ins=[(x, blk_x, idx)], consts=[gain.reshape(1, n)],
        outs=[((S5_TILES, m // S5_T, S5_T * LANES), BF16, (S5_TILES, tc, S5_T * LANES), lambda i: (0, i, 0))],
        scratch=[pltpu.VMEM((S5_TILES, tm, LANES), F32)])[0]


def _s5_out(y_tiles, wb16, res, gain, *, tm):
    m, n = res.shape
    tc = tm // S5_T

    def body(wb, r, c, o, scr):
        for rs in _sub_blocks(tm):
            sub = rs.stop - rs.start
            cs = slice(rs.start // S5_T, rs.stop // S5_T)
            for t in range(S5_T):
                for l in range(S5_TILES):
                    scr[0][l, pl.ds(rs.start + t, sub // S5_T, stride=S5_T), :] = (
                        r[0][l, cs, t * LANES:(t + 1) * LANES].astype(F32))
            y = jnp.concatenate([scr[0][l, rs, :] for l in range(S5_TILES)], axis=1).astype(BF16)
            vg = _dot(y, wb[0][...])
            xn = r[1][rs, :] + vg[:, :n] * jax.nn.sigmoid(vg[:, n:])
            o[0][rs, :] = xn
            o[1][rs, :] = _rms(xn, c[0][...]).astype(BF16)

    blk_x, idx = _rows(tm, n)
    return _resident_call(
        body, name="s5_out", m=m, tm=tm, weights=[wb16],
        row_ins=[(y_tiles, (S5_TILES, tc, S5_T * LANES), lambda i: (0, i, 0)), (res, blk_x, idx)],
        consts=[gain.reshape(1, n)],
        outs=[((m, n), F32, blk_x, idx), ((m, n), BF16, blk_x, idx)],
        scratch=[pltpu.VMEM((S5_TILES, tm, LANES), F32)])


def _ffn_up_kernel(*refs, tm, n_casts):
    h_ref, w1_ref, w3_ref = refs[:3]
    ci_refs = refs[3:3 + n_casts]
    o_ref = refs[3 + n_casts]
    co_refs = refs[4 + n_casts:4 + 2 * n_casts]
    w1_scr, w3_scr = refs[4 + 2 * n_casts:]
    _run_casts(ci_refs, co_refs)

    @pl.when(pl.program_id(1) == 0)
    def _():
        w1_scr[...] = w1_ref[...].astype(BF16)
        w3_scr[...] = w3_ref[...].astype(BF16)

    for rs in _sub_blocks(tm):
        h = h_ref[rs, :]
        o_ref[rs, :] = (jax.nn.silu(_dot(h, w1_scr[...])) * _dot(h, w3_scr[...])).astype(o_ref.dtype)


def _ffn_up(h, w1, w3, layer, casts, *, tm, tf):
    m, d = h.shape
    ff = w1.shape[2]
    ni = m // tm
    cast_in, cast_out, cast_shape = _cast_specs(casts, (ff // tf) * ni, lambda f, i: f * ni + i)
    return pl.pallas_call(
        functools.partial(_ffn_up_kernel, tm=tm, n_casts=len(casts)),
        grid=(ff // tf, ni),
        in_specs=[
            pl.BlockSpec((tm, d), lambda f, i: (i, 0)),
            pl.BlockSpec((None, d, tf), lambda f, i: (layer, 0, f)),
            pl.BlockSpec((None, d, tf), lambda f, i: (layer, 0, f)),
        ] + cast_in,
        out_specs=[pl.BlockSpec((tm, tf), lambda f, i: (i, f))] + cast_out,
        out_shape=[jax.ShapeDtypeStruct((m, ff), BF16)] + cast_shape,
        scratch_shapes=[pltpu.VMEM((d, tf), BF16), pltpu.VMEM((d, tf), BF16)],
        compiler_params=_params("arbitrary", "arbitrary"),
        name="ffn_up",
    )(h, w1, w3, *[arr for arr, _ in casts])


def _piece_transpose(v):
    piece = jax.lax.broadcasted_iota(jnp.int32, v[0].shape, 1) // S5_GROUP
    for delta in (4, 2, 1):
        keep = (piece & delta) == 0
        shift = delta * S5_GROUP
        new = list(v)
        for i in range(S5_GB):
            if i & delta == 0:
                a, b = v[i], v[i + delta]
                new[i] = jnp.where(keep, a, pltpu.roll(b, shift, axis=1))
                new[i + delta] = jnp.where(keep, pltpu.roll(a, LANES - shift, axis=1), b)
        v = new
    return v


def _toeplitz_rows(strip, s):
    lo, hi = strip[:, :LANES], strip[:, LANES:]
    lane = jax.lax.broadcasted_iota(jnp.int32, lo.shape, 1)
    shift = (s * S5_GROUP) % LANES
    if s == 0:
        return strip
    if s * S5_GROUP < LANES:
        lo_r, hi_r = pltpu.roll(lo, shift, axis=1), pltpu.roll(hi, shift, axis=1)
        return jnp.concatenate([jnp.where(lane >= shift, lo_r, 0.0),
                                jnp.where(lane >= shift, hi_r, lo_r)], axis=1)
    lo_r = lo if shift == 0 else pltpu.roll(lo, shift, axis=1)
    return jnp.concatenate([jnp.zeros_like(lo), jnp.where(lane >= shift, lo_r, 0.0)], axis=1)


def _s5_kernel(x_ref, bb_ref, tap_ref, skip_ref, min_ref, mout_ref, sc_ref, o_ref, mi_scr, *, chunks_per_seq):
    rows = x_ref.shape[1]
    n_idx = jax.lax.broadcasted_iota(jnp.int32, (rows, S5_RI), 0) % chunks_per_seq
    halves = S5_T // S5_GB
    for gi in range(S5_GB):
        strip = jnp.dot(bb_ref[gi], tap_ref[gi], preferred_element_type=F32,
                        precision=jax.lax.Precision.HIGHEST) + skip_ref[gi]
        for s in range(S5_T):
            mi_scr[gi, s * S5_GROUP:(s + 1) * S5_GROUP, :] = _toeplitz_rows(strip, s).astype(BF16)
    xin = [_piece_transpose([x_ref[0, :, (S5_GB * h + i) * LANES:(S5_GB * h + i + 1) * LANES].astype(F32)
                             for i in range(S5_GB)]) for h in range(halves)]
    ys = []
    for gi in range(S5_GB):
        xg = jnp.concatenate([xin[h][gi] for h in range(halves)], axis=1).astype(BF16)
        y = _dot(xg, mi_scr[gi])
        s = _dot(xg, min_ref[gi])
        sc = sc_ref[gi]
        for k in range(S5_SCAN_STEPS):
            d = 1 << k
            sh = jnp.where(n_idx >= d, pltpu.roll(s, d, axis=0), 0.0)
            s = s + sc[k:k + 1, :] * sh + sc[8 + k:9 + k, :] * pltpu.roll(sh, S5_STATE, axis=1)
        s_prev = jnp.where(n_idx >= 1, pltpu.roll(s, 1, axis=0), 0.0)
        y = y + _dot(s_prev.astype(BF16), mout_ref[gi])
        ys.append(jax.nn.gelu(y))
    for h in range(halves):
        out = _piece_transpose([ys[gi][:, h * LANES:(h + 1) * LANES] for gi in range(S5_GB)])
        for i in range(S5_GB):
            t = S5_GB * h + i
            o_ref[0, :, t * LANES:(t + 1) * LANES] = out[i].astype(o_ref.dtype)


def _s5_core(x_tiles, bb, tap, skip, m_in, m_out, sc, *, chunks_per_seq):
    tiles, rows, width = x_tiles.shape
    return pl.pallas_call(
        functools.partial(_s5_kernel, chunks_per_seq=chunks_per_seq),
        grid=(tiles,),
        in_specs=[
            pl.BlockSpec((1, rows, width), lambda i: (i, 0, 0)),
            pl.BlockSpec((S5_GB, S5_GROUP, S5_RI), lambda i: (i, 0, 0)),
            pl.BlockSpec((S5_GB, S5_RI, S5_TC), lambda i: (i, 0, 0)),
            pl.BlockSpec((S5_GB, S5_GROUP, S5_TC), lambda i: (i, 0, 0)),
            pl.BlockSpec((S5_GB, S5_TC, S5_RI), lambda i: (i, 0, 0)),
            pl.BlockSpec((S5_GB, S5_RI, S5_TC), lambda i: (i, 0, 0)),
            pl.BlockSpec((S5_GB, 16, S5_RI), lambda i: (i, 0, 0)),
        ],
        out_specs=pl.BlockSpec((1, rows, width), lambda i: (i, 0, 0)),
        out_shape=jax.ShapeDtypeStruct((tiles, rows, width), BF16),
        scratch_shapes=[pltpu.VMEM((S5_GB, S5_TC, S5_TC), BF16)],
        compiler_params=_params("parallel"),
        name="s5_core",
    )(x_tiles, bb, tap, skip, m_in, m_out, sc)


def _s5_tables(a_re, a_im, log_dt, b_re, b_im, c_re, c_im, d):
    a_re, a_im = a_re.astype(F32), a_im.astype(F32)
    dt = jnp.exp(log_dt.astype(F32))[:, None]
    mag = jnp.exp(a_re * dt)
    lb_re, lb_im = mag * jnp.cos(a_im * dt), mag * jnp.sin(a_im * dt)
    den = a_re * a_re + a_im * a_im
    q_re = ((lb_re - 1.0) * a_re + lb_im * a_im) / den
    q_im = (lb_im * a_re - (lb_re - 1.0) * a_im) / den
    b_re = b_re.astype(F32).transpose(0, 2, 1)
    b_im = b_im.astype(F32).transpose(0, 2, 1)
    bb_re = q_re[:, None, :] * b_re - q_im[:, None, :] * b_im
    bb_im = q_re[:, None, :] * b_im + q_im[:, None, :] * b_re
    ks = jnp.arange(S5_T + 1, dtype=F32)[:, None, None]
    pmag = jnp.exp(ks * (a_re * dt))
    pk_re = pmag * jnp.cos(ks * (a_im * dt))
    pk_im = pmag * jnp.sin(ks * (a_im * dt))
    bb = jnp.concatenate([bb_re, bb_im], axis=2)
    pr_re, pr_im = pk_re[S5_T - 1::-1], pk_im[S5_T - 1::-1]
    e_re = pr_re[:, :, None, :] * bb_re - pr_im[:, :, None, :] * bb_im
    e_im = pr_re[:, :, None, :] * bb_im + pr_im[:, :, None, :] * bb_re
    m_in = jnp.concatenate([e_re, e_im], axis=3)
    m_in = m_in.transpose(1, 0, 2, 3).reshape(S5_GROUPS, S5_TC, S5_RI).astype(BF16)
    c_re, c_im = c_re.astype(F32), c_im.astype(F32)
    ct_re, ct_im = c_re.transpose(0, 2, 1), c_im.transpose(0, 2, 1)
    pt_re, pt_im = pk_re.transpose(1, 2, 0), pk_im.transpose(1, 2, 0)
    w_re = ct_re[:, :, None, :] * pt_re[:, :, :, None] - ct_im[:, :, None, :] * pt_im[:, :, :, None]
    w_im = ct_re[:, :, None, :] * pt_im[:, :, :, None] + ct_im[:, :, None, :] * pt_re[:, :, :, None]
    w = jnp.concatenate([w_re, -w_im], axis=1)
    tap = w[:, :, :S5_T].reshape(S5_GROUPS, S5_RI, S5_TC)
    m_out = w[:, :, 1:].reshape(S5_GROUPS, S5_RI, S5_TC).astype(BF16)
    skip = d.astype(F32).reshape(S5_GROUPS, 1, S5_GROUP) * jnp.eye(S5_GROUP, dtype=F32)
    skip = jnp.pad(skip, ((0, 0), (0, 0), (0, S5_TC - S5_GROUP)))
    f_re, f_im = pk_re[S5_T], pk_im[S5_T]
    rows_a, rows_b = [], []
    for _ in range(S5_SCAN_STEPS):
        rows_a.append(jnp.concatenate([f_re, f_re], axis=1))
        rows_b.append(jnp.concatenate([-f_im, f_im], axis=1))
        f_re, f_im = f_re * f_re - f_im * f_im, 2.0 * f_re * f_im
    zero = jnp.zeros_like(rows_a[0])
    sc = jnp.stack(rows_a + [zero] + rows_b + [zero], axis=1)
    return bb, tap, skip, m_in, m_out, sc


def kernel(x, p, norm_mix, norm_ffn, norm_ple, norm_final, gm_w_in, gm_ln_g, gm_ln_b, gm_w_s, gm_b_s, gm_w_out, s5_w_in, s5_a_re, s5_a_im, s5_log_dt, s5_b_re, s5_b_im, s5_c_re, s5_c_im, s5_d, s5_w_out, ffn_w1, ffn_w3, ffn_w2, ple_w_gate, ple_w_proj):
    bsz, seq, d = x.shape
    depth = p.shape[0]
    m = bsz * seq
    xs = x.reshape(m, d)
    ps = p.reshape(depth, m, PLE_DIM)
    s5_in_b16 = s5_out_b16 = None
    for i in range(depth):
        j = i // 2
        if i % 2 == 0:
            gm_in_b16, gm_out_b16 = _cast_call([(gm_w_in, j), (gm_w_out, j)])
            z = _gm_in(xs, norm_mix[i], gm_in_b16, tm=512)
            g = _sgu(z, gm_ln_g, gm_ln_b, gm_w_s, gm_b_s, j, chunks=2)
            xs, hn = _mm_res_norm(g, gm_out_b16, xs, norm_ffn[i], tm=512)
        else:
            tables = _s5_tables(s5_a_re[j], s5_a_im[j], s5_log_dt[j], s5_b_re[j], s5_b_im[j],
                                s5_c_re[j], s5_c_im[j], s5_d[j])
            u = _s5_in(xs, norm_mix[i], s5_in_b16, tm=512)
            y = _s5_core(u, *tables, chunks_per_seq=seq // S5_T)
            xs, hn = _s5_out(y, s5_out_b16, xs, norm_ffn[i], tm=256)
        a, w2_b16, wg_b16, wp_b16 = _ffn_up(
            hn, ffn_w1, ffn_w3, i, [(ffn_w2, i), (ple_w_gate, i), (ple_w_proj, i)], tm=2048, tf=512)
        next_s5 = i + 1 < depth and (i + 1) % 2 == 1
        xs, *cast = _ffn_down(a, w2_b16, xs, [(s5_w_in, (i + 1) // 2)] if next_s5 else [], tm=256)
        if next_s5:
            s5_in_b16 = cast[0]
        xs, *cast = _ple(xs, norm_ple[i], ps, i, wg_b16, wp_b16,
                         norm_final if i == depth - 1 else None,
                         [(s5_w_out, (i + 1) // 2)] if next_s5 else [], tm=512)
        if next_s5:
            s5_out_b16 = cast[0]
    return xs.reshape(bsz, seq, d)
```

```python
import functools

import jax
import jax.numpy as jnp
from jax.experimental import pallas as pl
from jax.experimental.pallas import tpu as pltpu

F32 = jnp.float32
BF16 = jnp.bfloat16

D_MODEL = 2048
PLE_DIM = 256
EPS = 1e-6
LANES = 128
BF16_ROWS = 16

GM_CHUNK = 128
GM_HEAD_DIM = 128
GM_HEADS = D_MODEL // GM_HEAD_DIM

S5_GROUP = 16
S5_GROUPS = D_MODEL // S5_GROUP
S5_STATE = 64
S5_T = 16
S5_TC = S5_T * S5_GROUP
S5_RI = 2 * S5_STATE
S5_GB = LANES // S5_GROUP
S5_TILES = D_MODEL // LANES
S5_SCAN_STEPS = 7

W_CHUNKS = 16
SUB_ROWS = 256
VMEM_LIMIT = 56 * 1024 * 1024


def _params(*sem):
    return pltpu.CompilerParams(dimension_semantics=sem, vmem_limit_bytes=VMEM_LIMIT)


def _rms(xf, gain):
    ms = jnp.mean(xf * xf, axis=-1, keepdims=True)
    return xf * jax.lax.rsqrt(ms + EPS) * gain


def _dot(a, b):
    return jnp.dot(a, b, preferred_element_type=F32)


def _sub_blocks(tm):
    sub = min(tm, SUB_ROWS)
    return [slice(r, r + sub) for r in range(0, tm, sub)]


def _cast_rows(k, steps):
    rows = BF16_ROWS
    while k % rows or k // rows > steps:
        rows += BF16_ROWS
    return rows


def _cast_specs(casts, steps, step_of):
    ins, outs, shapes = [], [], []
    for arr, layer in casts:
        _, k, n = arr.shape
        rows = _cast_rows(k, steps)
        last = k // rows - 1
        ins.append(pl.BlockSpec(
            (None, rows, n), lambda *g, layer=layer, last=last: (layer, jnp.minimum(step_of(*g), last), 0)))
        outs.append(pl.BlockSpec((rows, n), lambda *g, last=last: (jnp.minimum(step_of(*g), last), 0)))
        shapes.append(jax.ShapeDtypeStruct((k, n), BF16))
    return ins, outs, shapes


def _run_casts(ci_refs, co_refs):
    for ci, co in zip(ci_refs, co_refs):
        co[...] = ci[...].astype(BF16)


def _cast_kernel(*refs):
    half = len(refs) // 2
    _run_casts(refs[:half], refs[half:])


def _cast_call(casts):
    cast_in, cast_out, cast_shape = _cast_specs(casts, W_CHUNKS, lambda s: s)
    return pl.pallas_call(
        _cast_kernel,
        grid=(W_CHUNKS,),
        in_specs=cast_in,
        out_specs=cast_out,
        out_shape=cast_shape,
        compiler_params=_params("arbitrary"),
        name="cast_weights",
    )(*[arr for arr, _ in casts])


def _resident_call(body, *, name, m, tm, weights, row_ins, consts, outs, casts=(), scratch=()):
    nw, nk, nr, nc, no = len(weights), len(casts), len(row_ins), len(consts), len(outs)
    steps = m // tm

    def kern(*refs):
        pos = 0

        def take(n):
            nonlocal pos
            pos += n
            return refs[pos - n:pos]

        w_refs, ci_refs, r_refs, c_refs = take(nw), take(nk), take(nr), take(nc)
        o_refs, co_refs, s_refs = take(no), take(nk), take(len(scratch))
        _run_casts(ci_refs, co_refs)
        body(w_refs, r_refs, c_refs, o_refs, s_refs)

    in_specs, args = [], []
    for arr in weights:
        in_specs.append(pl.BlockSpec(arr.shape, lambda s: (0, 0), pipeline_mode=pl.Buffered(1)))
        args.append(arr)
    cast_in, cast_out, cast_shape = _cast_specs(casts, steps, lambda s: s)
    in_specs += cast_in
    args += [arr for arr, _ in casts]
    for arr, block, index_fn in row_ins:
        in_specs.append(pl.BlockSpec(block, index_fn))
        args.append(arr)
    for arr in consts:
        in_specs.append(pl.BlockSpec((1, arr.shape[1]), lambda s: (0, 0)))
        args.append(arr)
    out_specs = [pl.BlockSpec(block, index_fn) for _, _, block, index_fn in outs]
    out_shape = [jax.ShapeDtypeStruct(shape, dtype) for shape, dtype, _, _ in outs]
    return pl.pallas_call(
        kern,
        grid=(steps,),
        in_specs=in_specs,
        out_specs=out_specs + cast_out,
        out_shape=out_shape + cast_shape,
        scratch_shapes=list(scratch),
        compiler_params=_params("arbitrary"),
        name=name,
    )(*args)


def _rows(tm, n):
    return (tm, n), (lambda i: (i, 0))


def _gm_in(x, gain, wb16, *, tm):
    m, k = x.shape
    n = wb16.shape[1]

    def body(wb, r, c, o, scr):
        for rs in _sub_blocks(tm):
            h = _rms(r[0][rs, :], c[0][...]).astype(BF16)
            o[0][rs, :] = jax.nn.gelu(_dot(h, wb[0][...])).astype(BF16)

    blk_x, idx = _rows(tm, k)
    blk_o, _ = _rows(tm, n)
    return _resident_call(
        body, name="gm_in", m=m, tm=tm, weights=[wb16],
        row_ins=[(x, blk_x, idx)], consts=[gain.reshape(1, k)],
        outs=[((m, n), BF16, blk_o, idx)])[0]


def _gm_mix(z, ln_g, ln_b, w_s, b_s, layer, wb16, res, gain, *, tm):
    m, n = res.shape
    bs = jnp.broadcast_to(b_s[layer][:, :, None], (GM_HEADS, GM_CHUNK, GM_HEAD_DIM)).astype(F32)

    def body(wb, r, c, o, scr):
        u_ref, v_ref, res_ref, ws_ref, bs_ref = r
        row = jax.lax.broadcasted_iota(jnp.int32, (GM_CHUNK, GM_CHUNK), 0)
        col = jax.lax.broadcasted_iota(jnp.int32, (GM_CHUNK, GM_CHUNK), 1)
        wm = [jnp.where(row >= col, ws_ref[h], 0.0).astype(BF16) for h in range(GM_HEADS)]
        for rs in _sub_blocks(tm):
            for c0 in range(rs.start, rs.stop, GM_CHUNK):
                rows = slice(c0, c0 + GM_CHUNK)
                v = v_ref[rows, :].astype(F32)
                mu = jnp.mean(v, axis=-1, keepdims=True)
                vc = v - mu
                var = jnp.mean(vc * vc, axis=-1, keepdims=True)
                vn = (vc * jax.lax.rsqrt(var + EPS) * c[1][...] + c[2][...]).astype(BF16)
                for h in range(GM_HEADS):
                    cols = slice(h * GM_HEAD_DIM, (h + 1) * GM_HEAD_DIM)
                    sv = _dot(wm[h], vn[:, cols]) + bs_ref[h]
                    scr[0][rows, cols] = (u_ref[rows, cols].astype(F32) * sv).astype(BF16)
            xn = res_ref[rs, :] + _dot(scr[0][rs, :], wb[0][...])
            o[0][rs, :] = xn
            o[1][rs, :] = _rms(xn, c[0][...]).astype(BF16)

    blk_x, idx = _rows(tm, n)
    return _resident_call(
        body, name="gm_mix", m=m, tm=tm, weights=[wb16],
        row_ins=[(z, blk_x, idx), (z, blk_x, lambda i: (i, 1)), (res, blk_x, idx),
                 (w_s, (None, GM_HEADS, GM_CHUNK, GM_CHUNK), lambda i: (layer, 0, 0, 0)),
                 (bs, (GM_HEADS, GM_CHUNK, GM_HEAD_DIM), lambda i: (0, 0, 0))],
        consts=[gain.reshape(1, n), ln_g[layer].reshape(1, n), ln_b[layer].reshape(1, n)],
        outs=[((m, n), F32, blk_x, idx), ((m, n), BF16, blk_x, idx)],
        scratch=[pltpu.VMEM((tm, n), BF16)])


def _ffn_down(a, w2b16, res, casts, *, tm):
    m, n = res.shape

    def body(wb, r, c, o, scr):
        for rs in _sub_blocks(tm):
            o[0][rs, :] = r[1][rs, :] + _dot(r[0][rs, :], wb[0][...])

    blk_a, idx = _rows(tm, a.shape[1])
    blk_x, _ = _rows(tm, n)
    return _resident_call(
        body, name="ffn_down", m=m, tm=tm, weights=[w2b16], casts=casts,
        row_ins=[(a, blk_a, idx), (res, blk_x, idx)], consts=[],
        outs=[((m, n), F32, blk_x, idx)])


def _ple(x, gain, p, layer, wgb16, wpb16, final_gain, casts, *, tm):
    m, n = x.shape

    def body(wb, r, c, o, scr):
        for rs in _sub_blocks(tm):
            xf = r[0][rs, :]
            h = _rms(xf, c[0][...]).astype(BF16)
            gate = jax.nn.sigmoid(_dot(h, wb[0][...]))
            proj = _dot(r[1][rs, :].astype(BF16), wb[1][...])
            xn = xf + gate * proj
            if final_gain is not None:
                xn = _rms(xn, c[1][...])
            o[0][rs, :] = xn

    blk_x, idx = _rows(tm, n)
    blk_p, _ = _rows(tm, PLE_DIM)
    consts = [gain.reshape(1, n)] + ([] if final_gain is None else [final_gain.reshape(1, n)])
    return _resident_call(
        body, name="ple", m=m, tm=tm, weights=[wgb16, wpb16], casts=casts,
        row_ins=[(x, blk_x, idx), (p, (None,) + blk_p, lambda i: (layer, i, 0))], consts=consts,
        outs=[((m, n), F32, blk_x, idx)])


def _s5_in(x, gain, wb16, *, tm):
    m, n = x.shape
    tc = tm // S5_T

    def body(wb, r, c, o, scr):
        for rs in _sub_blocks(tm):
            h = _rms(r[0][rs, :], c[0][...]).astype(BF16)
            u = _dot(h, wb[0][...])
            for l in range(S5_TILES):
                scr[0][l, rs, :] = u[:, l * LANES:(l + 1) * LANES]
            sub = rs.stop - rs.start
            cs = slice(rs.start // S5_T, rs.stop // S5_T)
            for t in range(S5_T):
                for l in range(S5_TILES):
                    o[0][l, cs, t * LANES:(t + 1) * LANES] = (
                        scr[0][l, pl.ds(rs.start + t, sub // S5_T, stride=S5_T), :].astype(BF16))

    blk_x, idx = _rows(tm, n)
    return _resident_call(
        body, name="s5_in", m=m, tm=tm, weights=[wb16],
        row_ins=[(x, blk_x, idx)], consts=[gain.reshape(1, n)],
        outs=[((S5_TILES, m // S5_T, S5_T * LANES), BF16, (S5_TILES, tc, S5_T * LANES), lambda i: (0, i, 0))],
        scratch=[pltpu.VMEM((S5_TILES, tm, LANES), F32)])[0]


def _s5_out(y_tiles, wb16, res, gain, *, tm):
    m, n = res.shape
    tc = tm // S5_T

    def body(wb, r, c, o, scr):
        for rs in _sub_blocks(tm):
            sub = rs.stop - rs.start
            cs = slice(rs.start // S5_T, rs.stop // S5_T)
            for t in range(S5_T):
                for l in range(S5_TILES):
                    scr[0][l, pl.ds(rs.start + t, sub // S5_T, stride=S5_T), :] = (
                        r[0][l, cs, t * LANES:(t + 1) * LANES].astype(F32))
            y = jnp.concatenate([scr[0][l, rs, :] for l in range(S5_TILES)], axis=1).astype(BF16)
            vg = _dot(y, wb[0][...])
            xn = r[1][rs, :] + vg[:, :n] * jax.nn.sigmoid(vg[:, n:])
            o[0][rs, :] = xn
            o[1][rs, :] = _rms(xn, c[0][...]).astype(BF16)

    blk_x, idx = _rows(tm, n)
    return _resident_call(
        body, name="s5_out", m=m, tm=tm, weights=[wb16],
        row_ins=[(y_tiles, (S5_TILES, tc, S5_T * LANES), lambda i: (0, i, 0)), (res, blk_x, idx)],
        consts=[gain.reshape(1, n)],
        outs=[((m, n), F32, blk_x, idx), ((m, n), BF16, blk_x, idx)],
        scratch=[pltpu.VMEM((S5_TILES, tm, LANES), F32)])


def _ffn_up_kernel(*refs, tm, n_casts):
    h_ref, w1_ref, w3_ref = refs[:3]
    ci_refs = refs[3:3 + n_casts]
    o_ref = refs[3 + n_casts]
    co_refs = refs[4 + n_casts:4 + 2 * n_casts]
    w1_scr, w3_scr = refs[4 + 2 * n_casts:]
    _run_casts(ci_refs, co_refs)

    @pl.when(pl.program_id(1) == 0)
    def _():
        w1_scr[...] = w1_ref[...].astype(BF16)
        w3_scr[...] = w3_ref[...].astype(BF16)

    for rs in _sub_blocks(tm):
        h = h_ref[rs, :]
        o_ref[rs, :] = (jax.nn.silu(_dot(h, w1_scr[...])) * _dot(h, w3_scr[...])).astype(o_ref.dtype)


def _ffn_up(h, w1, w3, layer, casts, *, tm, tf):
    m, d = h.shape
    ff = w1.shape[2]
    ni = m // tm
    cast_in, cast_out, cast_shape = _cast_specs(casts, (ff // tf) * ni, lambda f, i: f * ni + i)
    return pl.pallas_call(
        functools.partial(_ffn_up_kernel, tm=tm, n_casts=len(casts)),
        grid=(ff // tf, ni),
        in_specs=[
            pl.BlockSpec((tm, d), lambda f, i: (i, 0)),
            pl.BlockSpec((None, d, tf), lambda f, i: (layer, 0, f)),
            pl.BlockSpec((None, d, tf), lambda f, i: (layer, 0, f)),
        ] + cast_in,
        out_specs=[pl.BlockSpec((tm, tf), lambda f, i: (i, f))] + cast_out,
        out_shape=[jax.ShapeDtypeStruct((m, ff), BF16)] + cast_shape,
        scratch_shapes=[pltpu.VMEM((d, tf), BF16), pltpu.VMEM((d, tf), BF16)],
        compiler_params=_params("arbitrary", "arbitrary"),
        name="ffn_up",
    )(h, w1, w3, *[arr for arr, _ in casts])


def _piece_transpose(v):
    piece = jax.lax.broadcasted_iota(jnp.int32, v[0].shape, 1) // S5_GROUP
    for delta in (4, 2, 1):
        keep = (piece & delta) == 0
        shift = delta * S5_GROUP
        new = list(v)
        for i in range(S5_GB):
            if i & delta == 0:
                a, b = v[i], v[i + delta]
                new[i] = jnp.where(keep, a, pltpu.roll(b, shift, axis=1))
                new[i + delta] = jnp.where(keep, pltpu.roll(a, LANES - shift, axis=1), b)
        v = new
    return v


def _toeplitz_rows(strip, s):
    lo, hi = strip[:, :LANES], strip[:, LANES:]
    lane = jax.lax.broadcasted_iota(jnp.int32, lo.shape, 1)
    shift = (s * S5_GROUP) % LANES
    if s == 0:
        return strip
    if s * S5_GROUP < LANES:
        lo_r, hi_r = pltpu.roll(lo, shift, axis=1), pltpu.roll(hi, shift, axis=1)
        return jnp.concatenate([jnp.where(lane >= shift, lo_r, 0.0),
                                jnp.where(lane >= shift, hi_r, lo_r)], axis=1)
    lo_r = lo if shift == 0 else pltpu.roll(lo, shift, axis=1)
    return jnp.concatenate([jnp.zeros_like(lo), jnp.where(lane >= shift, lo_r, 0.0)], axis=1)


def _s5_kernel(x_ref, bb_ref, tap_ref, skip_ref, min_ref, mout_ref, sc_ref, o_ref, mi_scr, *, chunks_per_seq):
    rows = x_ref.shape[1]
    n_idx = jax.lax.broadcasted_iota(jnp.int32, (rows, S5_RI), 0) % chunks_per_seq
    halves = S5_T // S5_GB
    for gi in range(S5_GB):
        strip = jnp.dot(bb_ref[gi], tap_ref[gi], preferred_element_type=F32,
                        precision=jax.lax.Precision.HIGHEST) + skip_ref[gi]
        for s in range(S5_T):
            mi_scr[gi, s * S5_GROUP:(s + 1) * S5_GROUP, :] = _toeplitz_rows(strip, s).astype(BF16)
    xin = [_piece_transpose([x_ref[0, :, (S5_GB * h + i) * LANES:(S5_GB * h + i + 1) * LANES].astype(F32)
                             for i in range(S5_GB)]) for h in range(halves)]
    ys = []
    for gi in range(S5_GB):
        xg = jnp.concatenate([xin[h][gi] for h in range(halves)], axis=1).astype(BF16)
        y = _dot(xg, mi_scr[gi])
        s = _dot(xg, min_ref[gi])
        sc = sc_ref[gi]
        for k in range(S5_SCAN_STEPS):
            d = 1 << k
            sh = jnp.where(n_idx >= d, pltpu.roll(s, d, axis=0), 0.0)
            s = s + sc[k:k + 1, :] * sh + sc[8 + k:9 + k, :] * pltpu.roll(sh, S5_STATE, axis=1)
        s_prev = jnp.where(n_idx >= 1, pltpu.roll(s, 1, axis=0), 0.0)
        y = y + _dot(s_prev.astype(BF16), mout_ref[gi])
        ys.append(jax.nn.gelu(y))
    for h in range(halves):
        out = _piece_transpose([ys[gi][:, h * LANES:(h + 1) * LANES] for gi in range(S5_GB)])
        for i in range(S5_GB):
            t = S5_GB * h + i
            o_ref[0, :, t * LANES:(t + 1) * LANES] = out[i].astype(o_ref.dtype)


def _s5_core(x_tiles, bb, tap, skip, m_in, m_out, sc, *, chunks_per_seq):
    tiles, rows, width = x_tiles.shape
    return pl.pallas_call(
        functools.partial(_s5_kernel, chunks_per_seq=chunks_per_seq),
        grid=(tiles,),
        in_specs=[
            pl.BlockSpec((1, rows, width), lambda i: (i, 0, 0)),
            pl.BlockSpec((S5_GB, S5_GROUP, S5_RI), lambda i: (i, 0, 0)),
            pl.BlockSpec((S5_GB, S5_RI, S5_TC), lambda i: (i, 0, 0)),
            pl.BlockSpec((S5_GB, S5_GROUP, S5_TC), lambda i: (i, 0, 0)),
            pl.BlockSpec((S5_GB, S5_TC, S5_RI), lambda i: (i, 0, 0)),
            pl.BlockSpec((S5_GB, S5_RI, S5_TC), lambda i: (i, 0, 0)),
            pl.BlockSpec((S5_GB, 16, S5_RI), lambda i: (i, 0, 0)),
        ],
        out_specs=pl.BlockSpec((1, rows, width), lambda i: (i, 0, 0)),
        out_shape=jax.ShapeDtypeStruct((tiles, rows, width), BF16),
        scratch_shapes=[pltpu.VMEM((S5_GB, S5_TC, S5_TC), BF16)],
        compiler_params=_params("parallel"),
        name="s5_core",
    )(x_tiles, bb, tap, skip, m_in, m_out, sc)


def _s5_tables(a_re, a_im, log_dt, b_re, b_im, c_re, c_im, d):
    a_re, a_im = a_re.astype(F32), a_im.astype(F32)
    dt = jnp.exp(log_dt.astype(F32))[:, None]
    mag = jnp.exp(a_re * dt)
    lb_re, lb_im = mag * jnp.cos(a_im * dt), mag * jnp.sin(a_im * dt)
    den = a_re * a_re + a_im * a_im
    q_re = ((lb_re - 1.0) * a_re + lb_im * a_im) / den
    q_im = (lb_im * a_re - (lb_re - 1.0) * a_im) / den
    b_re = b_re.astype(F32).transpose(0, 2, 1)
    b_im = b_im.astype(F32).transpose(0, 2, 1)
    bb_re = q_re[:, None, :] * b_re - q_im[:, None, :] * b_im
    bb_im = q_re[:, None, :] * b_im + q_im[:, None, :] * b_re
    ks = jnp.arange(S5_T + 1, dtype=F32)[:, None, None]
    pmag = jnp.exp(ks * (a_re * dt))
    pk_re = pmag * jnp.cos(ks * (a_im * dt))
    pk_im = pmag * jnp.sin(ks * (a_im * dt))
    bb = jnp.concatenate([bb_re, bb_im], axis=2)
    pr_re, pr_im = pk_re[S5_T - 1::-1], pk_im[S5_T - 1::-1]
    e_re = pr_re[:, :, None, :] * bb_re - pr_im[:, :, None, :] * bb_im
    e_im = pr_re[:, :, None, :] * bb_im + pr_im[:, :, None, :] * bb_re
    m_in = jnp.concatenate([e_re, e_im], axis=3)
    m_in = m_in.transpose(1, 0, 2, 3).reshape(S5_GROUPS, S5_TC, S5_RI).astype(BF16)
    hi = jax.lax.Precision.HIGHEST
    ct_re = c_re.astype(F32).transpose(0, 2, 1)
    ct_im = c_im.astype(F32).transpose(0, 2, 1)
    pt_re, pt_im = pk_re.transpose(1, 2, 0), pk_im.transpose(1, 2, 0)
    lane = jnp.arange(S5_TC)
    rep_c = (lane[None, :] % S5_GROUP == jnp.arange(S5_GROUP)[:, None]).astype(F32)
    rep_k = (lane[None, :] // S5_GROUP == jnp.arange(S5_T)[:, None]).astype(F32)
    cl_re, cl_im = jnp.dot(ct_re, rep_c, precision=hi), jnp.dot(ct_im, rep_c, precision=hi)

    def readout(k0):
        pl_re = jnp.dot(pt_re[:, :, k0:k0 + S5_T], rep_k, precision=hi)
        pl_im = jnp.dot(pt_im[:, :, k0:k0 + S5_T], rep_k, precision=hi)
        return jnp.concatenate([cl_re * pl_re - cl_im * pl_im, -(cl_re * pl_im + cl_im * pl_re)], axis=1)

    tap = readout(0)
    m_out = readout(1).astype(BF16)
    skip = d.astype(F32).reshape(S5_GROUPS, 1, S5_GROUP) * jnp.eye(S5_GROUP, dtype=F32)
    skip = jnp.pad(skip, ((0, 0), (0, 0), (0, S5_TC - S5_GROUP)))
    f_re, f_im = pk_re[S5_T], pk_im[S5_T]
    rows_a, rows_b = [], []
    for _ in range(S5_SCAN_STEPS):
        rows_a.append(jnp.concatenate([f_re, f_re], axis=1))
        rows_b.append(jnp.concatenate([-f_im, f_im], axis=1))
        f_re, f_im = f_re * f_re - f_im * f_im, 2.0 * f_re * f_im
    zero = jnp.zeros_like(rows_a[0])
    sc = jnp.stack(rows_a + [zero] + rows_b + [zero], axis=1)
    return bb, tap, skip, m_in, m_out, sc


def kernel(x, p, norm_mix, norm_ffn, norm_ple, norm_final, gm_w_in, gm_ln_g, gm_ln_b, gm_w_s, gm_b_s, gm_w_out, s5_w_in, s5_a_re, s5_a_im, s5_log_dt, s5_b_re, s5_b_im, s5_c_re, s5_c_im, s5_d, s5_w_out, ffn_w1, ffn_w3, ffn_w2, ple_w_gate, ple_w_proj):
    bsz, seq, d = x.shape
    depth = p.shape[0]
    m = bsz * seq
    xs = x.reshape(m, d)
    ps = p.reshape(depth, m, PLE_DIM)
    s5_in_b16 = s5_out_b16 = None
    for i in range(depth):
        j = i // 2
        if i % 2 == 0:
            gm_in_b16, gm_out_b16 = _cast_call([(gm_w_in, j), (gm_w_out, j)])
            z = _gm_in(xs, norm_mix[i], gm_in_b16, tm=512)
            xs, hn = _gm_mix(z, gm_ln_g, gm_ln_b, gm_w_s, gm_b_s, j, gm_out_b16, xs, norm_ffn[i], tm=512)
        else:
            tables = _s5_tables(s5_a_re[j], s5_a_im[j], s5_log_dt[j], s5_b_re[j], s5_b_im[j],
                                s5_c_re[j], s5_c_im[j], s5_d[j])
            u = _s5_in(xs, norm_mix[i], s5_in_b16, tm=512)
            y = _s5_core(u, *tables, chunks_per_seq=seq // S5_T)
            xs, hn = _s5_out(y, s5_out_b16, xs, norm_ffn[i], tm=256)
        a, w2_b16, wg_b16, wp_b16 = _ffn_up(
            hn, ffn_w1, ffn_w3, i, [(ffn_w2, i), (ple_w_gate, i), (ple_w_proj, i)], tm=2048, tf=512)
        next_s5 = i + 1 < depth and (i + 1) % 2 == 1
        xs, *cast = _ffn_down(a, w2_b16, xs, [(s5_w_in, (i + 1) // 2)] if next_s5 else [], tm=256)
        if next_s5:
            s5_in_b16 = cast[0]
        xs, *cast = _ple(xs, norm_ple[i], ps, i, wg_b16, wp_b16,
                         norm_final if i == depth - 1 else None,
                         [(s5_w_out, (i + 1) // 2)] if next_s5 else [], tm=512)
        if next_s5:
            s5_out_b16 = cast[0]
    return xs.reshape(bsz, seq, d)
```

```python
import functools

import jax
import jax.numpy as jnp
from jax.experimental import pallas as pl
from jax.experimental.pallas import tpu as pltpu

F32 = jnp.float32
BF16 = jnp.bfloat16

D_MODEL = 2048
PLE_DIM = 256
EPS = 1e-6
LANES = 128
SUBLANES = 8
BF16_ROWS = 16

GM_CHUNK = 128
GM_HEAD_DIM = 128
GM_HEADS = D_MODEL // GM_HEAD_DIM

S5_GROUP = 16
S5_GROUPS = D_MODEL // S5_GROUP
S5_STATE = 64
S5_T = 16
S5_TC = S5_T * S5_GROUP
S5_RI = 2 * S5_STATE
S5_GB = LANES // S5_GROUP
S5_TILES = D_MODEL // LANES
S5_SCAN_STEPS = 7

W_CHUNKS = 16
SUB_ROWS = 256
VMEM_LIMIT = 56 * 1024 * 1024


def _params(*sem):
    return pltpu.CompilerParams(dimension_semantics=sem, vmem_limit_bytes=VMEM_LIMIT)


def _rms(xf, gain):
    ms = jnp.mean(xf * xf, axis=-1, keepdims=True)
    return xf * jax.lax.rsqrt(ms + EPS) * gain


def _dot(a, b):
    return jnp.dot(a, b, preferred_element_type=F32)


def _sub_blocks(tm):
    sub = min(tm, SUB_ROWS)
    return [slice(r, r + sub) for r in range(0, tm, sub)]


def _cast_rows(k, steps):
    rows = BF16_ROWS
    while k % rows or k // rows > steps:
        rows += BF16_ROWS
    return rows


def _cast_specs(casts, steps, step_of):
    ins, outs, shapes = [], [], []
    for arr, layer in casts:
        _, k, n = arr.shape
        rows = _cast_rows(k, steps)
        last = k // rows - 1
        ins.append(pl.BlockSpec(
            (None, rows, n), lambda *g, layer=layer, last=last: (layer, jnp.minimum(step_of(*g), last), 0)))
        outs.append(pl.BlockSpec((rows, n), lambda *g, last=last: (jnp.minimum(step_of(*g), last), 0)))
        shapes.append(jax.ShapeDtypeStruct((k, n), BF16))
    return ins, outs, shapes


def _run_casts(ci_refs, co_refs):
    for ci, co in zip(ci_refs, co_refs):
        co[...] = ci[...].astype(BF16)


def _cast_kernel(*refs):
    half = len(refs) // 2
    _run_casts(refs[:half], refs[half:])


def _cast_call(casts):
    cast_in, cast_out, cast_shape = _cast_specs(casts, W_CHUNKS, lambda s: s)
    return pl.pallas_call(
        _cast_kernel,
        grid=(W_CHUNKS,),
        in_specs=cast_in,
        out_specs=cast_out,
        out_shape=cast_shape,
        compiler_params=_params("arbitrary"),
        name="cast_weights",
    )(*[arr for arr, _ in casts])


def _resident_call(body, *, name, m, tm, weights, row_ins, consts, outs, casts=(), scratch=()):
    nw, nk, nr, nc, no = len(weights), len(casts), len(row_ins), len(consts), len(outs)
    steps = m // tm

    def kern(*refs):
        pos = 0

        def take(n):
            nonlocal pos
            pos += n
            return refs[pos - n:pos]

        w_refs, ci_refs, r_refs, c_refs = take(nw), take(nk), take(nr), take(nc)
        o_refs, co_refs, s_refs = take(no), take(nk), take(len(scratch))
        _run_casts(ci_refs, co_refs)
        body(w_refs, r_refs, c_refs, o_refs, s_refs)

    in_specs, args = [], []
    for arr in weights:
        in_specs.append(pl.BlockSpec(arr.shape, lambda s: (0, 0), pipeline_mode=pl.Buffered(1)))
        args.append(arr)
    cast_in, cast_out, cast_shape = _cast_specs(casts, steps, lambda s: s)
    in_specs += cast_in
    args += [arr for arr, _ in casts]
    for arr, block, index_fn in row_ins:
        in_specs.append(pl.BlockSpec(block, index_fn))
        args.append(arr)
    for arr in consts:
        in_specs.append(pl.BlockSpec((1, arr.shape[1]), lambda s: (0, 0)))
        args.append(arr)
    out_specs = [pl.BlockSpec(block, index_fn) for _, _, block, index_fn in outs]
    out_shape = [jax.ShapeDtypeStruct(shape, dtype) for shape, dtype, _, _ in outs]
    return pl.pallas_call(
        kern,
        grid=(steps,),
        in_specs=in_specs,
        out_specs=out_specs + cast_out,
        out_shape=out_shape + cast_shape,
        scratch_shapes=list(scratch),
        compiler_params=_params("arbitrary"),
        name=name,
    )(*args)


def _rows(tm, n):
    return (tm, n), (lambda i: (i, 0))


def _gm_in(x, gain, wb16, *, tm):
    m, k = x.shape
    n = wb16.shape[1]

    def body(wb, r, c, o, scr):
        for rs in _sub_blocks(tm):
            h = _rms(r[0][rs, :], c[0][...]).astype(BF16)
            o[0][rs, :] = jax.nn.gelu(_dot(h, wb[0][...])).astype(BF16)

    blk_x, idx = _rows(tm, k)
    blk_o, _ = _rows(tm, n)
    return _resident_call(
        body, name="gm_in", m=m, tm=tm, weights=[wb16],
        row_ins=[(x, blk_x, idx)], consts=[gain.reshape(1, k)],
        outs=[((m, n), BF16, blk_o, idx)])[0]


def _gm_mix(z, ln_g, ln_b, w_s, b_s, layer, wb16, res, gain, *, tm):
    m, n = res.shape
    bs = jnp.broadcast_to(b_s[layer][:, :, None], (GM_HEADS, GM_CHUNK, GM_HEAD_DIM)).astype(F32)

    def body(wb, r, c, o, scr):
        u_ref, v_ref, res_ref, ws_ref, bs_ref = r
        row = jax.lax.broadcasted_iota(jnp.int32, (GM_CHUNK, GM_CHUNK), 0)
        col = jax.lax.broadcasted_iota(jnp.int32, (GM_CHUNK, GM_CHUNK), 1)
        wm = [jnp.where(row >= col, ws_ref[h], 0.0).astype(BF16) for h in range(GM_HEADS)]
        for rs in _sub_blocks(tm):
            for c0 in range(rs.start, rs.stop, GM_CHUNK):
                rows = slice(c0, c0 + GM_CHUNK)
                v = v_ref[rows, :].astype(F32)
                mu = jnp.mean(v, axis=-1, keepdims=True)
                vc = v - mu
                var = jnp.mean(vc * vc, axis=-1, keepdims=True)
                vn = (vc * jax.lax.rsqrt(var + EPS) * c[1][...] + c[2][...]).astype(BF16)
                for h in range(GM_HEADS):
                    cols = slice(h * GM_HEAD_DIM, (h + 1) * GM_HEAD_DIM)
                    sv = _dot(wm[h], vn[:, cols]) + bs_ref[h]
                    scr[0][rows, cols] = (u_ref[rows, cols].astype(F32) * sv).astype(BF16)
            xn = res_ref[rs, :] + _dot(scr[0][rs, :], wb[0][...])
            o[0][rs, :] = xn
            o[1][rs, :] = _rms(xn, c[0][...]).astype(BF16)

    blk_x, idx = _rows(tm, n)
    return _resident_call(
        body, name="gm_mix", m=m, tm=tm, weights=[wb16],
        row_ins=[(z, blk_x, idx), (z, blk_x, lambda i: (i, 1)), (res, blk_x, idx),
                 (w_s, (None, GM_HEADS, GM_CHUNK, GM_CHUNK), lambda i: (layer, 0, 0, 0)),
                 (bs, (GM_HEADS, GM_CHUNK, GM_HEAD_DIM), lambda i: (0, 0, 0))],
        consts=[gain.reshape(1, n), ln_g[layer].reshape(1, n), ln_b[layer].reshape(1, n)],
        outs=[((m, n), F32, blk_x, idx), ((m, n), BF16, blk_x, idx)],
        scratch=[pltpu.VMEM((tm, n), BF16)])


def _ffn_down(a, w2b16, res, casts, *, tm):
    m, n = res.shape

    def body(wb, r, c, o, scr):
        for rs in _sub_blocks(tm):
            o[0][rs, :] = r[1][rs, :] + _dot(r[0][rs, :], wb[0][...])

    blk_a, idx = _rows(tm, a.shape[1])
    blk_x, _ = _rows(tm, n)
    return _resident_call(
        body, name="ffn_down", m=m, tm=tm, weights=[w2b16], casts=casts,
        row_ins=[(a, blk_a, idx), (res, blk_x, idx)], consts=[],
        outs=[((m, n), F32, blk_x, idx)])


def _ple(x, gain, p, layer, wgb16, wpb16, final_gain, casts, *, tm):
    m, n = x.shape

    def body(wb, r, c, o, scr):
        for rs in _sub_blocks(tm):
            xf = r[0][rs, :]
            h = _rms(xf, c[0][...]).astype(BF16)
            gate = jax.nn.sigmoid(_dot(h, wb[0][...]))
            proj = _dot(r[1][rs, :].astype(BF16), wb[1][...])
            xn = xf + gate * proj
            if final_gain is not None:
                xn = _rms(xn, c[1][...])
            o[0][rs, :] = xn

    blk_x, idx = _rows(tm, n)
    blk_p, _ = _rows(tm, PLE_DIM)
    consts = [gain.reshape(1, n)] + ([] if final_gain is None else [final_gain.reshape(1, n)])
    return _resident_call(
        body, name="ple", m=m, tm=tm, weights=[wgb16, wpb16], casts=casts,
        row_ins=[(x, blk_x, idx), (p, (None,) + blk_p, lambda i: (layer, i, 0))], consts=consts,
        outs=[((m, n), F32, blk_x, idx)])


def _s5_in(x, gain, wb16, *, tm):
    m, n = x.shape
    tc = tm // S5_T

    def body(wb, r, c, o, scr):
        for rs in _sub_blocks(tm):
            h = _rms(r[0][rs, :], c[0][...]).astype(BF16)
            u = _dot(h, wb[0][...])
            for l in range(S5_TILES):
                scr[0][l, rs, :] = u[:, l * LANES:(l + 1) * LANES]
            sub = rs.stop - rs.start
            cs = slice(rs.start // S5_T, rs.stop // S5_T)
            for t in range(S5_T):
                for l in range(S5_TILES):
                    o[0][l, cs, t * LANES:(t + 1) * LANES] = (
                        scr[0][l, pl.ds(rs.start + t, sub // S5_T, stride=S5_T), :].astype(BF16))

    blk_x, idx = _rows(tm, n)
    return _resident_call(
        body, name="s5_in", m=m, tm=tm, weights=[wb16],
        row_ins=[(x, blk_x, idx)], consts=[gain.reshape(1, n)],
        outs=[((S5_TILES, m // S5_T, S5_T * LANES), BF16, (S5_TILES, tc, S5_T * LANES), lambda i: (0, i, 0))],
        scratch=[pltpu.VMEM((S5_TILES, tm, LANES), F32)])[0]


def _s5_out(y_tiles, wb16, res, gain, *, tm):
    m, n = res.shape
    tc = tm // S5_T

    def body(wb, r, c, o, scr):
        for rs in _sub_blocks(tm):
            sub = rs.stop - rs.start
            cs = slice(rs.start // S5_T, rs.stop // S5_T)
            for t in range(S5_T):
                for l in range(S5_TILES):
                    scr[0][l, pl.ds(rs.start + t, sub // S5_T, stride=S5_T), :] = (
                        r[0][l, cs, t * LANES:(t + 1) * LANES].astype(F32))
            y = jnp.concatenate([scr[0][l, rs, :] for l in range(S5_TILES)], axis=1).astype(BF16)
            vg = _dot(y, wb[0][...])
            xn = r[1][rs, :] + vg[:, :n] * jax.nn.sigmoid(vg[:, n:])
            o[0][rs, :] = xn
            o[1][rs, :] = _rms(xn, c[0][...]).astype(BF16)

    blk_x, idx = _rows(tm, n)
    return _resident_call(
        body, name="s5_out", m=m, tm=tm, weights=[wb16],
        row_ins=[(y_tiles, (S5_TILES, tc, S5_T * LANES), lambda i: (0, i, 0)), (res, blk_x, idx)],
        consts=[gain.reshape(1, n)],
        outs=[((m, n), F32, blk_x, idx), ((m, n), BF16, blk_x, idx)],
        scratch=[pltpu.VMEM((S5_TILES, tm, LANES), F32)])


def _ffn_up_kernel(*refs, tm, n_casts):
    h_ref, w1_ref, w3_ref = refs[:3]
    ci_refs = refs[3:3 + n_casts]
    o_ref = refs[3 + n_casts]
    co_refs = refs[4 + n_casts:4 + 2 * n_casts]
    w1_scr, w3_scr = refs[4 + 2 * n_casts:]
    _run_casts(ci_refs, co_refs)

    @pl.when(pl.program_id(1) == 0)
    def _():
        w1_scr[...] = w1_ref[...].astype(BF16)
        w3_scr[...] = w3_ref[...].astype(BF16)

    for rs in _sub_blocks(tm):
        h = h_ref[rs, :]
        o_ref[rs, :] = (jax.nn.silu(_dot(h, w1_scr[...])) * _dot(h, w3_scr[...])).astype(o_ref.dtype)


def _ffn_up(h, w1, w3, layer, casts, *, tm, tf):
    m, d = h.shape
    ff = w1.shape[2]
    ni = m // tm
    cast_in, cast_out, cast_shape = _cast_specs(casts, (ff // tf) * ni, lambda f, i: f * ni + i)
    return pl.pallas_call(
        functools.partial(_ffn_up_kernel, tm=tm, n_casts=len(casts)),
        grid=(ff // tf, ni),
        in_specs=[
            pl.BlockSpec((tm, d), lambda f, i: (i, 0)),
            pl.BlockSpec((None, d, tf), lambda f, i: (layer, 0, f)),
            pl.BlockSpec((None, d, tf), lambda f, i: (layer, 0, f)),
        ] + cast_in,
        out_specs=[pl.BlockSpec((tm, tf), lambda f, i: (i, f))] + cast_out,
        out_shape=[jax.ShapeDtypeStruct((m, ff), BF16)] + cast_shape,
        scratch_shapes=[pltpu.VMEM((d, tf), BF16), pltpu.VMEM((d, tf), BF16)],
        compiler_params=_params("arbitrary", "arbitrary"),
        name="ffn_up",
    )(h, w1, w3, *[arr for arr, _ in casts])


def _piece_transpose(v):
    piece = jax.lax.broadcasted_iota(jnp.int32, v[0].shape, 1) // S5_GROUP
    for delta in (4, 2, 1):
        keep = (piece & delta) == 0
        shift = delta * S5_GROUP
        new = list(v)
        for i in range(S5_GB):
            if i & delta == 0:
                a, b = v[i], v[i + delta]
                new[i] = jnp.where(keep, a, pltpu.roll(b, shift, axis=1))
                new[i + delta] = jnp.where(keep, pltpu.roll(a, LANES - shift, axis=1), b)
        v = new
    return v


def _toeplitz_rows(strip, s):
    lo, hi = strip[:, :LANES], strip[:, LANES:]
    lane = jax.lax.broadcasted_iota(jnp.int32, lo.shape, 1)
    shift = (s * S5_GROUP) % LANES
    if s == 0:
        return strip
    if s * S5_GROUP < LANES:
        lo_r, hi_r = pltpu.roll(lo, shift, axis=1), pltpu.roll(hi, shift, axis=1)
        return jnp.concatenate([jnp.where(lane >= shift, lo_r, 0.0),
                                jnp.where(lane >= shift, hi_r, lo_r)], axis=1)
    lo_r = lo if shift == 0 else pltpu.roll(lo, shift, axis=1)
    return jnp.concatenate([jnp.zeros_like(lo), jnp.where(lane >= shift, lo_r, 0.0)], axis=1)


def _cmul_add(s, fa, fb, x):
    return s + fa * x + fb * pltpu.roll(x, S5_STATE, axis=1)


def _chunk_scan(ss, sc_ref, sf_ref, s_scr, c_scr, *, chunks_per_seq):
    groups = range(len(ss))
    rows = ss[0].shape[0]
    tiles = rows // SUBLANES
    r_idx = jax.lax.broadcasted_iota(jnp.int32, ss[0].shape, 0) % SUBLANES
    for k in range(3):
        d = 1 << k
        ss = [_cmul_add(ss[g], sc_ref[g, k:k + 1, :], sc_ref[g, 8 + k:9 + k, :],
                        jnp.where(r_idx >= d, pltpu.roll(ss[g], d, axis=0), 0.0)) for g in groups]
    for g in groups:
        s_scr[g] = ss[g]
    ts = [s_scr[g, pl.ds(SUBLANES - 1, tiles, stride=SUBLANES), :] for g in groups]
    j_idx = jax.lax.broadcasted_iota(jnp.int32, ts[0].shape, 0) % (chunks_per_seq // SUBLANES)
    for k in range(3, S5_SCAN_STEPS):
        d = 1 << (k - 3)
        ts = [_cmul_add(ts[g], sc_ref[g, k:k + 1, :], sc_ref[g, 8 + k:9 + k, :],
                        jnp.where(j_idx >= d, pltpu.roll(ts[g], d, axis=0), 0.0)) for g in groups]
    for g in groups:
        carry = jnp.where(j_idx >= 1, pltpu.roll(ts[g], 1, axis=0), 0.0)
        c_scr[g, 0] = carry
        c_scr[g, 1] = pltpu.roll(carry, S5_STATE, axis=1)
    outs = []
    for g in groups:
        fa, fb = sf_ref[g, :SUBLANES, :], sf_ref[g, SUBLANES:, :]
        out = []
        for j in range(tiles):
            cb = jnp.broadcast_to(c_scr[g, 0, j:j + 1, :], (SUBLANES, S5_RI))
            cs = jnp.broadcast_to(c_scr[g, 1, j:j + 1, :], (SUBLANES, S5_RI))
            out.append(ss[g][j * SUBLANES:(j + 1) * SUBLANES] + fa * cb + fb * cs)
        outs.append(jnp.concatenate(out, axis=0))
    return outs


def _s5_kernel(x_ref, bb_ref, tap_ref, skip_ref, min_ref, mout_ref, sc_ref, sf_ref, o_ref,
               mi_scr, s_scr, c_scr, *, chunks_per_seq):
    rows = x_ref.shape[1]
    n_idx = jax.lax.broadcasted_iota(jnp.int32, (rows, S5_RI), 0) % chunks_per_seq
    halves = S5_T // S5_GB
    for gi in range(S5_GB):
        strip = jnp.dot(bb_ref[gi], tap_ref[gi], preferred_element_type=F32,
                        precision=jax.lax.Precision.HIGHEST) + skip_ref[gi]
        for s in range(S5_T):
            mi_scr[gi, s * S5_GROUP:(s + 1) * S5_GROUP, :] = _toeplitz_rows(strip, s).astype(BF16)
    xin = [_piece_transpose([x_ref[0, :, (S5_GB * h + i) * LANES:(S5_GB * h + i + 1) * LANES]
                             for i in range(S5_GB)]) for h in range(halves)]
    groups = range(S5_GB)
    xg = [jnp.concatenate([xin[h][g] for h in range(halves)], axis=1) for g in groups]
    ys = [_dot(xg[g], mi_scr[g]) for g in groups]
    ss = [_dot(xg[g], min_ref[g]) for g in groups]
    ss = _chunk_scan(ss, sc_ref, sf_ref, s_scr, c_scr, chunks_per_seq=chunks_per_seq)
    for g in groups:
        s_prev = jnp.where(n_idx >= 1, pltpu.roll(ss[g], 1, axis=0), 0.0)
        ys[g] = jax.nn.gelu(ys[g] + _dot(s_prev.astype(BF16), mout_ref[g])).astype(o_ref.dtype)
    for h in range(halves):
        out = _piece_transpose([ys[gi][:, h * LANES:(h + 1) * LANES] for gi in range(S5_GB)])
        for i in range(S5_GB):
            t = S5_GB * h + i
            o_ref[0, :, t * LANES:(t + 1) * LANES] = out[i]


def _s5_core(x_tiles, bb, tap, skip, m_in, m_out, sc, sf, *, chunks_per_seq):
    tiles, rows, width = x_tiles.shape
    return pl.pallas_call(
        functools.partial(_s5_kernel, chunks_per_seq=chunks_per_seq),
        grid=(tiles,),
        in_specs=[
            pl.BlockSpec((1, rows, width), lambda i: (i, 0, 0)),
            pl.BlockSpec((S5_GB, S5_GROUP, S5_RI), lambda i: (i, 0, 0)),
            pl.BlockSpec((S5_GB, S5_RI, S5_TC), lambda i: (i, 0, 0)),
            pl.BlockSpec((S5_GB, S5_GROUP, S5_TC), lambda i: (i, 0, 0)),
            pl.BlockSpec((S5_GB, S5_TC, S5_RI), lambda i: (i, 0, 0)),
            pl.BlockSpec((S5_GB, S5_RI, S5_TC), lambda i: (i, 0, 0)),
            pl.BlockSpec((S5_GB, 2 * SUBLANES, S5_RI), lambda i: (i, 0, 0)),
            pl.BlockSpec((S5_GB, 2 * SUBLANES, S5_RI), lambda i: (i, 0, 0)),
        ],
        out_specs=pl.BlockSpec((1, rows, width), lambda i: (i, 0, 0)),
        out_shape=jax.ShapeDtypeStruct((tiles, rows, width), BF16),
        scratch_shapes=[pltpu.VMEM((S5_GB, S5_TC, S5_TC), BF16),
                        pltpu.VMEM((S5_GB, rows, S5_RI), F32),
                        pltpu.VMEM((S5_GB, 2, rows // SUBLANES, S5_RI), F32)],
        compiler_params=_params("parallel"),
        name="s5_core",
    )(x_tiles, bb, tap, skip, m_in, m_out, sc, sf)


def _s5_tables(a_re, a_im, log_dt, b_re, b_im, c_re, c_im, d):
    a_re, a_im = a_re.astype(F32), a_im.astype(F32)
    dt = jnp.exp(log_dt.astype(F32))[:, None]
    mag = jnp.exp(a_re * dt)
    lb_re, lb_im = mag * jnp.cos(a_im * dt), mag * jnp.sin(a_im * dt)
    den = a_re * a_re + a_im * a_im
    q_re = ((lb_re - 1.0) * a_re + lb_im * a_im) / den
    q_im = (lb_im * a_re - (lb_re - 1.0) * a_im) / den
    b_re = b_re.astype(F32).transpose(0, 2, 1)
    b_im = b_im.astype(F32).transpose(0, 2, 1)
    bb_re = q_re[:, None, :] * b_re - q_im[:, None, :] * b_im
    bb_im = q_re[:, None, :] * b_im + q_im[:, None, :] * b_re
    ks = jnp.arange(S5_T + 1, dtype=F32)[:, None, None]
    pmag = jnp.exp(ks * (a_re * dt))
    pk_re = pmag * jnp.cos(ks * (a_im * dt))
    pk_im = pmag * jnp.sin(ks * (a_im * dt))
    bb = jnp.concatenate([bb_re, bb_im], axis=2)
    pr_re, pr_im = pk_re[S5_T - 1::-1], pk_im[S5_T - 1::-1]
    e_re = pr_re[:, :, None, :] * bb_re - pr_im[:, :, None, :] * bb_im
    e_im = pr_re[:, :, None, :] * bb_im + pr_im[:, :, None, :] * bb_re
    m_in = jnp.concatenate([e_re, e_im], axis=3)
    m_in = m_in.transpose(1, 0, 2, 3).reshape(S5_GROUPS, S5_TC, S5_RI).astype(BF16)
    hi = jax.lax.Precision.HIGHEST
    ct_re = c_re.astype(F32).transpose(0, 2, 1)
    ct_im = c_im.astype(F32).transpose(0, 2, 1)
    pt_re, pt_im = pk_re.transpose(1, 2, 0), pk_im.transpose(1, 2, 0)
    lane = jnp.arange(S5_TC)
    rep_c = (lane[None, :] % S5_GROUP == jnp.arange(S5_GROUP)[:, None]).astype(F32)
    rep_k = (lane[None, :] // S5_GROUP == jnp.arange(S5_T)[:, None]).astype(F32)
    cl_re, cl_im = jnp.dot(ct_re, rep_c, precision=hi), jnp.dot(ct_im, rep_c, precision=hi)

    def readout(k0):
        pl_re = jnp.dot(pt_re[:, :, k0:k0 + S5_T], rep_k, precision=hi)
        pl_im = jnp.dot(pt_im[:, :, k0:k0 + S5_T], rep_k, precision=hi)
        return jnp.concatenate([cl_re * pl_re - cl_im * pl_im, -(cl_re * pl_im + cl_im * pl_re)], axis=1)

    tap = readout(0)
    m_out = readout(1).astype(BF16)
    skip = d.astype(F32).reshape(S5_GROUPS, 1, S5_GROUP) * jnp.eye(S5_GROUP, dtype=F32)
    skip = jnp.pad(skip, ((0, 0), (0, 0), (0, S5_TC - S5_GROUP)))
    f_re, f_im = pk_re[S5_T], pk_im[S5_T]
    rows_a, rows_b = [], []
    for _ in range(S5_SCAN_STEPS):
        rows_a.append(jnp.concatenate([f_re, f_re], axis=1))
        rows_b.append(jnp.concatenate([-f_im, f_im], axis=1))
        f_re, f_im = f_re * f_re - f_im * f_im, 2.0 * f_re * f_im
    zero = jnp.zeros_like(rows_a[0])
    sc = jnp.stack(rows_a + [zero] + rows_b + [zero], axis=1)
    f_re, f_im = pk_re[S5_T], pk_im[S5_T]
    rows_a, rows_b = [], []
    for _ in range(SUBLANES):
        rows_a.append(jnp.concatenate([f_re, f_re], axis=1))
        rows_b.append(jnp.concatenate([-f_im, f_im], axis=1))
        f_re, f_im = (f_re * pk_re[S5_T] - f_im * pk_im[S5_T], f_re * pk_im[S5_T] + f_im * pk_re[S5_T])
    sf = jnp.stack(rows_a + rows_b, axis=1)
    return bb, tap, skip, m_in, m_out, sc, sf


def kernel(x, p, norm_mix, norm_ffn, norm_ple, norm_final, gm_w_in, gm_ln_g, gm_ln_b, gm_w_s, gm_b_s, gm_w_out, s5_w_in, s5_a_re, s5_a_im, s5_log_dt, s5_b_re, s5_b_im, s5_c_re, s5_c_im, s5_d, s5_w_out, ffn_w1, ffn_w3, ffn_w2, ple_w_gate, ple_w_proj):
    bsz, seq, d = x.shape
    depth = p.shape[0]
    m = bsz * seq
    xs = x.reshape(m, d)
    ps = p.reshape(depth, m, PLE_DIM)
    s5_in_b16 = s5_out_b16 = None
    for i in range(depth):
        j = i // 2
        if i % 2 == 0:
            gm_in_b16, gm_out_b16 = _cast_call([(gm_w_in, j), (gm_w_out, j)])
            z = _gm_in(xs, norm_mix[i], gm_in_b16, tm=512)
            xs, hn = _gm_mix(z, gm_ln_g, gm_ln_b, gm_w_s, gm_b_s, j, gm_out_b16, xs, norm_ffn[i], tm=512)
        else:
            tables = _s5_tables(s5_a_re[j], s5_a_im[j], s5_log_dt[j], s5_b_re[j], s5_b_im[j],
                                s5_c_re[j], s5_c_im[j], s5_d[j])
            u = _s5_in(xs, norm_mix[i], s5_in_b16, tm=1024)
            y = _s5_core(u, *tables, chunks_per_seq=seq // S5_T)
            xs, hn = _s5_out(y, s5_out_b16, xs, norm_ffn[i], tm=512)
        a, w2_b16, wg_b16, wp_b16 = _ffn_up(
            hn, ffn_w1, ffn_w3, i, [(ffn_w2, i), (ple_w_gate, i), (ple_w_proj, i)], tm=2048, tf=512)
        next_s5 = i + 1 < depth and (i + 1) % 2 == 1
        xs, *cast = _ffn_down(a, w2_b16, xs, [(s5_w_in, (i + 1) // 2)] if next_s5 else [], tm=256)
        if next_s5:
            s5_in_b16 = cast[0]
        xs, *cast = _ple(xs, norm_ple[i], ps, i, wg_b16, wp_b16,
                         norm_final if i == depth - 1 else None,
                         [(s5_w_out, (i + 1) // 2)] if next_s5 else [], tm=512)
        if next_s5:
            s5_out_b16 = cast[0]
    return xs.reshape(bsz, seq, d)
```

```python
import functools

import jax
import jax.numpy as jnp
from jax.experimental import pallas as pl
from jax.experimental.pallas import tpu as pltpu

F32 = jnp.float32
BF16 = jnp.bfloat16

D_MODEL = 2048
PLE_DIM = 256
EPS = 1e-6
LANES = 128
SUBLANES = 8
BF16_ROWS = 16

GM_CHUNK = 128
GM_HEAD_DIM = 128
GM_HEADS = D_MODEL // GM_HEAD_DIM

S5_GROUP = 16
S5_GROUPS = D_MODEL // S5_GROUP
S5_STATE = 64
S5_T = 16
S5_TC = S5_T * S5_GROUP
S5_RI = 2 * S5_STATE
S5_GB = LANES // S5_GROUP
S5_TILES = D_MODEL // LANES
S5_SCAN_STEPS = 7

W_CHUNKS = 16
SUB_ROWS = 256
VMEM_LIMIT = 56 * 1024 * 1024


def _params(*sem):
    return pltpu.CompilerParams(dimension_semantics=sem, vmem_limit_bytes=VMEM_LIMIT)


def _rms(xf, gain):
    ms = jnp.mean(xf * xf, axis=-1, keepdims=True)
    return xf * jax.lax.rsqrt(ms + EPS) * gain


def _dot(a, b):
    return jnp.dot(a, b, preferred_element_type=F32)


def _sub_blocks(tm):
    sub = min(tm, SUB_ROWS)
    return [slice(r, r + sub) for r in range(0, tm, sub)]


def _cast_rows(k, steps):
    rows = BF16_ROWS
    while k % rows or k // rows > steps:
        rows += BF16_ROWS
    return rows


def _cast_specs(casts, steps, step_of):
    ins, outs, shapes = [], [], []
    for arr, layer in casts:
        _, k, n = arr.shape
        rows = _cast_rows(k, steps)
        last = k // rows - 1
        ins.append(pl.BlockSpec(
            (None, rows, n), lambda *g, layer=layer, last=last: (layer, jnp.minimum(step_of(*g), last), 0)))
        outs.append(pl.BlockSpec((rows, n), lambda *g, last=last: (jnp.minimum(step_of(*g), last), 0)))
        shapes.append(jax.ShapeDtypeStruct((k, n), BF16))
    return ins, outs, shapes


def _run_casts(ci_refs, co_refs):
    for ci, co in zip(ci_refs, co_refs):
        co[...] = ci[...].astype(BF16)


def _cast_kernel(*refs):
    half = len(refs) // 2
    _run_casts(refs[:half], refs[half:])


def _cast_call(casts):
    cast_in, cast_out, cast_shape = _cast_specs(casts, W_CHUNKS, lambda s: s)
    return pl.pallas_call(
        _cast_kernel,
        grid=(W_CHUNKS,),
        in_specs=cast_in,
        out_specs=cast_out,
        out_shape=cast_shape,
        compiler_params=_params("arbitrary"),
        name="cast_weights",
    )(*[arr for arr, _ in casts])


def _resident_call(body, *, name, m, tm, weights, row_ins, consts, outs, casts=(), scratch=()):
    nw, nk, nr, nc, no = len(weights), len(casts), len(row_ins), len(consts), len(outs)
    steps = m // tm

    def kern(*refs):
        pos = 0

        def take(n):
            nonlocal pos
            pos += n
            return refs[pos - n:pos]

        w_refs, ci_refs, r_refs, c_refs = take(nw), take(nk), take(nr), take(nc)
        o_refs, co_refs, s_refs = take(no), take(nk), take(len(scratch))
        _run_casts(ci_refs, co_refs)
        body(w_refs, r_refs, c_refs, o_refs, s_refs)

    in_specs, args = [], []
    for arr in weights:
        in_specs.append(pl.BlockSpec(arr.shape, lambda s: (0, 0), pipeline_mode=pl.Buffered(1)))
        args.append(arr)
    cast_in, cast_out, cast_shape = _cast_specs(casts, steps, lambda s: s)
    in_specs += cast_in
    args += [arr for arr, _ in casts]
    for arr, block, index_fn in row_ins:
        in_specs.append(pl.BlockSpec(block, index_fn))
        args.append(arr)
    for arr in consts:
        in_specs.append(pl.BlockSpec((1, arr.shape[1]), lambda s: (0, 0)))
        args.append(arr)
    out_specs = [pl.BlockSpec(block, index_fn) for _, _, block, index_fn in outs]
    out_shape = [jax.ShapeDtypeStruct(shape, dtype) for shape, dtype, _, _ in outs]
    return pl.pallas_call(
        kern,
        grid=(steps,),
        in_specs=in_specs,
        out_specs=out_specs + cast_out,
        out_shape=out_shape + cast_shape,
        scratch_shapes=list(scratch),
        compiler_params=_params("arbitrary"),
        name=name,
    )(*args)


def _rows(tm, n):
    return (tm, n), (lambda i: (i, 0))


def _gm_in(x, gain, wb16, casts, *, tm):
    m, k = x.shape
    n = wb16.shape[1]

    def body(wb, r, c, o, scr):
        for rs in _sub_blocks(tm):
            h = _rms(r[0][rs, :], c[0][...]).astype(BF16)
            for cs in (slice(0, n // 2), slice(n // 2, n)):
                o[0][rs, cs] = jax.nn.gelu(_dot(h, wb[0][:, cs])).astype(BF16)

    blk_x, idx = _rows(tm, k)
    blk_o, _ = _rows(tm, n)
    return _resident_call(
        body, name="gm_in", m=m, tm=tm, weights=[wb16], casts=casts,
        row_ins=[(x, blk_x, idx)], consts=[gain.reshape(1, k)],
        outs=[((m, n), BF16, blk_o, idx)])


def _gm_mix(z, ln_g, ln_b, w_s, b_s, layer, wb16, res, gain, *, tm):
    m, n = res.shape
    bs = jnp.broadcast_to(b_s[layer][:, :, None], (GM_HEADS, GM_CHUNK, GM_HEAD_DIM)).astype(F32)

    def body(wb, r, c, o, scr):
        u_ref, v_ref, res_ref, ws_ref, bs_ref = r
        row = jax.lax.broadcasted_iota(jnp.int32, (GM_CHUNK, GM_CHUNK), 0)
        col = jax.lax.broadcasted_iota(jnp.int32, (GM_CHUNK, GM_CHUNK), 1)
        wm = [jnp.where(row >= col, ws_ref[h], 0.0).astype(BF16) for h in range(GM_HEADS)]
        for rs in _sub_blocks(tm):
            for c0 in range(rs.start, rs.stop, GM_CHUNK):
                rows = slice(c0, c0 + GM_CHUNK)
                v = v_ref[rows, :].astype(F32)
                mu = jnp.mean(v, axis=-1, keepdims=True)
                vc = v - mu
                var = jnp.mean(vc * vc, axis=-1, keepdims=True)
                vn = (vc * jax.lax.rsqrt(var + EPS) * c[1][...] + c[2][...]).astype(BF16)
                for h in range(GM_HEADS):
                    cols = slice(h * GM_HEAD_DIM, (h + 1) * GM_HEAD_DIM)
                    sv = _dot(wm[h], vn[:, cols]) + bs_ref[h]
                    scr[0][rows, cols] = (u_ref[rows, cols].astype(F32) * sv).astype(BF16)
            xn = res_ref[rs, :] + _dot(scr[0][rs, :], wb[0][...])
            o[0][rs, :] = xn
            o[1][rs, :] = _rms(xn, c[0][...]).astype(BF16)

    blk_x, idx = _rows(tm, n)
    return _resident_call(
        body, name="gm_mix", m=m, tm=tm, weights=[wb16],
        row_ins=[(z, blk_x, idx), (z, blk_x, lambda i: (i, 1)), (res, blk_x, idx),
                 (w_s, (None, GM_HEADS, GM_CHUNK, GM_CHUNK), lambda i: (layer, 0, 0, 0)),
                 (bs, (GM_HEADS, GM_CHUNK, GM_HEAD_DIM), lambda i: (0, 0, 0))],
        consts=[gain.reshape(1, n), ln_g[layer].reshape(1, n), ln_b[layer].reshape(1, n)],
        outs=[((m, n), F32, blk_x, idx), ((m, n), BF16, blk_x, idx)],
        scratch=[pltpu.VMEM((tm, n), BF16)])


def _ffn_down(a, w2b16, res, casts, *, tm):
    m, n = res.shape

    def body(wb, r, c, o, scr):
        for rs in _sub_blocks(tm):
            o[0][rs, :] = r[1][rs, :] + _dot(r[0][rs, :], wb[0][...])

    blk_a, idx = _rows(tm, a.shape[1])
    blk_x, _ = _rows(tm, n)
    return _resident_call(
        body, name="ffn_down", m=m, tm=tm, weights=[w2b16], casts=casts,
        row_ins=[(a, blk_a, idx), (res, blk_x, idx)], consts=[],
        outs=[((m, n), F32, blk_x, idx)])


def _ple(x, gain, p, layer, wgb16, wpb16, final_gain, casts, *, tm):
    m, n = x.shape

    def body(wb, r, c, o, scr):
        for rs in _sub_blocks(tm):
            xf = r[0][rs, :]
            h = _rms(xf, c[0][...]).astype(BF16)
            gate = jax.nn.sigmoid(_dot(h, wb[0][...]))
            proj = _dot(r[1][rs, :].astype(BF16), wb[1][...])
            xn = xf + gate * proj
            if final_gain is not None:
                xn = _rms(xn, c[1][...])
            o[0][rs, :] = xn

    blk_x, idx = _rows(tm, n)
    blk_p, _ = _rows(tm, PLE_DIM)
    consts = [gain.reshape(1, n)] + ([] if final_gain is None else [final_gain.reshape(1, n)])
    return _resident_call(
        body, name="ple", m=m, tm=tm, weights=[wgb16, wpb16], casts=casts,
        row_ins=[(x, blk_x, idx), (p, (None,) + blk_p, lambda i: (layer, i, 0))], consts=consts,
        outs=[((m, n), F32, blk_x, idx)])


def _s5_in(x, gain, wb16, *, tm):
    m, n = x.shape
    tc = tm // S5_T

    def body(wb, r, c, o, scr):
        for rs in _sub_blocks(tm):
            h = _rms(r[0][rs, :], c[0][...]).astype(BF16)
            u = _dot(h, wb[0][...])
            for l in range(S5_TILES):
                scr[0][l, rs, :] = u[:, l * LANES:(l + 1) * LANES]
            sub = rs.stop - rs.start
            cs = slice(rs.start // S5_T, rs.stop // S5_T)
            for t in range(S5_T):
                for l in range(S5_TILES):
                    o[0][l, cs, t * LANES:(t + 1) * LANES] = (
                        scr[0][l, pl.ds(rs.start + t, sub // S5_T, stride=S5_T), :].astype(BF16))

    blk_x, idx = _rows(tm, n)
    return _resident_call(
        body, name="s5_in", m=m, tm=tm, weights=[wb16],
        row_ins=[(x, blk_x, idx)], consts=[gain.reshape(1, n)],
        outs=[((S5_TILES, m // S5_T, S5_T * LANES), BF16, (S5_TILES, tc, S5_T * LANES), lambda i: (0, i, 0))],
        scratch=[pltpu.VMEM((S5_TILES, tm, LANES), F32)])[0]


def _s5_out(y_tiles, wb16, res, gain, *, tm):
    m, n = res.shape
    tc = tm // S5_T

    def body(wb, r, c, o, scr):
        for rs in _sub_blocks(tm):
            sub = rs.stop - rs.start
            cs = slice(rs.start // S5_T, rs.stop // S5_T)
            for t in range(S5_T):
                for l in range(S5_TILES):
                    scr[0][l, pl.ds(rs.start + t, sub // S5_T, stride=S5_T), :] = (
                        r[0][l, cs, t * LANES:(t + 1) * LANES].astype(F32))
            y = jnp.concatenate([scr[0][l, rs, :] for l in range(S5_TILES)], axis=1).astype(BF16)
            vg = _dot(y, wb[0][...])
            xn = r[1][rs, :] + vg[:, :n] * jax.nn.sigmoid(vg[:, n:])
            o[0][rs, :] = xn
            o[1][rs, :] = _rms(xn, c[0][...]).astype(BF16)

    blk_x, idx = _rows(tm, n)
    return _resident_call(
        body, name="s5_out", m=m, tm=tm, weights=[wb16],
        row_ins=[(y_tiles, (S5_TILES, tc, S5_T * LANES), lambda i: (0, i, 0)), (res, blk_x, idx)],
        consts=[gain.reshape(1, n)],
        outs=[((m, n), F32, blk_x, idx), ((m, n), BF16, blk_x, idx)],
        scratch=[pltpu.VMEM((S5_TILES, tm, LANES), F32)])


def _ffn_up_kernel(*refs, tm, n_casts):
    h_ref, w1_ref, w3_ref = refs[:3]
    ci_refs = refs[3:3 + n_casts]
    o_ref = refs[3 + n_casts]
    co_refs = refs[4 + n_casts:4 + 2 * n_casts]
    w1_scr, w3_scr = refs[4 + 2 * n_casts:]
    _run_casts(ci_refs, co_refs)

    @pl.when(pl.program_id(1) == 0)
    def _():
        w1_scr[...] = w1_ref[...].astype(BF16)
        w3_scr[...] = w3_ref[...].astype(BF16)

    for rs in _sub_blocks(tm):
        h = h_ref[rs, :]
        o_ref[rs, :] = (jax.nn.silu(_dot(h, w1_scr[...])) * _dot(h, w3_scr[...])).astype(o_ref.dtype)


def _ffn_up(h, w1, w3, layer, casts, *, tm, tf):
    m, d = h.shape
    ff = w1.shape[2]
    ni = m // tm
    cast_in, cast_out, cast_shape = _cast_specs(casts, (ff // tf) * ni, lambda f, i: f * ni + i)
    return pl.pallas_call(
        functools.partial(_ffn_up_kernel, tm=tm, n_casts=len(casts)),
        grid=(ff // tf, ni),
        in_specs=[
            pl.BlockSpec((tm, d), lambda f, i: (i, 0)),
            pl.BlockSpec((None, d, tf), lambda f, i: (layer, 0, f)),
            pl.BlockSpec((None, d, tf), lambda f, i: (layer, 0, f)),
        ] + cast_in,
        out_specs=[pl.BlockSpec((tm, tf), lambda f, i: (i, f))] + cast_out,
        out_shape=[jax.ShapeDtypeStruct((m, ff), BF16)] + cast_shape,
        scratch_shapes=[pltpu.VMEM((d, tf), BF16), pltpu.VMEM((d, tf), BF16)],
        compiler_params=_params("arbitrary", "arbitrary"),
        name="ffn_up",
    )(h, w1, w3, *[arr for arr, _ in casts])


def _piece_transpose(v):
    piece = jax.lax.broadcasted_iota(jnp.int32, v[0].shape, 1) // S5_GROUP
    for delta in (4, 2, 1):
        keep = (piece & delta) == 0
        shift = delta * S5_GROUP
        new = list(v)
        for i in range(S5_GB):
            if i & delta == 0:
                a, b = v[i], v[i + delta]
                new[i] = jnp.where(keep, a, pltpu.roll(b, shift, axis=1))
                new[i + delta] = jnp.where(keep, pltpu.roll(a, LANES - shift, axis=1), b)
        v = new
    return v


def _toeplitz_rows(strip, s):
    lo, hi = strip[:, :LANES], strip[:, LANES:]
    lane = jax.lax.broadcasted_iota(jnp.int32, lo.shape, 1)
    shift = (s * S5_GROUP) % LANES
    if s == 0:
        return strip
    if s * S5_GROUP < LANES:
        lo_r, hi_r = pltpu.roll(lo, shift, axis=1), pltpu.roll(hi, shift, axis=1)
        return jnp.concatenate([jnp.where(lane >= shift, lo_r, 0.0),
                                jnp.where(lane >= shift, hi_r, lo_r)], axis=1)
    lo_r = lo if shift == 0 else pltpu.roll(lo, shift, axis=1)
    return jnp.concatenate([jnp.zeros_like(lo), jnp.where(lane >= shift, lo_r, 0.0)], axis=1)


def _cmul_add(s, fa, fb, x):
    return s + fa * x + fb * pltpu.roll(x, S5_STATE, axis=1)


def _chunk_scan(ss, sc_ref, sf_ref, s_scr, c_scr, *, chunks_per_seq):
    groups = range(len(ss))
    rows = ss[0].shape[0]
    tiles = rows // SUBLANES
    r_idx = jax.lax.broadcasted_iota(jnp.int32, ss[0].shape, 0) % SUBLANES
    for k in range(3):
        d = 1 << k
        ss = [_cmul_add(ss[g], sc_ref[g, k:k + 1, :], sc_ref[g, 8 + k:9 + k, :],
                        jnp.where(r_idx >= d, pltpu.roll(ss[g], d, axis=0), 0.0)) for g in groups]
    for g in groups:
        s_scr[g] = ss[g]
    ts = [s_scr[g, pl.ds(SUBLANES - 1, tiles, stride=SUBLANES), :] for g in groups]
    j_idx = jax.lax.broadcasted_iota(jnp.int32, ts[0].shape, 0) % (chunks_per_seq // SUBLANES)
    for k in range(3, S5_SCAN_STEPS):
        d = 1 << (k - 3)
        ts = [_cmul_add(ts[g], sc_ref[g, k:k + 1, :], sc_ref[g, 8 + k:9 + k, :],
                        jnp.where(j_idx >= d, pltpu.roll(ts[g], d, axis=0), 0.0)) for g in groups]
    for g in groups:
        carry = jnp.where(j_idx >= 1, pltpu.roll(ts[g], 1, axis=0), 0.0)
        c_scr[g, 0] = carry
        c_scr[g, 1] = pltpu.roll(carry, S5_STATE, axis=1)
    outs = []
    for g in groups:
        fa, fb = sf_ref[g, :SUBLANES, :], sf_ref[g, SUBLANES:, :]
        out = []
        for j in range(tiles):
            cb = jnp.broadcast_to(c_scr[g, 0, j:j + 1, :], (SUBLANES, S5_RI))
            cs = jnp.broadcast_to(c_scr[g, 1, j:j + 1, :], (SUBLANES, S5_RI))
            out.append(ss[g][j * SUBLANES:(j + 1) * SUBLANES] + fa * cb + fb * cs)
        outs.append(jnp.concatenate(out, axis=0))
    return outs


def _s5_kernel(x_ref, bb_ref, tap_ref, skip_ref, min_ref, mout_ref, sc_ref, sf_ref, o_ref,
               mi_scr, s_scr, c_scr, *, chunks_per_seq):
    rows = x_ref.shape[1]
    n_idx = jax.lax.broadcasted_iota(jnp.int32, (rows, S5_RI), 0) % chunks_per_seq
    halves = S5_T // S5_GB
    for gi in range(S5_GB):
        strip = jnp.dot(bb_ref[gi], tap_ref[gi], preferred_element_type=F32,
                        precision=jax.lax.Precision.HIGHEST) + skip_ref[gi]
        for s in range(S5_T):
            mi_scr[gi, s * S5_GROUP:(s + 1) * S5_GROUP, :] = _toeplitz_rows(strip, s).astype(BF16)
    xin = [_piece_transpose([x_ref[0, :, (S5_GB * h + i) * LANES:(S5_GB * h + i + 1) * LANES]
                             for i in range(S5_GB)]) for h in range(halves)]
    groups = range(S5_GB)
    xg = [jnp.concatenate([xin[h][g] for h in range(halves)], axis=1) for g in groups]
    ys = [_dot(xg[g], mi_scr[g]) for g in groups]
    ss = [_dot(xg[g], min_ref[g]) for g in groups]
    ss = _chunk_scan(ss, sc_ref, sf_ref, s_scr, c_scr, chunks_per_seq=chunks_per_seq)
    for g in groups:
        s_prev = jnp.where(n_idx >= 1, pltpu.roll(ss[g], 1, axis=0), 0.0)
        ys[g] = jax.nn.gelu(ys[g] + _dot(s_prev.astype(BF16), mout_ref[g])).astype(o_ref.dtype)
    for h in range(halves):
        out = _piece_transpose([ys[gi][:, h * LANES:(h + 1) * LANES] for gi in range(S5_GB)])
        for i in range(S5_GB):
            t = S5_GB * h + i
            o_ref[0, :, t * LANES:(t + 1) * LANES] = out[i]


def _s5_core(x_tiles, bb, tap, skip, m_in, m_out, sc, sf, *, chunks_per_seq):
    tiles, rows, width = x_tiles.shape
    return pl.pallas_call(
        functools.partial(_s5_kernel, chunks_per_seq=chunks_per_seq),
        grid=(tiles,),
        in_specs=[
            pl.BlockSpec((1, rows, width), lambda i: (i, 0, 0)),
            pl.BlockSpec((S5_GB, S5_GROUP, S5_RI), lambda i: (i, 0, 0)),
            pl.BlockSpec((S5_GB, S5_RI, S5_TC), lambda i: (i, 0, 0)),
            pl.BlockSpec((S5_GB, S5_GROUP, S5_TC), lambda i: (i, 0, 0)),
            pl.BlockSpec((S5_GB, S5_TC, S5_RI), lambda i: (i, 0, 0)),
            pl.BlockSpec((S5_GB, S5_RI, S5_TC), lambda i: (i, 0, 0)),
            pl.BlockSpec((S5_GB, 2 * SUBLANES, S5_RI), lambda i: (i, 0, 0)),
            pl.BlockSpec((S5_GB, 2 * SUBLANES, S5_RI), lambda i: (i, 0, 0)),
        ],
        out_specs=pl.BlockSpec((1, rows, width), lambda i: (i, 0, 0)),
        out_shape=jax.ShapeDtypeStruct((tiles, rows, width), BF16),
        scratch_shapes=[pltpu.VMEM((S5_GB, S5_TC, S5_TC), BF16),
                        pltpu.VMEM((S5_GB, rows, S5_RI), F32),
                        pltpu.VMEM((S5_GB, 2, rows // SUBLANES, S5_RI), F32)],
        compiler_params=_params("parallel"),
        name="s5_core",
    )(x_tiles, bb, tap, skip, m_in, m_out, sc, sf)


def _s5_tables(a_re, a_im, log_dt, b_re, b_im, c_re, c_im, d):
    a_re, a_im = a_re.astype(F32), a_im.astype(F32)
    dt = jnp.exp(log_dt.astype(F32))[:, None]
    mag = jnp.exp(a_re * dt)
    lb_re, lb_im = mag * jnp.cos(a_im * dt), mag * jnp.sin(a_im * dt)
    den = a_re * a_re + a_im * a_im
    q_re = ((lb_re - 1.0) * a_re + lb_im * a_im) / den
    q_im = (lb_im * a_re - (lb_re - 1.0) * a_im) / den
    b_re = b_re.astype(F32).transpose(0, 2, 1)
    b_im = b_im.astype(F32).transpose(0, 2, 1)
    bb_re = q_re[:, None, :] * b_re - q_im[:, None, :] * b_im
    bb_im = q_re[:, None, :] * b_im + q_im[:, None, :] * b_re
    ks = jnp.arange(S5_T + 1, dtype=F32)[:, None, None]
    pmag = jnp.exp(ks * (a_re * dt))
    pk_re = pmag * jnp.cos(ks * (a_im * dt))
    pk_im = pmag * jnp.sin(ks * (a_im * dt))
    bb = jnp.concatenate([bb_re, bb_im], axis=2)
    pr_re, pr_im = pk_re[S5_T - 1::-1], pk_im[S5_T - 1::-1]
    pr_a = jnp.concatenate([pr_re, pr_re], axis=2)[:, :, None, :]
    pr_b = jnp.concatenate([-pr_im, pr_im], axis=2)[:, :, None, :]
    m_in = pr_a * bb + pr_b * jnp.concatenate([bb_im, bb_re], axis=2)
    m_in = m_in.transpose(1, 0, 2, 3).reshape(S5_GROUPS, S5_TC, S5_RI).astype(BF16)
    hi = jax.lax.Precision.HIGHEST
    ct_re = c_re.astype(F32).transpose(0, 2, 1)
    ct_im = c_im.astype(F32).transpose(0, 2, 1)
    pt_re, pt_im = pk_re.transpose(1, 2, 0), pk_im.transpose(1, 2, 0)
    lane = jnp.arange((S5_T + 1) * S5_GROUP)
    rep_c = (lane[None, :] % S5_GROUP == jnp.arange(S5_GROUP)[:, None]).astype(F32)
    rep_k = (lane[None, :] // S5_GROUP == jnp.arange(S5_T + 1)[:, None]).astype(F32)
    x12 = jnp.dot(jnp.concatenate([ct_re, -ct_re, -ct_im, -ct_im], axis=1), rep_c, precision=hi)
    y12 = jnp.dot(jnp.concatenate([pt_re, pt_im, pt_im, pt_re], axis=1), rep_k, precision=hi)
    w = x12[:, :S5_RI] * y12[:, :S5_RI] + x12[:, S5_RI:] * y12[:, S5_RI:]
    tap = w[:, :, :S5_TC]
    m_out = w[:, :, S5_GROUP:].astype(BF16)
    skip = d.astype(F32).reshape(S5_GROUPS, 1, S5_GROUP) * jnp.eye(S5_GROUP, dtype=F32)
    skip = jnp.pad(skip, ((0, 0), (0, 0), (0, S5_TC - S5_GROUP)))
    f_re, f_im = pk_re[S5_T], pk_im[S5_T]
    rows_a, rows_b = [], []
    for _ in range(S5_SCAN_STEPS):
        rows_a.append(jnp.concatenate([f_re, f_re], axis=1))
        rows_b.append(jnp.concatenate([-f_im, f_im], axis=1))
        f_re, f_im = f_re * f_re - f_im * f_im, 2.0 * f_re * f_im
    zero = jnp.zeros_like(rows_a[0])
    sc = jnp.stack(rows_a + [zero] + rows_b + [zero], axis=1)
    f_re, f_im = pk_re[S5_T], pk_im[S5_T]
    rows_a, rows_b = [], []
    for _ in range(SUBLANES):
        rows_a.append(jnp.concatenate([f_re, f_re], axis=1))
        rows_b.append(jnp.concatenate([-f_im, f_im], axis=1))
        f_re, f_im = (f_re * pk_re[S5_T] - f_im * pk_im[S5_T], f_re * pk_im[S5_T] + f_im * pk_re[S5_T])
    sf = jnp.stack(rows_a + rows_b, axis=1)
    return bb, tap, skip, m_in, m_out, sc, sf


def kernel(x, p, norm_mix, norm_ffn, norm_ple, norm_final, gm_w_in, gm_ln_g, gm_ln_b, gm_w_s, gm_b_s, gm_w_out, s5_w_in, s5_a_re, s5_a_im, s5_log_dt, s5_b_re, s5_b_im, s5_c_re, s5_c_im, s5_d, s5_w_out, ffn_w1, ffn_w3, ffn_w2, ple_w_gate, ple_w_proj):
    bsz, seq, d = x.shape
    depth = p.shape[0]
    m = bsz * seq
    xs = x.reshape(m, d)
    ps = p.reshape(depth, m, PLE_DIM)
    s5_in_b16 = s5_out_b16 = None
    for i in range(depth):
        j = i // 2
        if i % 2 == 0:
            gm_in_b16, = _cast_call([(gm_w_in, j)])
            z, gm_out_b16 = _gm_in(xs, norm_mix[i], gm_in_b16, [(gm_w_out, j)], tm=512)
            xs, hn = _gm_mix(z, gm_ln_g, gm_ln_b, gm_w_s, gm_b_s, j, gm_out_b16, xs, norm_ffn[i], tm=512)
        else:
            tables = _s5_tables(s5_a_re[j], s5_a_im[j], s5_log_dt[j], s5_b_re[j], s5_b_im[j],
                                s5_c_re[j], s5_c_im[j], s5_d[j])
            u = _s5_in(xs, norm_mix[i], s5_in_b16, tm=1024)
            y = _s5_core(u, *tables, chunks_per_seq=seq // S5_T)
            xs, hn = _s5_out(y, s5_out_b16, xs, norm_ffn[i], tm=512)
        a, w2_b16, wg_b16, wp_b16 = _ffn_up(
            hn, ffn_w1, ffn_w3, i, [(ffn_w2, i), (ple_w_gate, i), (ple_w_proj, i)], tm=2048, tf=512)
        next_s5 = i + 1 < depth and (i + 1) % 2 == 1
        xs, *cast = _ffn_down(a, w2_b16, xs, [(s5_w_in, (i + 1) // 2)] if next_s5 else [], tm=256)
        if next_s5:
            s5_in_b16 = cast[0]
        xs, *cast = _ple(xs, norm_ple[i], ps, i, wg_b16, wp_b16,
                         norm_final if i == depth - 1 else None,
                         [(s5_w_out, (i + 1) // 2)] if next_s5 else [], tm=512)
        if next_s5:
            s5_out_b16 = cast[0]
    return xs.reshape(bsz, seq, d)
```

```python
import functools

import jax
import jax.numpy as jnp
from jax.experimental import pallas as pl
from jax.experimental.pallas import tpu as pltpu

F32 = jnp.float32
BF16 = jnp.bfloat16

D_MODEL = 2048
PLE_DIM = 256
EPS = 1e-6
LANES = 128
SUBLANES = 8
BF16_ROWS = 16

GM_CHUNK = 128
GM_HEAD_DIM = 128
GM_HEADS = D_MODEL // GM_HEAD_DIM

S5_GROUP = 16
S5_GROUPS = D_MODEL // S5_GROUP
S5_STATE = 64
S5_T = 16
S5_TC = S5_T * S5_GROUP
S5_RI = 2 * S5_STATE
S5_GB = LANES // S5_GROUP
S5_TILES = D_MODEL // LANES
S5_SCAN_STEPS = 7

W_CHUNKS = 16
SUB_ROWS = 256
VMEM_LIMIT = 56 * 1024 * 1024


def _params(*sem):
    return pltpu.CompilerParams(dimension_semantics=sem, vmem_limit_bytes=VMEM_LIMIT)


def _rms(xf, gain):
    ms = jnp.mean(xf * xf, axis=-1, keepdims=True)
    return xf * jax.lax.rsqrt(ms + EPS) * gain


def _dot(a, b):
    return jnp.dot(a, b, preferred_element_type=F32)


def _sub_blocks(tm):
    sub = min(tm, SUB_ROWS)
    return [slice(r, r + sub) for r in range(0, tm, sub)]


def _cast_rows(k, steps):
    rows = BF16_ROWS
    while k % rows or k // rows > steps:
        rows += BF16_ROWS
    return rows


def _cast_specs(casts, steps, step_of):
    ins, outs, shapes = [], [], []
    for arr, layer in casts:
        _, k, n = arr.shape
        rows = _cast_rows(k, steps)
        last = k // rows - 1
        ins.append(pl.BlockSpec(
            (None, rows, n), lambda *g, layer=layer, last=last: (layer, jnp.minimum(step_of(*g), last), 0)))
        outs.append(pl.BlockSpec((rows, n), lambda *g, last=last: (jnp.minimum(step_of(*g), last), 0)))
        shapes.append(jax.ShapeDtypeStruct((k, n), BF16))
    return ins, outs, shapes


def _run_casts(ci_refs, co_refs):
    for ci, co in zip(ci_refs, co_refs):
        co[...] = ci[...].astype(BF16)


def _cast_kernel(*refs):
    half = len(refs) // 2
    _run_casts(refs[:half], refs[half:])


def _cast_call(casts):
    cast_in, cast_out, cast_shape = _cast_specs(casts, W_CHUNKS, lambda s: s)
    return pl.pallas_call(
        _cast_kernel,
        grid=(W_CHUNKS,),
        in_specs=cast_in,
        out_specs=cast_out,
        out_shape=cast_shape,
        compiler_params=_params("arbitrary"),
        name="cast_weights",
    )(*[arr for arr, _ in casts])


def _resident_call(body, *, name, m, tm, weights, row_ins, consts, outs, casts=(), scratch=()):
    nw, nk, nr, nc, no = len(weights), len(casts), len(row_ins), len(consts), len(outs)
    steps = m // tm

    def kern(*refs):
        pos = 0

        def take(n):
            nonlocal pos
            pos += n
            return refs[pos - n:pos]

        w_refs, ci_refs, r_refs, c_refs = take(nw), take(nk), take(nr), take(nc)
        o_refs, co_refs, s_refs = take(no), take(nk), take(len(scratch))
        _run_casts(ci_refs, co_refs)
        body(w_refs, r_refs, c_refs, o_refs, s_refs)

    in_specs, args = [], []
    for arr in weights:
        in_specs.append(pl.BlockSpec(arr.shape, lambda s: (0, 0), pipeline_mode=pl.Buffered(1)))
        args.append(arr)
    cast_in, cast_out, cast_shape = _cast_specs(casts, steps, lambda s: s)
    in_specs += cast_in
    args += [arr for arr, _ in casts]
    for arr, block, index_fn in row_ins:
        in_specs.append(pl.BlockSpec(block, index_fn))
        args.append(arr)
    for arr in consts:
        in_specs.append(pl.BlockSpec((1, arr.shape[1]), lambda s: (0, 0)))
        args.append(arr)
    out_specs = [pl.BlockSpec(block, index_fn) for _, _, block, index_fn in outs]
    out_shape = [jax.ShapeDtypeStruct(shape, dtype) for shape, dtype, _, _ in outs]
    return pl.pallas_call(
        kern,
        grid=(steps,),
        in_specs=in_specs,
        out_specs=out_specs + cast_out,
        out_shape=out_shape + cast_shape,
        scratch_shapes=list(scratch),
        compiler_params=_params("arbitrary"),
        name=name,
    )(*args)


def _rows(tm, n):
    return (tm, n), (lambda i: (i, 0))


def _gm_in(x, gain, wb16, casts, *, tm):
    m, k = x.shape
    n = wb16.shape[1]

    def body(wb, r, c, o, scr):
        for rs in _sub_blocks(tm):
            h = _rms(r[0][rs, :], c[0][...]).astype(BF16)
            o[0][rs, :] = jax.nn.gelu(_dot(h, wb[0][...])).astype(BF16)

    blk_x, idx = _rows(tm, k)
    blk_o, _ = _rows(tm, n)
    return _resident_call(
        body, name="gm_in", m=m, tm=tm, weights=[wb16], casts=casts,
        row_ins=[(x, blk_x, idx)], consts=[gain.reshape(1, k)],
        outs=[((m, n), BF16, blk_o, idx)])


def _gm_mix(z, ln_g, ln_b, w_s, b_s, layer, wb16, res, gain, *, tm):
    m, n = res.shape
    bs = jnp.broadcast_to(b_s[layer][:, :, None], (GM_HEADS, GM_CHUNK, GM_HEAD_DIM)).astype(F32)

    def body(wb, r, c, o, scr):
        u_ref, v_ref, res_ref, ws_ref, bs_ref = r
        row = jax.lax.broadcasted_iota(jnp.int32, (GM_CHUNK, GM_CHUNK), 0)
        col = jax.lax.broadcasted_iota(jnp.int32, (GM_CHUNK, GM_CHUNK), 1)
        wm = [jnp.where(row >= col, ws_ref[h], 0.0).astype(BF16) for h in range(GM_HEADS)]
        for rs in _sub_blocks(tm):
            for c0 in range(rs.start, rs.stop, GM_CHUNK):
                rows = slice(c0, c0 + GM_CHUNK)
                v = v_ref[rows, :].astype(F32)
                mu = jnp.mean(v, axis=-1, keepdims=True)
                vc = v - mu
                var = jnp.mean(vc * vc, axis=-1, keepdims=True)
                vn = (vc * jax.lax.rsqrt(var + EPS) * c[1][...] + c[2][...]).astype(BF16)
                for h in range(GM_HEADS):
                    cols = slice(h * GM_HEAD_DIM, (h + 1) * GM_HEAD_DIM)
                    sv = _dot(wm[h], vn[:, cols]) + bs_ref[h]
                    scr[0][rows, cols] = (u_ref[rows, cols].astype(F32) * sv).astype(BF16)
            xn = res_ref[rs, :] + _dot(scr[0][rs, :], wb[0][...])
            o[0][rs, :] = xn
            o[1][rs, :] = _rms(xn, c[0][...]).astype(BF16)

    blk_x, idx = _rows(tm, n)
    return _resident_call(
        body, name="gm_mix", m=m, tm=tm, weights=[wb16],
        row_ins=[(z, blk_x, idx), (z, blk_x, lambda i: (i, 1)), (res, blk_x, idx),
                 (w_s, (None, GM_HEADS, GM_CHUNK, GM_CHUNK), lambda i: (layer, 0, 0, 0)),
                 (bs, (GM_HEADS, GM_CHUNK, GM_HEAD_DIM), lambda i: (0, 0, 0))],
        consts=[gain.reshape(1, n), ln_g[layer].reshape(1, n), ln_b[layer].reshape(1, n)],
        outs=[((m, n), F32, blk_x, idx), ((m, n), BF16, blk_x, idx)],
        scratch=[pltpu.VMEM((tm, n), BF16)])


def _ffn_down(a, w2b16, res, casts, *, tm):
    m, n = res.shape

    def body(wb, r, c, o, scr):
        for rs in _sub_blocks(tm):
            o[0][rs, :] = r[1][rs, :] + _dot(r[0][rs, :], wb[0][...])

    blk_a, idx = _rows(tm, a.shape[1])
    blk_x, _ = _rows(tm, n)
    return _resident_call(
        body, name="ffn_down", m=m, tm=tm, weights=[w2b16], casts=casts,
        row_ins=[(a, blk_a, idx), (res, blk_x, idx)], consts=[],
        outs=[((m, n), F32, blk_x, idx)])


def _ple(x, gain, p, layer, wgb16, wpb16, final_gain, casts, *, tm):
    m, n = x.shape

    def body(wb, r, c, o, scr):
        for rs in _sub_blocks(tm):
            xf = r[0][rs, :]
            h = _rms(xf, c[0][...]).astype(BF16)
            gate = jax.nn.sigmoid(_dot(h, wb[0][...]))
            proj = _dot(r[1][rs, :].astype(BF16), wb[1][...])
            xn = xf + gate * proj
            if final_gain is not None:
                xn = _rms(xn, c[1][...])
            o[0][rs, :] = xn

    blk_x, idx = _rows(tm, n)
    blk_p, _ = _rows(tm, PLE_DIM)
    consts = [gain.reshape(1, n)] + ([] if final_gain is None else [final_gain.reshape(1, n)])
    return _resident_call(
        body, name="ple", m=m, tm=tm, weights=[wgb16, wpb16], casts=casts,
        row_ins=[(x, blk_x, idx), (p, (None,) + blk_p, lambda i: (layer, i, 0))], consts=consts,
        outs=[((m, n), F32, blk_x, idx)])


def _s5_in(x, gain, wb16, *, tm):
    m, n = x.shape
    tc = tm // S5_T

    def body(wb, r, c, o, scr):
        for rs in _sub_blocks(tm):
            h = _rms(r[0][rs, :], c[0][...]).astype(BF16)
            u = _dot(h, wb[0][...])
            for l in range(S5_TILES):
                scr[0][l, rs, :] = u[:, l * LANES:(l + 1) * LANES]
            sub = rs.stop - rs.start
            cs = slice(rs.start // S5_T, rs.stop // S5_T)
            for t in range(S5_T):
                for l in range(S5_TILES):
                    o[0][l, cs, t * LANES:(t + 1) * LANES] = (
                        scr[0][l, pl.ds(rs.start + t, sub // S5_T, stride=S5_T), :].astype(BF16))

    blk_x, idx = _rows(tm, n)
    return _resident_call(
        body, name="s5_in", m=m, tm=tm, weights=[wb16],
        row_ins=[(x, blk_x, idx)], consts=[gain.reshape(1, n)],
        outs=[((S5_TILES, m // S5_T, S5_T * LANES), BF16, (S5_TILES, tc, S5_T * LANES), lambda i: (0, i, 0))],
        scratch=[pltpu.VMEM((S5_TILES, tm, LANES), F32)])[0]


def _s5_out(y_tiles, wb16, res, gain, *, tm):
    m, n = res.shape
    tc = tm // S5_T

    def body(wb, r, c, o, scr):
        for rs in _sub_blocks(tm):
            sub = rs.stop - rs.start
            cs = slice(rs.start // S5_T, rs.stop // S5_T)
            for t in range(S5_T):
                for l in range(S5_TILES):
                    scr[0][l, pl.ds(rs.start + t, sub // S5_T, stride=S5_T), :] = (
                        r[0][l, cs, t * LANES:(t + 1) * LANES].astype(F32))
            y = jnp.concatenate([scr[0][l, rs, :] for l in range(S5_TILES)], axis=1).astype(BF16)
            vg = _dot(y, wb[0][...])
            xn = r[1][rs, :] + vg[:, :n] * jax.nn.sigmoid(vg[:, n:])
            o[0][rs, :] = xn
            o[1][rs, :] = _rms(xn, c[0][...]).astype(BF16)

    blk_x, idx = _rows(tm, n)
    return _resident_call(
        body, name="s5_out", m=m, tm=tm, weights=[wb16],
        row_ins=[(y_tiles, (S5_TILES, tc, S5_T * LANES), lambda i: (0, i, 0)), (res, blk_x, idx)],
        consts=[gain.reshape(1, n)],
        outs=[((m, n), F32, blk_x, idx), ((m, n), BF16, blk_x, idx)],
        scratch=[pltpu.VMEM((S5_TILES, tm, LANES), F32)])


def _ffn_up_kernel(*refs, tm, n_casts):
    h_ref, w1_ref, w3_ref = refs[:3]
    ci_refs = refs[3:3 + n_casts]
    o_ref = refs[3 + n_casts]
    co_refs = refs[4 + n_casts:4 + 2 * n_casts]
    w1_scr, w3_scr = refs[4 + 2 * n_casts:]
    _run_casts(ci_refs, co_refs)

    @pl.when(pl.program_id(1) == 0)
    def _():
        w1_scr[...] = w1_ref[...].astype(BF16)
        w3_scr[...] = w3_ref[...].astype(BF16)

    for rs in _sub_blocks(tm):
        h = h_ref[rs, :]
        o_ref[rs, :] = (jax.nn.silu(_dot(h, w1_scr[...])) * _dot(h, w3_scr[...])).astype(o_ref.dtype)


def _ffn_up(h, w1, w3, layer, casts, *, tm, tf):
    m, d = h.shape
    ff = w1.shape[2]
    ni = m // tm
    cast_in, cast_out, cast_shape = _cast_specs(casts, (ff // tf) * ni, lambda f, i: f * ni + i)
    return pl.pallas_call(
        functools.partial(_ffn_up_kernel, tm=tm, n_casts=len(casts)),
        grid=(ff // tf, ni),
        in_specs=[
            pl.BlockSpec((tm, d), lambda f, i: (i, 0)),
            pl.BlockSpec((None, d, tf), lambda f, i: (layer, 0, f)),
            pl.BlockSpec((None, d, tf), lambda f, i: (layer, 0, f)),
        ] + cast_in,
        out_specs=[pl.BlockSpec((tm, tf), lambda f, i: (i, f))] + cast_out,
        out_shape=[jax.ShapeDtypeStruct((m, ff), BF16)] + cast_shape,
        scratch_shapes=[pltpu.VMEM((d, tf), BF16), pltpu.VMEM((d, tf), BF16)],
        compiler_params=_params("arbitrary", "arbitrary"),
        name="ffn_up",
    )(h, w1, w3, *[arr for arr, _ in casts])


def _piece_transpose(v):
    piece = jax.lax.broadcasted_iota(jnp.int32, v[0].shape, 1) // S5_GROUP
    for delta in (4, 2, 1):
        keep = (piece & delta) == 0
        shift = delta * S5_GROUP
        new = list(v)
        for i in range(S5_GB):
            if i & delta == 0:
                a, b = v[i], v[i + delta]
                new[i] = jnp.where(keep, a, pltpu.roll(b, shift, axis=1))
                new[i + delta] = jnp.where(keep, pltpu.roll(a, LANES - shift, axis=1), b)
        v = new
    return v


def _toeplitz_rows(strip, s):
    lo, hi = strip[:, :LANES], strip[:, LANES:]
    lane = jax.lax.broadcasted_iota(jnp.int32, lo.shape, 1)
    shift = (s * S5_GROUP) % LANES
    if s == 0:
        return strip
    if s * S5_GROUP < LANES:
        lo_r, hi_r = pltpu.roll(lo, shift, axis=1), pltpu.roll(hi, shift, axis=1)
        return jnp.concatenate([jnp.where(lane >= shift, lo_r, 0.0),
                                jnp.where(lane >= shift, hi_r, lo_r)], axis=1)
    lo_r = lo if shift == 0 else pltpu.roll(lo, shift, axis=1)
    return jnp.concatenate([jnp.zeros_like(lo), jnp.where(lane >= shift, lo_r, 0.0)], axis=1)


def _cmul_add(s, fa, fb, x):
    return s + fa * x + fb * pltpu.roll(x, S5_STATE, axis=1)


def _chunk_scan(ss, sc_ref, sf_ref, s_scr, c_scr, *, chunks_per_seq):
    groups = range(len(ss))
    rows = ss[0].shape[0]
    tiles = rows // SUBLANES
    r_idx = jax.lax.broadcasted_iota(jnp.int32, ss[0].shape, 0) % SUBLANES
    for k in range(3):
        d = 1 << k
        ss = [_cmul_add(ss[g], sc_ref[g, k:k + 1, :], sc_ref[g, 8 + k:9 + k, :],
                        jnp.where(r_idx >= d, pltpu.roll(ss[g], d, axis=0), 0.0)) for g in groups]
    for g in groups:
        s_scr[g] = ss[g]
    ts = [s_scr[g, pl.ds(SUBLANES - 1, tiles, stride=SUBLANES), :] for g in groups]
    j_idx = jax.lax.broadcasted_iota(jnp.int32, ts[0].shape, 0) % (chunks_per_seq // SUBLANES)
    for k in range(3, S5_SCAN_STEPS):
        d = 1 << (k - 3)
        ts = [_cmul_add(ts[g], sc_ref[g, k:k + 1, :], sc_ref[g, 8 + k:9 + k, :],
                        jnp.where(j_idx >= d, pltpu.roll(ts[g], d, axis=0), 0.0)) for g in groups]
    for g in groups:
        carry = jnp.where(j_idx >= 1, pltpu.roll(ts[g], 1, axis=0), 0.0)
        c_scr[g, 0] = carry
        c_scr[g, 1] = pltpu.roll(carry, S5_STATE, axis=1)
    outs = []
    for g in groups:
        fa, fb = sf_ref[g, :SUBLANES, :], sf_ref[g, SUBLANES:, :]
        out = []
        for j in range(tiles):
            cb = jnp.broadcast_to(c_scr[g, 0, j:j + 1, :], (SUBLANES, S5_RI))
            cs = jnp.broadcast_to(c_scr[g, 1, j:j + 1, :], (SUBLANES, S5_RI))
            out.append(ss[g][j * SUBLANES:(j + 1) * SUBLANES] + fa * cb + fb * cs)
        outs.append(jnp.concatenate(out, axis=0))
    return outs


def _spread(a, rep):
    hi = a.astype(BF16)
    lo = (a - hi.astype(F32)).astype(BF16)
    return _dot(hi, rep) + _dot(lo, rep)


def _s5_kernel(x_ref, bb_ref, ct_ref, pt_ref, rep_ref, skip_ref, min_ref, sc_ref, sf_ref, o_ref,
               mi_scr, mo_scr, s_scr, c_scr, *, chunks_per_seq):
    rows = x_ref.shape[1]
    n_idx = jax.lax.broadcasted_iota(jnp.int32, (rows, S5_RI), 0) % chunks_per_seq
    halves = S5_T // S5_GB
    gp = S5_GB * S5_STATE
    cl_re = _spread(ct_ref[0].reshape(gp, 2 * S5_GROUP), rep_ref[0])
    cl_im = _spread(ct_ref[1].reshape(gp, 2 * S5_GROUP), rep_ref[0])
    pt_re, pt_im = pt_ref[0].reshape(gp, 2 * S5_GROUP), pt_ref[1].reshape(gp, 2 * S5_GROUP)
    readout = []
    for k0 in range(2):
        pl_re, pl_im = _spread(pt_re, rep_ref[1 + k0]), _spread(pt_im, rep_ref[1 + k0])
        readout.append((cl_re * pl_re - cl_im * pl_im, -(cl_re * pl_im + cl_im * pl_re)))
    for gi in range(S5_GB):
        ps = slice(gi * S5_STATE, (gi + 1) * S5_STATE)
        tap = jnp.concatenate([readout[0][0][ps], readout[0][1][ps]], axis=0)
        mo_scr[gi] = jnp.concatenate([readout[1][0][ps], readout[1][1][ps]], axis=0).astype(BF16)
        strip = jnp.dot(bb_ref[gi], tap, preferred_element_type=F32,
                        precision=jax.lax.Precision.HIGHEST) + skip_ref[gi]
        for s in range(S5_T):
            mi_scr[gi, s * S5_GROUP:(s + 1) * S5_GROUP, :] = _toeplitz_rows(strip, s).astype(BF16)
    xin = [_piece_transpose([x_ref[0, :, (S5_GB * h + i) * LANES:(S5_GB * h + i + 1) * LANES]
                             for i in range(S5_GB)]) for h in range(halves)]
    groups = range(S5_GB)
    xg = [jnp.concatenate([xin[h][g] for h in range(halves)], axis=1) for g in groups]
    ys = [_dot(xg[g], mi_scr[g]) for g in groups]
    ss = [_dot(xg[g], min_ref[g]) for g in groups]
    ss = _chunk_scan(ss, sc_ref, sf_ref, s_scr, c_scr, chunks_per_seq=chunks_per_seq)
    for g in groups:
        s_prev = jnp.where(n_idx >= 1, pltpu.roll(ss[g], 1, axis=0), 0.0)
        ys[g] = jax.nn.gelu(ys[g] + _dot(s_prev.astype(BF16), mo_scr[g])).astype(o_ref.dtype)
    for h in range(halves):
        out = _piece_transpose([ys[gi][:, h * LANES:(h + 1) * LANES] for gi in range(S5_GB)])
        for i in range(S5_GB):
            t = S5_GB * h + i
            o_ref[0, :, t * LANES:(t + 1) * LANES] = out[i]


def _s5_core(x_tiles, bb, ct, pt, rep, skip, m_in, sc, sf, *, chunks_per_seq):
    tiles, rows, width = x_tiles.shape
    return pl.pallas_call(
        functools.partial(_s5_kernel, chunks_per_seq=chunks_per_seq),
        grid=(tiles,),
        in_specs=[
            pl.BlockSpec((1, rows, width), lambda i: (i, 0, 0)),
            pl.BlockSpec((S5_GB, S5_GROUP, S5_RI), lambda i: (i, 0, 0)),
            pl.BlockSpec((2, S5_GB, S5_STATE, 2 * S5_GROUP), lambda i: (0, i, 0, 0)),
            pl.BlockSpec((2, S5_GB, S5_STATE, 2 * S5_GROUP), lambda i: (0, i, 0, 0)),
            pl.BlockSpec((3, 2 * S5_GROUP, S5_TC), lambda i: (0, 0, 0)),
            pl.BlockSpec((S5_GB, S5_GROUP, S5_TC), lambda i: (i, 0, 0)),
            pl.BlockSpec((S5_GB, S5_TC, S5_RI), lambda i: (i, 0, 0)),
            pl.BlockSpec((S5_GB, 2 * SUBLANES, S5_RI), lambda i: (i, 0, 0)),
            pl.BlockSpec((S5_GB, 2 * SUBLANES, S5_RI), lambda i: (i, 0, 0)),
        ],
        out_specs=pl.BlockSpec((1, rows, width), lambda i: (i, 0, 0)),
        out_shape=jax.ShapeDtypeStruct((tiles, rows, width), BF16),
        scratch_shapes=[pltpu.VMEM((S5_GB, S5_TC, S5_TC), BF16),
                        pltpu.VMEM((S5_GB, S5_RI, S5_TC), BF16),
                        pltpu.VMEM((S5_GB, rows, S5_RI), F32),
                        pltpu.VMEM((S5_GB, 2, rows // SUBLANES, S5_RI), F32)],
        compiler_params=_params("parallel"),
        name="s5_core",
    )(x_tiles, bb, ct, pt, rep, skip, m_in, sc, sf)


def _s5_tables(a_re, a_im, log_dt, b_re, b_im, c_re, c_im, d):
    a_re, a_im = a_re.astype(F32), a_im.astype(F32)
    dt = jnp.exp(log_dt.astype(F32))[:, None]
    mag = jnp.exp(a_re * dt)
    lb_re, lb_im = mag * jnp.cos(a_im * dt), mag * jnp.sin(a_im * dt)
    den = a_re * a_re + a_im * a_im
    q_re = ((lb_re - 1.0) * a_re + lb_im * a_im) / den
    q_im = (lb_im * a_re - (lb_re - 1.0) * a_im) / den
    b_re = b_re.astype(F32).transpose(0, 2, 1)
    b_im = b_im.astype(F32).transpose(0, 2, 1)
    bb_re = q_re[:, None, :] * b_re - q_im[:, None, :] * b_im
    bb_im = q_re[:, None, :] * b_im + q_im[:, None, :] * b_re
    ks = jnp.arange(S5_T + 1, dtype=F32)[:, None, None]
    pmag = jnp.exp(ks * (a_re * dt))
    pk_re = pmag * jnp.cos(ks * (a_im * dt))
    pk_im = pmag * jnp.sin(ks * (a_im * dt))
    bb = jnp.concatenate([bb_re, bb_im], axis=2)
    pr_re, pr_im = pk_re[S5_T - 1::-1], pk_im[S5_T - 1::-1]
    e_re = pr_re[:, :, None, :] * bb_re - pr_im[:, :, None, :] * bb_im
    e_im = pr_re[:, :, None, :] * bb_im + pr_im[:, :, None, :] * bb_re
    m_in = jnp.concatenate([e_re, e_im], axis=3)
    m_in = m_in.transpose(1, 0, 2, 3).reshape(S5_GROUPS, S5_TC, S5_RI).astype(BF16)
    pad32 = lambda a: jnp.pad(a, ((0, 0), (0, 0), (0, 2 * S5_GROUP - a.shape[2])))
    ct = jnp.stack([pad32(c_re.astype(F32).transpose(0, 2, 1)), pad32(c_im.astype(F32).transpose(0, 2, 1))])
    pt = jnp.stack([pad32(pk_re.transpose(1, 2, 0)), pad32(pk_im.transpose(1, 2, 0))])
    lane = jnp.arange(S5_TC)[None, :]
    row = jnp.arange(2 * S5_GROUP)[:, None]
    rep = jnp.stack([(lane % S5_GROUP == row), (lane // S5_GROUP == row),
                     (lane // S5_GROUP + 1 == row)]).astype(BF16)
    skip = d.astype(F32).reshape(S5_GROUPS, 1, S5_GROUP) * jnp.eye(S5_GROUP, dtype=F32)
    skip = jnp.pad(skip, ((0, 0), (0, 0), (0, S5_TC - S5_GROUP)))
    f_re, f_im = pk_re[S5_T], pk_im[S5_T]
    rows_a, rows_b = [], []
    for _ in range(S5_SCAN_STEPS):
        rows_a.append(jnp.concatenate([f_re, f_re], axis=1))
        rows_b.append(jnp.concatenate([-f_im, f_im], axis=1))
        f_re, f_im = f_re * f_re - f_im * f_im, 2.0 * f_re * f_im
    zero = jnp.zeros_like(rows_a[0])
    sc = jnp.stack(rows_a + [zero] + rows_b + [zero], axis=1)
    f_re, f_im = pk_re[S5_T], pk_im[S5_T]
    rows_a, rows_b = [], []
    for _ in range(SUBLANES):
        rows_a.append(jnp.concatenate([f_re, f_re], axis=1))
        rows_b.append(jnp.concatenate([-f_im, f_im], axis=1))
        f_re, f_im = (f_re * pk_re[S5_T] - f_im * pk_im[S5_T], f_re * pk_im[S5_T] + f_im * pk_re[S5_T])
    sf = jnp.stack(rows_a + rows_b, axis=1)
    return bb, ct, pt, rep, skip, m_in, sc, sf


def kernel(x, p, norm_mix, norm_ffn, norm_ple, norm_final, gm_w_in, gm_ln_g, gm_ln_b, gm_w_s, gm_b_s, gm_w_out, s5_w_in, s5_a_re, s5_a_im, s5_log_dt, s5_b_re, s5_b_im, s5_c_re, s5_c_im, s5_d, s5_w_out, ffn_w1, ffn_w3, ffn_w2, ple_w_gate, ple_w_proj):
    bsz, seq, d = x.shape
    depth = p.shape[0]
    m = bsz * seq
    xs = x.reshape(m, d)
    ps = p.reshape(depth, m, PLE_DIM)
    s5_in_b16 = s5_out_b16 = None
    for i in range(depth):
        j = i // 2
        if i % 2 == 0:
            gm_in_b16, = _cast_call([(gm_w_in, j)])
            z, gm_out_b16 = _gm_in(xs, norm_mix[i], gm_in_b16, [(gm_w_out, j)], tm=512)
            xs, hn = _gm_mix(z, gm_ln_g, gm_ln_b, gm_w_s, gm_b_s, j, gm_out_b16, xs, norm_ffn[i], tm=512)
        else:
            tables = _s5_tables(s5_a_re[j], s5_a_im[j], s5_log_dt[j], s5_b_re[j], s5_b_im[j],
                                s5_c_re[j], s5_c_im[j], s5_d[j])
            u = _s5_in(xs, norm_mix[i], s5_in_b16, tm=1024)
            y = _s5_core(u, *tables, chunks_per_seq=seq // S5_T)
            xs, hn = _s5_out(y, s5_out_b16, xs, norm_ffn[i], tm=512)
        a, w2_b16, wg_b16, wp_b16 = _ffn_up(
            hn, ffn_w1, ffn_w3, i, [(ffn_w2, i), (ple_w_gate, i), (ple_w_proj, i)], tm=2048, tf=512)
        next_s5 = i + 1 < depth and (i + 1) % 2 == 1
        xs, *cast = _ffn_down(a, w2_b16, xs, [(s5_w_in, (i + 1) // 2)] if next_s5 else [], tm=256)
        if next_s5:
            s5_in_b16 = cast[0]
        xs, *cast = _ple(xs, norm_ple[i], ps, i, wg_b16, wp_b16,
                         norm_final if i == depth - 1 else None,
                         [(s5_w_out, (i + 1) // 2)] if next_s5 else [], tm=512)
        if next_s5:
            s5_out_b16 = cast[0]
    return xs.reshape(bsz, seq, d)
```

```python
import functools

import jax
import jax.numpy as jnp
from jax.experimental import pallas as pl
from jax.experimental.pallas import tpu as pltpu

F32 = jnp.float32
BF16 = jnp.bfloat16

D_MODEL = 2048
PLE_DIM = 256
EPS = 1e-6
LANES = 128
SUBLANES = 8
BF16_ROWS = 16

GM_CHUNK = 128
GM_HEAD_DIM = 128
GM_HEADS = D_MODEL // GM_HEAD_DIM

S5_GROUP = 16
S5_GROUPS = D_MODEL // S5_GROUP
S5_STATE = 64
S5_T = 16
S5_TC = S5_T * S5_GROUP
S5_RI = 2 * S5_STATE
S5_GB = LANES // S5_GROUP
S5_TILES = D_MODEL // LANES
S5_SCAN_STEPS = 7

W_CHUNKS = 16
SUB_ROWS = 256
VMEM_LIMIT = 56 * 1024 * 1024


def _params(*sem):
    return pltpu.CompilerParams(dimension_semantics=sem, vmem_limit_bytes=VMEM_LIMIT)


def _rms(xf, gain):
    ms = jnp.mean(xf * xf, axis=-1, keepdims=True)
    return xf * jax.lax.rsqrt(ms + EPS) * gain


def _dot(a, b):
    return jnp.dot(a, b, preferred_element_type=F32)


def _sub_blocks(tm):
    sub = min(tm, SUB_ROWS)
    return [slice(r, r + sub) for r in range(0, tm, sub)]


def _cast_rows(k, steps):
    rows = BF16_ROWS
    while k % rows or k // rows > steps:
        rows += BF16_ROWS
    return rows


def _cast_specs(casts, steps, step_of):
    ins, outs, shapes = [], [], []
    for arr, layer in casts:
        _, k, n = arr.shape
        rows = _cast_rows(k, steps)
        last = k // rows - 1
        ins.append(pl.BlockSpec(
            (None, rows, n), lambda *g, layer=layer, last=last: (layer, jnp.minimum(step_of(*g), last), 0)))
        outs.append(pl.BlockSpec((rows, n), lambda *g, last=last: (jnp.minimum(step_of(*g), last), 0)))
        shapes.append(jax.ShapeDtypeStruct((k, n), BF16))
    return ins, outs, shapes


def _run_casts(ci_refs, co_refs):
    for ci, co in zip(ci_refs, co_refs):
        co[...] = ci[...].astype(BF16)


def _cast_kernel(*refs):
    half = len(refs) // 2
    _run_casts(refs[:half], refs[half:])


def _cast_call(casts):
    cast_in, cast_out, cast_shape = _cast_specs(casts, W_CHUNKS, lambda s: s)
    return pl.pallas_call(
        _cast_kernel,
        grid=(W_CHUNKS,),
        in_specs=cast_in,
        out_specs=cast_out,
        out_shape=cast_shape,
        compiler_params=_params("arbitrary"),
        name="cast_weights",
    )(*[arr for arr, _ in casts])


def _resident_call(body, *, name, m, tm, weights, row_ins, consts, outs, casts=(), scratch=()):
    nw, nk, nr, nc, no = len(weights), len(casts), len(row_ins), len(consts), len(outs)
    steps = m // tm

    def kern(*refs):
        pos = 0

        def take(n):
            nonlocal pos
            pos += n
            return refs[pos - n:pos]

        w_refs, ci_refs, r_refs, c_refs = take(nw), take(nk), take(nr), take(nc)
        o_refs, co_refs, s_refs = take(no), take(nk), take(len(scratch))
        _run_casts(ci_refs, co_refs)
        body(w_refs, r_refs, c_refs, o_refs, s_refs)

    in_specs, args = [], []
    for arr in weights:
        in_specs.append(pl.BlockSpec(arr.shape, lambda s: (0, 0), pipeline_mode=pl.Buffered(1)))
        args.append(arr)
    cast_in, cast_out, cast_shape = _cast_specs(casts, steps, lambda s: s)
    in_specs += cast_in
    args += [arr for arr, _ in casts]
    for arr, block, index_fn in row_ins:
        in_specs.append(pl.BlockSpec(block, index_fn))
        args.append(arr)
    for arr in consts:
        in_specs.append(pl.BlockSpec((1, arr.shape[1]), lambda s: (0, 0)))
        args.append(arr)
    out_specs = [pl.BlockSpec(block, index_fn) for _, _, block, index_fn in outs]
    out_shape = [jax.ShapeDtypeStruct(shape, dtype) for shape, dtype, _, _ in outs]
    return pl.pallas_call(
        kern,
        grid=(steps,),
        in_specs=in_specs,
        out_specs=out_specs + cast_out,
        out_shape=out_shape + cast_shape,
        scratch_shapes=list(scratch),
        compiler_params=_params("arbitrary"),
        name=name,
    )(*args)


def _rows(tm, n):
    return (tm, n), (lambda i: (i, 0))


def _gm_in(x, gain, wb16, casts, *, tm):
    m, k = x.shape
    n = wb16.shape[1]

    def body(wb, r, c, o, scr):
        for rs in _sub_blocks(tm):
            h = _rms(r[0][rs, :], c[0][...]).astype(BF16)
            o[0][rs, :] = jax.nn.gelu(_dot(h, wb[0][...])).astype(BF16)

    blk_x, idx = _rows(tm, k)
    blk_o, _ = _rows(tm, n)
    return _resident_call(
        body, name="gm_in", m=m, tm=tm, weights=[wb16], casts=casts,
        row_ins=[(x, blk_x, idx)], consts=[gain.reshape(1, k)],
        outs=[((m, n), BF16, blk_o, idx)])


def _gm_mix(z, ln_g, ln_b, w_s, b_s, layer, wb16, res, gain, *, tm):
    m, n = res.shape
    bs = jnp.broadcast_to(b_s[layer][:, :, None], (GM_HEADS, GM_CHUNK, GM_HEAD_DIM)).astype(F32)

    def body(wb, r, c, o, scr):
        u_ref, v_ref, res_ref, ws_ref, bs_ref = r
        row = jax.lax.broadcasted_iota(jnp.int32, (GM_CHUNK, GM_CHUNK), 0)
        col = jax.lax.broadcasted_iota(jnp.int32, (GM_CHUNK, GM_CHUNK), 1)
        wm = [jnp.where(row >= col, ws_ref[h], 0.0).astype(BF16) for h in range(GM_HEADS)]
        for rs in _sub_blocks(tm):
            for c0 in range(rs.start, rs.stop, GM_CHUNK):
                rows = slice(c0, c0 + GM_CHUNK)
                v = v_ref[rows, :].astype(F32)
                mu = jnp.mean(v, axis=-1, keepdims=True)
                vc = v - mu
                var = jnp.mean(vc * vc, axis=-1, keepdims=True)
                vn = (vc * jax.lax.rsqrt(var + EPS) * c[1][...] + c[2][...]).astype(BF16)
                for h in range(GM_HEADS):
                    cols = slice(h * GM_HEAD_DIM, (h + 1) * GM_HEAD_DIM)
                    sv = _dot(wm[h], vn[:, cols]) + bs_ref[h]
                    scr[0][rows, cols] = (u_ref[rows, cols].astype(F32) * sv).astype(BF16)
            xn = res_ref[rs, :] + _dot(scr[0][rs, :], wb[0][...])
            o[0][rs, :] = xn
            o[1][rs, :] = _rms(xn, c[0][...]).astype(BF16)

    blk_x, idx = _rows(tm, n)
    return _resident_call(
        body, name="gm_mix", m=m, tm=tm, weights=[wb16],
        row_ins=[(z, blk_x, idx), (z, blk_x, lambda i: (i, 1)), (res, blk_x, idx),
                 (w_s, (None, GM_HEADS, GM_CHUNK, GM_CHUNK), lambda i: (layer, 0, 0, 0)),
                 (bs, (GM_HEADS, GM_CHUNK, GM_HEAD_DIM), lambda i: (0, 0, 0))],
        consts=[gain.reshape(1, n), ln_g[layer].reshape(1, n), ln_b[layer].reshape(1, n)],
        outs=[((m, n), F32, blk_x, idx), ((m, n), BF16, blk_x, idx)],
        scratch=[pltpu.VMEM((tm, n), BF16)])


def _ffn_down(a, w2b16, res, casts, *, tm):
    m, n = res.shape

    def body(wb, r, c, o, scr):
        for rs in _sub_blocks(tm):
            o[0][rs, :] = r[1][rs, :] + _dot(r[0][rs, :], wb[0][...])

    blk_a, idx = _rows(tm, a.shape[1])
    blk_x, _ = _rows(tm, n)
    return _resident_call(
        body, name="ffn_down", m=m, tm=tm, weights=[w2b16], casts=casts,
        row_ins=[(a, blk_a, idx), (res, blk_x, idx)], consts=[],
        outs=[((m, n), F32, blk_x, idx)])


def _ple(x, gain, p, layer, wgb16, wpb16, casts, *, tm, final_gain=None, s5_in=None):
    assert final_gain is None or s5_in is None
    m, n = x.shape
    tc = tm // S5_T

    def body(wb, r, c, o, scr):
        for rs in _sub_blocks(tm):
            xf = r[0][rs, :]
            h = _rms(xf, c[0][...]).astype(BF16)
            gate = jax.nn.sigmoid(_dot(h, wb[0][...]))
            proj = _dot(r[1][rs, :].astype(BF16), wb[1][...])
            xn = xf + gate * proj
            if final_gain is not None:
                xn = _rms(xn, c[1][...])
            o[0][rs, :] = xn
            if s5_in is not None:
                u = _dot(_rms(xn, c[1][...]).astype(BF16), wb[2][...])
                for l in range(S5_TILES):
                    scr[0][l, rs, :] = u[:, l * LANES:(l + 1) * LANES]
                sub = rs.stop - rs.start
                cs = slice(rs.start // S5_T, rs.stop // S5_T)
                for t in range(S5_T):
                    for l in range(S5_TILES):
                        o[1][l, cs, t * LANES:(t + 1) * LANES] = (
                            scr[0][l, pl.ds(rs.start + t, sub // S5_T, stride=S5_T), :].astype(BF16))

    blk_x, idx = _rows(tm, n)
    blk_p, _ = _rows(tm, PLE_DIM)
    weights, consts = [wgb16, wpb16], [gain.reshape(1, n)]
    outs, scratch = [((m, n), F32, blk_x, idx)], []
    if final_gain is not None:
        consts.append(final_gain.reshape(1, n))
    if s5_in is not None:
        consts.append(s5_in[0].reshape(1, n))
        weights.append(s5_in[1])
        outs.append(((S5_TILES, m // S5_T, S5_T * LANES), BF16, (S5_TILES, tc, S5_T * LANES),
                     lambda i: (0, i, 0)))
        scratch.append(pltpu.VMEM((S5_TILES, tm, LANES), F32))
    return _resident_call(
        body, name="ple", m=m, tm=tm, weights=weights, casts=casts,
        row_ins=[(x, blk_x, idx), (p, (None,) + blk_p, lambda i: (layer, i, 0))], consts=consts,
        outs=outs, scratch=scratch)


def _s5_out(y_tiles, wb16, res, gain, *, tm):
    m, n = res.shape
    tc = tm // S5_T

    def body(wb, r, c, o, scr):
        for rs in _sub_blocks(tm):
            sub = rs.stop - rs.start
            cs = slice(rs.start // S5_T, rs.stop // S5_T)
            for t in range(S5_T):
                for l in range(S5_TILES):
                    scr[0][l, pl.ds(rs.start + t, sub // S5_T, stride=S5_T), :] = (
                        r[0][l, cs, t * LANES:(t + 1) * LANES].astype(F32))
            y = jnp.concatenate([scr[0][l, rs, :] for l in range(S5_TILES)], axis=1).astype(BF16)
            vg = _dot(y, wb[0][...])
            xn = r[1][rs, :] + vg[:, :n] * jax.nn.sigmoid(vg[:, n:])
            o[0][rs, :] = xn
            o[1][rs, :] = _rms(xn, c[0][...]).astype(BF16)

    blk_x, idx = _rows(tm, n)
    return _resident_call(
        body, name="s5_out", m=m, tm=tm, weights=[wb16],
        row_ins=[(y_tiles, (S5_TILES, tc, S5_T * LANES), lambda i: (0, i, 0)), (res, blk_x, idx)],
        consts=[gain.reshape(1, n)],
        outs=[((m, n), F32, blk_x, idx), ((m, n), BF16, blk_x, idx)],
        scratch=[pltpu.VMEM((S5_TILES, tm, LANES), F32)])


def _ffn_up_kernel(*refs, tm, n_casts):
    h_ref, w1_ref, w3_ref = refs[:3]
    ci_refs = refs[3:3 + n_casts]
    o_ref = refs[3 + n_casts]
    co_refs = refs[4 + n_casts:4 + 2 * n_casts]
    w1_scr, w3_scr = refs[4 + 2 * n_casts:]
    _run_casts(ci_refs, co_refs)

    @pl.when(pl.program_id(1) == 0)
    def _():
        w1_scr[...] = w1_ref[...].astype(BF16)
        w3_scr[...] = w3_ref[...].astype(BF16)

    for rs in _sub_blocks(tm):
        h = h_ref[rs, :]
        o_ref[rs, :] = (jax.nn.silu(_dot(h, w1_scr[...])) * _dot(h, w3_scr[...])).astype(o_ref.dtype)


def _ffn_up(h, w1, w3, layer, casts, *, tm, tf):
    m, d = h.shape
    ff = w1.shape[2]
    ni = m // tm
    cast_in, cast_out, cast_shape = _cast_specs(casts, (ff // tf) * ni, lambda f, i: f * ni + i)
    return pl.pallas_call(
        functools.partial(_ffn_up_kernel, tm=tm, n_casts=len(casts)),
        grid=(ff // tf, ni),
        in_specs=[
            pl.BlockSpec((tm, d), lambda f, i: (i, 0)),
            pl.BlockSpec((None, d, tf), lambda f, i: (layer, 0, f)),
            pl.BlockSpec((None, d, tf), lambda f, i: (layer, 0, f)),
        ] + cast_in,
        out_specs=[pl.BlockSpec((tm, tf), lambda f, i: (i, f))] + cast_out,
        out_shape=[jax.ShapeDtypeStruct((m, ff), BF16)] + cast_shape,
        scratch_shapes=[pltpu.VMEM((d, tf), BF16), pltpu.VMEM((d, tf), BF16)],
        compiler_params=_params("arbitrary", "arbitrary"),
        name="ffn_up",
    )(h, w1, w3, *[arr for arr, _ in casts])


def _piece_transpose(v):
    piece = jax.lax.broadcasted_iota(jnp.int32, v[0].shape, 1) // S5_GROUP
    for delta in (4, 2, 1):
        keep = (piece & delta) == 0
        shift = delta * S5_GROUP
        new = list(v)
        for i in range(S5_GB):
            if i & delta == 0:
                a, b = v[i], v[i + delta]
                new[i] = jnp.where(keep, a, pltpu.roll(b, shift, axis=1))
                new[i + delta] = jnp.where(keep, pltpu.roll(a, LANES - shift, axis=1), b)
        v = new
    return v


def _toeplitz_rows(strip, s):
    lo, hi = strip[:, :LANES], strip[:, LANES:]
    lane = jax.lax.broadcasted_iota(jnp.int32, lo.shape, 1)
    shift = (s * S5_GROUP) % LANES
    if s == 0:
        return strip
    if s * S5_GROUP < LANES:
        lo_r, hi_r = pltpu.roll(lo, shift, axis=1), pltpu.roll(hi, shift, axis=1)
        return jnp.concatenate([jnp.where(lane >= shift, lo_r, 0.0),
                                jnp.where(lane >= shift, hi_r, lo_r)], axis=1)
    lo_r = lo if shift == 0 else pltpu.roll(lo, shift, axis=1)
    return jnp.concatenate([jnp.zeros_like(lo), jnp.where(lane >= shift, lo_r, 0.0)], axis=1)


def _cmul_add(s, fa, fb, x):
    return s + fa * x + fb * pltpu.roll(x, S5_STATE, axis=1)


def _chunk_scan(ss, sc_ref, sf_ref, s_scr, c_scr, *, chunks_per_seq):
    groups = range(len(ss))
    rows = ss[0].shape[0]
    tiles = rows // SUBLANES
    r_idx = jax.lax.broadcasted_iota(jnp.int32, ss[0].shape, 0) % SUBLANES
    for k in range(3):
        d = 1 << k
        ss = [_cmul_add(ss[g], sc_ref[g, k:k + 1, :], sc_ref[g, 8 + k:9 + k, :],
                        jnp.where(r_idx >= d, pltpu.roll(ss[g], d, axis=0), 0.0)) for g in groups]
    for g in groups:
        s_scr[g] = ss[g]
    ts = [s_scr[g, pl.ds(SUBLANES - 1, tiles, stride=SUBLANES), :] for g in groups]
    j_idx = jax.lax.broadcasted_iota(jnp.int32, ts[0].shape, 0) % (chunks_per_seq // SUBLANES)
    for k in range(3, S5_SCAN_STEPS):
        d = 1 << (k - 3)
        ts = [_cmul_add(ts[g], sc_ref[g, k:k + 1, :], sc_ref[g, 8 + k:9 + k, :],
                        jnp.where(j_idx >= d, pltpu.roll(ts[g], d, axis=0), 0.0)) for g in groups]
    for g in groups:
        carry = jnp.where(j_idx >= 1, pltpu.roll(ts[g], 1, axis=0), 0.0)
        c_scr[g, 0] = carry
        c_scr[g, 1] = pltpu.roll(carry, S5_STATE, axis=1)
    outs = []
    for g in groups:
        fa, fb = sf_ref[g, :SUBLANES, :], sf_ref[g, SUBLANES:, :]
        out = []
        for j in range(tiles):
            cb = jnp.broadcast_to(c_scr[g, 0, j:j + 1, :], (SUBLANES, S5_RI))
            cs = jnp.broadcast_to(c_scr[g, 1, j:j + 1, :], (SUBLANES, S5_RI))
            out.append(ss[g][j * SUBLANES:(j + 1) * SUBLANES] + fa * cb + fb * cs)
        outs.append(jnp.concatenate(out, axis=0))
    return outs


def _spread(a, rep):
    hi = a.astype(BF16)
    lo = (a - hi.astype(F32)).astype(BF16)
    return _dot(hi, rep) + _dot(lo, rep)


def _s5_kernel(x_ref, bb_ref, ct_ref, pt_ref, rep_ref, skip_ref, min_ref, sc_ref, sf_ref, o_ref,
               mi_scr, mo_scr, s_scr, c_scr, *, chunks_per_seq):
    rows = x_ref.shape[1]
    n_idx = jax.lax.broadcasted_iota(jnp.int32, (rows, S5_RI), 0) % chunks_per_seq
    halves = S5_T // S5_GB
    gp = S5_GB * S5_STATE
    cl_re = _spread(ct_ref[0].reshape(gp, 2 * S5_GROUP), rep_ref[0])
    cl_im = _spread(ct_ref[1].reshape(gp, 2 * S5_GROUP), rep_ref[0])
    pt_re, pt_im = pt_ref[0].reshape(gp, 2 * S5_GROUP), pt_ref[1].reshape(gp, 2 * S5_GROUP)
    readout = []
    for k0 in range(2):
        pl_re, pl_im = _spread(pt_re, rep_ref[1 + k0]), _spread(pt_im, rep_ref[1 + k0])
        readout.append((cl_re * pl_re - cl_im * pl_im, -(cl_re * pl_im + cl_im * pl_re)))
    for gi in range(S5_GB):
        ps = slice(gi * S5_STATE, (gi + 1) * S5_STATE)
        tap = jnp.concatenate([readout[0][0][ps], readout[0][1][ps]], axis=0)
        mo_scr[gi] = jnp.concatenate([readout[1][0][ps], readout[1][1][ps]], axis=0).astype(BF16)
        strip = jnp.dot(bb_ref[gi], tap, preferred_element_type=F32,
                        precision=jax.lax.Precision.HIGHEST) + skip_ref[gi]
        for s in range(S5_T):
            mi_scr[gi, s * S5_GROUP:(s + 1) * S5_GROUP, :] = _toeplitz_rows(strip, s).astype(BF16)
    xin = [_piece_transpose([x_ref[0, :, (S5_GB * h + i) * LANES:(S5_GB * h + i + 1) * LANES]
                             for i in range(S5_GB)]) for h in range(halves)]
    groups = range(S5_GB)
    xg = [jnp.concatenate([xin[h][g] for h in range(halves)], axis=1) for g in groups]
    ys = [_dot(xg[g], mi_scr[g]) for g in groups]
    ss = [_dot(xg[g], min_ref[g]) for g in groups]
    ss = _chunk_scan(ss, sc_ref, sf_ref, s_scr, c_scr, chunks_per_seq=chunks_per_seq)
    for g in groups:
        s_prev = jnp.where(n_idx >= 1, pltpu.roll(ss[g], 1, axis=0), 0.0)
        ys[g] = jax.nn.gelu(ys[g] + _dot(s_prev.astype(BF16), mo_scr[g])).astype(o_ref.dtype)
    for h in range(halves):
        out = _piece_transpose([ys[gi][:, h * LANES:(h + 1) * LANES] for gi in range(S5_GB)])
        for i in range(S5_GB):
            t = S5_GB * h + i
            o_ref[0, :, t * LANES:(t + 1) * LANES] = out[i]


def _s5_core(x_tiles, bb, ct, pt, rep, skip, m_in, sc, sf, *, chunks_per_seq):
    tiles, rows, width = x_tiles.shape
    return pl.pallas_call(
        functools.partial(_s5_kernel, chunks_per_seq=chunks_per_seq),
        grid=(tiles,),
        in_specs=[
            pl.BlockSpec((1, rows, width), lambda i: (i, 0, 0)),
            pl.BlockSpec((S5_GB, S5_GROUP, S5_RI), lambda i: (i, 0, 0)),
            pl.BlockSpec((2, S5_GB, S5_STATE, 2 * S5_GROUP), lambda i: (0, i, 0, 0)),
            pl.BlockSpec((2, S5_GB, S5_STATE, 2 * S5_GROUP), lambda i: (0, i, 0, 0)),
            pl.BlockSpec((3, 2 * S5_GROUP, S5_TC), lambda i: (0, 0, 0)),
            pl.BlockSpec((S5_GB, S5_GROUP, S5_TC), lambda i: (i, 0, 0)),
            pl.BlockSpec((S5_GB, S5_TC, S5_RI), lambda i: (i, 0, 0)),
            pl.BlockSpec((S5_GB, 2 * SUBLANES, S5_RI), lambda i: (i, 0, 0)),
            pl.BlockSpec((S5_GB, 2 * SUBLANES, S5_RI), lambda i: (i, 0, 0)),
        ],
        out_specs=pl.BlockSpec((1, rows, width), lambda i: (i, 0, 0)),
        out_shape=jax.ShapeDtypeStruct((tiles, rows, width), BF16),
        scratch_shapes=[pltpu.VMEM((S5_GB, S5_TC, S5_TC), BF16),
                        pltpu.VMEM((S5_GB, S5_RI, S5_TC), BF16),
                        pltpu.VMEM((S5_GB, rows, S5_RI), F32),
                        pltpu.VMEM((S5_GB, 2, rows // SUBLANES, S5_RI), F32)],
        compiler_params=_params("parallel"),
        name="s5_core",
    )(x_tiles, bb, ct, pt, rep, skip, m_in, sc, sf)


def _s5_tables(a_re, a_im, log_dt, b_re, b_im, c_re, c_im, d):
    a_re, a_im = a_re.astype(F32), a_im.astype(F32)
    dt = jnp.exp(log_dt.astype(F32))[:, None]
    mag = jnp.exp(a_re * dt)
    lb_re, lb_im = mag * jnp.cos(a_im * dt), mag * jnp.sin(a_im * dt)
    den = a_re * a_re + a_im * a_im
    q_re = ((lb_re - 1.0) * a_re + lb_im * a_im) / den
    q_im = (lb_im * a_re - (lb_re - 1.0) * a_im) / den
    b_re = b_re.astype(F32).transpose(0, 2, 1)
    b_im = b_im.astype(F32).transpose(0, 2, 1)
    bb_re = q_re[:, None, :] * b_re - q_im[:, None, :] * b_im
    bb_im = q_re[:, None, :] * b_im + q_im[:, None, :] * b_re
    ks = jnp.arange(S5_T + 1, dtype=F32)[:, None, None]
    pmag = jnp.exp(ks * (a_re * dt))
    pk_re = pmag * jnp.cos(ks * (a_im * dt))
    pk_im = pmag * jnp.sin(ks * (a_im * dt))
    bb = jnp.concatenate([bb_re, bb_im], axis=2)
    pr_re, pr_im = pk_re[S5_T - 1::-1], pk_im[S5_T - 1::-1]
    e_re = pr_re[:, :, None, :] * bb_re - pr_im[:, :, None, :] * bb_im
    e_im = pr_re[:, :, None, :] * bb_im + pr_im[:, :, None, :] * bb_re
    m_in = jnp.concatenate([e_re, e_im], axis=3)
    m_in = m_in.transpose(1, 0, 2, 3).reshape(S5_GROUPS, S5_TC, S5_RI).astype(BF16)
    pad32 = lambda a: jnp.pad(a, ((0, 0), (0, 0), (0, 2 * S5_GROUP - a.shape[2])))
    ct = jnp.stack([pad32(c_re.astype(F32).transpose(0, 2, 1)), pad32(c_im.astype(F32).transpose(0, 2, 1))])
    pt = jnp.stack([pad32(pk_re.transpose(1, 2, 0)), pad32(pk_im.transpose(1, 2, 0))])
    lane = jnp.arange(S5_TC)[None, :]
    row = jnp.arange(2 * S5_GROUP)[:, None]
    rep = jnp.stack([(lane % S5_GROUP == row), (lane // S5_GROUP == row),
                     (lane // S5_GROUP + 1 == row)]).astype(BF16)
    skip = d.astype(F32).reshape(S5_GROUPS, 1, S5_GROUP) * jnp.eye(S5_GROUP, dtype=F32)
    skip = jnp.pad(skip, ((0, 0), (0, 0), (0, S5_TC - S5_GROUP)))
    f_re, f_im = pk_re[S5_T], pk_im[S5_T]
    rows_a, rows_b = [], []
    for _ in range(S5_SCAN_STEPS):
        rows_a.append(jnp.concatenate([f_re, f_re], axis=1))
        rows_b.append(jnp.concatenate([-f_im, f_im], axis=1))
        f_re, f_im = f_re * f_re - f_im * f_im, 2.0 * f_re * f_im
    zero = jnp.zeros_like(rows_a[0])
    sc = jnp.stack(rows_a + [zero] + rows_b + [zero], axis=1)
    f_re, f_im = pk_re[S5_T], pk_im[S5_T]
    rows_a, rows_b = [], []
    for _ in range(SUBLANES):
        rows_a.append(jnp.concatenate([f_re, f_re], axis=1))
        rows_b.append(jnp.concatenate([-f_im, f_im], axis=1))
        f_re, f_im = (f_re * pk_re[S5_T] - f_im * pk_im[S5_T], f_re * pk_im[S5_T] + f_im * pk_re[S5_T])
    sf = jnp.stack(rows_a + rows_b, axis=1)
    return bb, ct, pt, rep, skip, m_in, sc, sf


def kernel(x, p, norm_mix, norm_ffn, norm_ple, norm_final, gm_w_in, gm_ln_g, gm_ln_b, gm_w_s, gm_b_s, gm_w_out, s5_w_in, s5_a_re, s5_a_im, s5_log_dt, s5_b_re, s5_b_im, s5_c_re, s5_c_im, s5_d, s5_w_out, ffn_w1, ffn_w3, ffn_w2, ple_w_gate, ple_w_proj):
    bsz, seq, d = x.shape
    depth = p.shape[0]
    m = bsz * seq
    xs = x.reshape(m, d)
    ps = p.reshape(depth, m, PLE_DIM)
    u = s5_out_b16 = None
    for i in range(depth):
        j = i // 2
        if i % 2 == 0:
            gm_in_b16, = _cast_call([(gm_w_in, j)])
            z, gm_out_b16 = _gm_in(xs, norm_mix[i], gm_in_b16, [(gm_w_out, j)], tm=512)
            xs, hn = _gm_mix(z, gm_ln_g, gm_ln_b, gm_w_s, gm_b_s, j, gm_out_b16, xs, norm_ffn[i], tm=512)
        else:
            tables = _s5_tables(s5_a_re[j], s5_a_im[j], s5_log_dt[j], s5_b_re[j], s5_b_im[j],
                                s5_c_re[j], s5_c_im[j], s5_d[j])
            y = _s5_core(u, *tables, chunks_per_seq=seq // S5_T)
            xs, hn = _s5_out(y, s5_out_b16, xs, norm_ffn[i], tm=512)
        a, w2_b16, wg_b16, wp_b16 = _ffn_up(
            hn, ffn_w1, ffn_w3, i, [(ffn_w2, i), (ple_w_gate, i), (ple_w_proj, i)], tm=2048, tf=512)
        if i + 1 < depth and (i + 1) % 2 == 1:
            jn = (i + 1) // 2
            xs, s5_in_b16 = _ffn_down(a, w2_b16, xs, [(s5_w_in, jn)], tm=256)
            xs, u, s5_out_b16 = _ple(xs, norm_ple[i], ps, i, wg_b16, wp_b16, [(s5_w_out, jn)], tm=512,
                                     s5_in=(norm_mix[i + 1], s5_in_b16))
        else:
            xs, = _ffn_down(a, w2_b16, xs, [], tm=256)
            xs, = _ple(xs, norm_ple[i], ps, i, wg_b16, wp_b16, [], tm=512,
                       final_gain=norm_final if i == depth - 1 else None)
    return xs.reshape(bsz, seq, d)
```

```python
import functools

import jax
import jax.numpy as jnp
from jax.experimental import pallas as pl
from jax.experimental.pallas import tpu as pltpu

F32 = jnp.float32
BF16 = jnp.bfloat16

D_MODEL = 2048
PLE_DIM = 256
EPS = 1e-6
LANES = 128
SUBLANES = 8
BF16_ROWS = 16

GM_CHUNK = 128
GM_HEAD_DIM = 128
GM_HEADS = D_MODEL // GM_HEAD_DIM

S5_GROUP = 16
S5_GROUPS = D_MODEL // S5_GROUP
S5_STATE = 64
S5_T = 16
S5_TC = S5_T * S5_GROUP
S5_RI = 2 * S5_STATE
S5_GB = LANES // S5_GROUP
S5_TILES = D_MODEL // LANES
S5_SCAN_STEPS = 7

W_CHUNKS = 16
SUB_ROWS = 256
SUB_COLS = 512
VMEM_LIMIT = 56 * 1024 * 1024


def _params(*sem):
    return pltpu.CompilerParams(dimension_semantics=sem, vmem_limit_bytes=VMEM_LIMIT)


def _rms(xf, gain):
    ms = jnp.mean(xf * xf, axis=-1, keepdims=True)
    return xf * jax.lax.rsqrt(ms + EPS) * gain


def _dot(a, b):
    return jnp.dot(a, b, preferred_element_type=F32)


def _sub_blocks(tm):
    sub = min(tm, SUB_ROWS)
    return [slice(r, r + sub) for r in range(0, tm, sub)]


def _col_blocks(n):
    return [slice(c, c + SUB_COLS) for c in range(0, n, SUB_COLS)]


def _cast_rows(k, steps):
    rows = BF16_ROWS
    while k % rows or k // rows > steps:
        rows += BF16_ROWS
    return rows


def _cast_specs(casts, steps, step_of):
    ins, outs, shapes = [], [], []
    for arr, layer in casts:
        _, k, n = arr.shape
        rows = _cast_rows(k, steps)
        last = k // rows - 1
        ins.append(pl.BlockSpec(
            (None, rows, n), lambda *g, layer=layer, last=last: (layer, jnp.minimum(step_of(*g), last), 0)))
        outs.append(pl.BlockSpec((rows, n), lambda *g, last=last: (jnp.minimum(step_of(*g), last), 0)))
        shapes.append(jax.ShapeDtypeStruct((k, n), BF16))
    return ins, outs, shapes


def _run_casts(ci_refs, co_refs):
    for ci, co in zip(ci_refs, co_refs):
        co[...] = ci[...].astype(BF16)


def _cast_kernel(*refs):
    half = len(refs) // 2
    _run_casts(refs[:half], refs[half:])


def _cast_call(casts):
    cast_in, cast_out, cast_shape = _cast_specs(casts, W_CHUNKS, lambda s: s)
    return pl.pallas_call(
        _cast_kernel,
        grid=(W_CHUNKS,),
        in_specs=cast_in,
        out_specs=cast_out,
        out_shape=cast_shape,
        compiler_params=_params("arbitrary"),
        name="cast_weights",
    )(*[arr for arr, _ in casts])


def _resident_call(body, *, name, m, tm, weights, row_ins, consts, outs, casts=(), scratch=()):
    nw, nk, nr, nc, no = len(weights), len(casts), len(row_ins), len(consts), len(outs)
    steps = m // tm

    def kern(*refs):
        pos = 0

        def take(n):
            nonlocal pos
            pos += n
            return refs[pos - n:pos]

        w_refs, ci_refs, r_refs, c_refs = take(nw), take(nk), take(nr), take(nc)
        o_refs, co_refs, s_refs = take(no), take(nk), take(len(scratch))
        _run_casts(ci_refs, co_refs)
        body(w_refs, r_refs, c_refs, o_refs, s_refs)

    in_specs, args = [], []
    for arr in weights:
        in_specs.append(pl.BlockSpec(arr.shape, lambda s: (0, 0), pipeline_mode=pl.Buffered(1)))
        args.append(arr)
    cast_in, cast_out, cast_shape = _cast_specs(casts, steps, lambda s: s)
    in_specs += cast_in
    args += [arr for arr, _ in casts]
    for arr, block, index_fn in row_ins:
        in_specs.append(pl.BlockSpec(block, index_fn))
        args.append(arr)
    for arr in consts:
        in_specs.append(pl.BlockSpec((1, arr.shape[1]), lambda s: (0, 0)))
        args.append(arr)
    out_specs = [pl.BlockSpec(block, index_fn) for _, _, block, index_fn in outs]
    out_shape = [jax.ShapeDtypeStruct(shape, dtype) for shape, dtype, _, _ in outs]
    return pl.pallas_call(
        kern,
        grid=(steps,),
        in_specs=in_specs,
        out_specs=out_specs + cast_out,
        out_shape=out_shape + cast_shape,
        scratch_shapes=list(scratch),
        compiler_params=_params("arbitrary"),
        name=name,
    )(*args)


def _rows(tm, n):
    return (tm, n), (lambda i: (i, 0))


def _gm_in(x, gain, wb16, casts, *, tm):
    m, k = x.shape
    n = wb16.shape[1]

    def body(wb, r, c, o, scr):
        for rs in _sub_blocks(tm):
            h = _rms(r[0][rs, :], c[0][...]).astype(BF16)
            for cs in _col_blocks(n):
                o[0][rs, cs] = jax.nn.gelu(_dot(h, wb[0][:, cs])).astype(BF16)

    blk_x, idx = _rows(tm, k)
    blk_o, _ = _rows(tm, n)
    return _resident_call(
        body, name="gm_in", m=m, tm=tm, weights=[wb16], casts=casts,
        row_ins=[(x, blk_x, idx)], consts=[gain.reshape(1, k)],
        outs=[((m, n), BF16, blk_o, idx)])


def _gm_mix(z, ln_g, ln_b, w_s, b_s, layer, wb16, res, gain, *, tm):
    m, n = res.shape
    bs = jnp.broadcast_to(b_s[layer][:, :, None], (GM_HEADS, GM_CHUNK, GM_HEAD_DIM)).astype(F32)

    def body(wb, r, c, o, scr):
        u_ref, v_ref, res_ref, ws_ref, bs_ref = r
        row = jax.lax.broadcasted_iota(jnp.int32, (GM_CHUNK, GM_CHUNK), 0)
        col = jax.lax.broadcasted_iota(jnp.int32, (GM_CHUNK, GM_CHUNK), 1)
        wm = [jnp.where(row >= col, ws_ref[h], 0.0).astype(BF16) for h in range(GM_HEADS)]
        for rs in _sub_blocks(tm):
            for c0 in range(rs.start, rs.stop, GM_CHUNK):
                rows = slice(c0, c0 + GM_CHUNK)
                v = v_ref[rows, :].astype(F32)
                mu = jnp.mean(v, axis=-1, keepdims=True)
                vc = v - mu
                var = jnp.mean(vc * vc, axis=-1, keepdims=True)
                vn = (vc * jax.lax.rsqrt(var + EPS) * c[1][...] + c[2][...]).astype(BF16)
                for h in range(GM_HEADS):
                    cols = slice(h * GM_HEAD_DIM, (h + 1) * GM_HEAD_DIM)
                    sv = _dot(wm[h], vn[:, cols]) + bs_ref[h]
                    scr[0][rows, cols] = (u_ref[rows, cols].astype(F32) * sv).astype(BF16)
            g = scr[0][rs, :]
            xn = jnp.concatenate([res_ref[rs, cs] + _dot(g, wb[0][:, cs]) for cs in _col_blocks(n)], axis=1)
            o[0][rs, :] = xn
            o[1][rs, :] = _rms(xn, c[0][...]).astype(BF16)

    blk_x, idx = _rows(tm, n)
    return _resident_call(
        body, name="gm_mix", m=m, tm=tm, weights=[wb16],
        row_ins=[(z, blk_x, idx), (z, blk_x, lambda i: (i, 1)), (res, blk_x, idx),
                 (w_s, (None, GM_HEADS, GM_CHUNK, GM_CHUNK), lambda i: (layer, 0, 0, 0)),
                 (bs, (GM_HEADS, GM_CHUNK, GM_HEAD_DIM), lambda i: (0, 0, 0))],
        consts=[gain.reshape(1, n), ln_g[layer].reshape(1, n), ln_b[layer].reshape(1, n)],
        outs=[((m, n), F32, blk_x, idx), ((m, n), BF16, blk_x, idx)],
        scratch=[pltpu.VMEM((tm, n), BF16)])


def _ffn_down(a, w2b16, res, casts, *, tm):
    m, n = res.shape

    def body(wb, r, c, o, scr):
        for rs in _sub_blocks(tm):
            a_blk = r[0][rs, :]
            for cs in _col_blocks(n):
                o[0][rs, cs] = r[1][rs, cs] + _dot(a_blk, wb[0][:, cs])

    blk_a, idx = _rows(tm, a.shape[1])
    blk_x, _ = _rows(tm, n)
    return _resident_call(
        body, name="ffn_down", m=m, tm=tm, weights=[w2b16], casts=casts,
        row_ins=[(a, blk_a, idx), (res, blk_x, idx)], consts=[],
        outs=[((m, n), F32, blk_x, idx)])


def _ple(x, gain, p, layer, wgb16, wpb16, casts, *, tm, final_gain=None, s5_in=None):
    assert final_gain is None or s5_in is None
    m, n = x.shape
    tc = tm // S5_T

    def body(wb, r, c, o, scr):
        for rs in _sub_blocks(tm):
            xf = r[0][rs, :]
            h = _rms(xf, c[0][...]).astype(BF16)
            pb = r[1][rs, :].astype(BF16)
            xn = jnp.concatenate(
                [xf[:, cs] + jax.nn.sigmoid(_dot(h, wb[0][:, cs])) * _dot(pb, wb[1][:, cs])
                 for cs in _col_blocks(n)], axis=1)
            if final_gain is not None:
                xn = _rms(xn, c[1][...])
            o[0][rs, :] = xn
            if s5_in is not None:
                u = _dot(_rms(xn, c[1][...]).astype(BF16), wb[2][...])
                for l in range(S5_TILES):
                    scr[0][l, rs, :] = u[:, l * LANES:(l + 1) * LANES]
                sub = rs.stop - rs.start
                cs = slice(rs.start // S5_T, rs.stop // S5_T)
                for t in range(S5_T):
                    for l in range(S5_TILES):
                        o[1][l, cs, t * LANES:(t + 1) * LANES] = (
                            scr[0][l, pl.ds(rs.start + t, sub // S5_T, stride=S5_T), :].astype(BF16))

    blk_x, idx = _rows(tm, n)
    blk_p, _ = _rows(tm, PLE_DIM)
    weights, consts = [wgb16, wpb16], [gain.reshape(1, n)]
    outs, scratch = [((m, n), F32, blk_x, idx)], []
    if final_gain is not None:
        consts.append(final_gain.reshape(1, n))
    if s5_in is not None:
        consts.append(s5_in[0].reshape(1, n))
        weights.append(s5_in[1])
        outs.append(((S5_TILES, m // S5_T, S5_T * LANES), BF16, (S5_TILES, tc, S5_T * LANES),
                     lambda i: (0, i, 0)))
        scratch.append(pltpu.VMEM((S5_TILES, tm, LANES), F32))
    return _resident_call(
        body, name="ple", m=m, tm=tm, weights=weights, casts=casts,
        row_ins=[(x, blk_x, idx), (p, (None,) + blk_p, lambda i: (layer, i, 0))], consts=consts,
        outs=outs, scratch=scratch)


def _s5_out(y_tiles, wb16, res, gain, *, tm):
    m, n = res.shape
    tc = tm // S5_T

    def body(wb, r, c, o, scr):
        for rs in _sub_blocks(tm):
            sub = rs.stop - rs.start
            cs = slice(rs.start // S5_T, rs.stop // S5_T)
            for t in range(S5_T):
                for l in range(S5_TILES):
                    scr[0][l, pl.ds(rs.start + t, sub // S5_T, stride=S5_T), :] = (
                        r[0][l, cs, t * LANES:(t + 1) * LANES].astype(F32))
            y = jnp.concatenate([scr[0][l, rs, :] for l in range(S5_TILES)], axis=1).astype(BF16)
            xn = jnp.concatenate(
                [r[1][rs, cs] + _dot(y, wb[0][:, cs])
                 * jax.nn.sigmoid(_dot(y, wb[0][:, slice(n + cs.start, n + cs.stop)]))
                 for cs in _col_blocks(n)], axis=1)
            o[0][rs, :] = xn
            o[1][rs, :] = _rms(xn, c[0][...]).astype(BF16)

    blk_x, idx = _rows(tm, n)
    return _resident_call(
        body, name="s5_out", m=m, tm=tm, weights=[wb16],
        row_ins=[(y_tiles, (S5_TILES, tc, S5_T * LANES), lambda i: (0, i, 0)), (res, blk_x, idx)],
        consts=[gain.reshape(1, n)],
        outs=[((m, n), F32, blk_x, idx), ((m, n), BF16, blk_x, idx)],
        scratch=[pltpu.VMEM((S5_TILES, tm, LANES), F32)])


def _ffn_up_kernel(*refs, tm, n_casts):
    h_ref, w1_ref, w3_ref = refs[:3]
    ci_refs = refs[3:3 + n_casts]
    o_ref = refs[3 + n_casts]
    co_refs = refs[4 + n_casts:4 + 2 * n_casts]
    w1_scr, w3_scr = refs[4 + 2 * n_casts:]
    _run_casts(ci_refs, co_refs)

    @pl.when(pl.program_id(1) == 0)
    def _():
        w1_scr[...] = w1_ref[...].astype(BF16)
        w3_scr[...] = w3_ref[...].astype(BF16)

    for rs in _sub_blocks(tm):
        h = h_ref[rs, :]
        o_ref[rs, :] = (jax.nn.silu(_dot(h, w1_scr[...])) * _dot(h, w3_scr[...])).astype(o_ref.dtype)


def _ffn_up(h, w1, w3, layer, casts, *, tm, tf):
    m, d = h.shape
    ff = w1.shape[2]
    ni = m // tm
    cast_in, cast_out, cast_shape = _cast_specs(casts, (ff // tf) * ni, lambda f, i: f * ni + i)
    return pl.pallas_call(
        functools.partial(_ffn_up_kernel, tm=tm, n_casts=len(casts)),
        grid=(ff // tf, ni),
        in_specs=[
            pl.BlockSpec((tm, d), lambda f, i: (i, 0)),
            pl.BlockSpec((None, d, tf), lambda f, i: (layer, 0, f)),
            pl.BlockSpec((None, d, tf), lambda f, i: (layer, 0, f)),
        ] + cast_in,
        out_specs=[pl.BlockSpec((tm, tf), lambda f, i: (i, f))] + cast_out,
        out_shape=[jax.ShapeDtypeStruct((m, ff), BF16)] + cast_shape,
        scratch_shapes=[pltpu.VMEM((d, tf), BF16), pltpu.VMEM((d, tf), BF16)],
        compiler_params=_params("arbitrary", "arbitrary"),
        name="ffn_up",
    )(h, w1, w3, *[arr for arr, _ in casts])


def _piece_transpose(v):
    piece = jax.lax.broadcasted_iota(jnp.int32, v[0].shape, 1) // S5_GROUP
    for delta in (4, 2, 1):
        keep = (piece & delta) == 0
        shift = delta * S5_GROUP
        new = list(v)
        for i in range(S5_GB):
            if i & delta == 0:
                a, b = v[i], v[i + delta]
                new[i] = jnp.where(keep, a, pltpu.roll(b, shift, axis=1))
                new[i + delta] = jnp.where(keep, pltpu.roll(a, LANES - shift, axis=1), b)
        v = new
    return v


def _toeplitz_rows(strip, s):
    lo, hi = strip[:, :LANES], strip[:, LANES:]
    lane = jax.lax.broadcasted_iota(jnp.int32, lo.shape, 1)
    shift = (s * S5_GROUP) % LANES
    if s == 0:
        return strip
    if s * S5_GROUP < LANES:
        lo_r, hi_r = pltpu.roll(lo, shift, axis=1), pltpu.roll(hi, shift, axis=1)
        return jnp.concatenate([jnp.where(lane >= shift, lo_r, 0.0),
                                jnp.where(lane >= shift, hi_r, lo_r)], axis=1)
    lo_r = lo if shift == 0 else pltpu.roll(lo, shift, axis=1)
    return jnp.concatenate([jnp.zeros_like(lo), jnp.where(lane >= shift, lo_r, 0.0)], axis=1)


def _cmul_add(s, fa, fb, x):
    return s + fa * x + fb * pltpu.roll(x, S5_STATE, axis=1)


def _chunk_scan(ss, sc_ref, sf_ref, s_scr, c_scr, *, chunks_per_seq):
    groups = range(len(ss))
    rows = ss[0].shape[0]
    tiles = rows // SUBLANES
    r_idx = jax.lax.broadcasted_iota(jnp.int32, ss[0].shape, 0) % SUBLANES
    for k in range(3):
        d = 1 << k
        ss = [_cmul_add(ss[g], sc_ref[g, k:k + 1, :], sc_ref[g, 8 + k:9 + k, :],
                        jnp.where(r_idx >= d, pltpu.roll(ss[g], d, axis=0), 0.0)) for g in groups]
    for g in groups:
        s_scr[g] = ss[g]
    ts = [s_scr[g, pl.ds(SUBLANES - 1, tiles, stride=SUBLANES), :] for g in groups]
    j_idx = jax.lax.broadcasted_iota(jnp.int32, ts[0].shape, 0) % (chunks_per_seq // SUBLANES)
    for k in range(3, S5_SCAN_STEPS):
        d = 1 << (k - 3)
        ts = [_cmul_add(ts[g], sc_ref[g, k:k + 1, :], sc_ref[g, 8 + k:9 + k, :],
                        jnp.where(j_idx >= d, pltpu.roll(ts[g], d, axis=0), 0.0)) for g in groups]
    for g in groups:
        carry = jnp.where(j_idx >= 1, pltpu.roll(ts[g], 1, axis=0), 0.0)
        c_scr[g, 0] = carry
        c_scr[g, 1] = pltpu.roll(carry, S5_STATE, axis=1)
    outs = []
    for g in groups:
        fa, fb = sf_ref[g, :SUBLANES, :], sf_ref[g, SUBLANES:, :]
        out = []
        for j in range(tiles):
            cb = jnp.broadcast_to(c_scr[g, 0, j:j + 1, :], (SUBLANES, S5_RI))
            cs = jnp.broadcast_to(c_scr[g, 1, j:j + 1, :], (SUBLANES, S5_RI))
            out.append(ss[g][j * SUBLANES:(j + 1) * SUBLANES] + fa * cb + fb * cs)
        outs.append(jnp.concatenate(out, axis=0))
    return outs


def _s5_kernel(x_ref, bb_ref, cp_ref, rep_ref, skip_ref, pr_ref, sc_ref, sf_ref, o_ref,
               mi_scr, mn_scr, mo_scr, s_scr, c_scr, *, chunks_per_seq):
    rows = x_ref.shape[1]
    n_idx = jax.lax.broadcasted_iota(jnp.int32, (rows, S5_RI), 0) % chunks_per_seq
    halves = S5_T // S5_GB
    cp = cp_ref[...].reshape(S5_GB * S5_STATE, LANES)
    cp_hi = cp.astype(BF16)
    cp_lo = (cp - cp_hi.astype(F32)).astype(BF16)
    cl_re, cl_im, p0_re, p0_im, p1_re, p1_im = [
        _dot(cp_hi, rep_ref[i]) + _dot(cp_lo, rep_ref[i]) for i in range(6)]
    readout = [(cl_re * pl_re - cl_im * pl_im, -(cl_re * pl_im + cl_im * pl_re))
               for pl_re, pl_im in ((p0_re, p0_im), (p1_re, p1_im))]
    for gi in range(S5_GB):
        ps = slice(gi * S5_STATE, (gi + 1) * S5_STATE)
        tap = jnp.concatenate([readout[0][0][ps], readout[0][1][ps]], axis=0)
        mo_scr[gi] = jnp.concatenate([readout[1][0][ps], readout[1][1][ps]], axis=0).astype(BF16)
        bb = bb_ref[gi]
        strip = jnp.dot(bb, tap, preferred_element_type=F32,
                        precision=jax.lax.Precision.HIGHEST) + skip_ref[gi]
        bb_sw = pltpu.roll(bb, S5_STATE, axis=1)
        for s in range(S5_T):
            rs = slice(s * S5_GROUP, (s + 1) * S5_GROUP)
            mi_scr[gi, rs, :] = _toeplitz_rows(strip, s).astype(BF16)
            mn_scr[gi, rs, :] = (pr_ref[gi, s:s + 1, :] * bb
                                 + pr_ref[gi, S5_T + s:S5_T + s + 1, :] * bb_sw).astype(BF16)
    xin = [_piece_transpose([x_ref[0, :, (S5_GB * h + i) * LANES:(S5_GB * h + i + 1) * LANES]
                             for i in range(S5_GB)]) for h in range(halves)]
    groups = range(S5_GB)
    xg = [jnp.concatenate([xin[h][g] for h in range(halves)], axis=1) for g in groups]
    ys = [_dot(xg[g], mi_scr[g]) for g in groups]
    ss = [_dot(xg[g], mn_scr[g]) for g in groups]
    ss = _chunk_scan(ss, sc_ref, sf_ref, s_scr, c_scr, chunks_per_seq=chunks_per_seq)
    for g in groups:
        s_prev = jnp.where(n_idx >= 1, pltpu.roll(ss[g], 1, axis=0), 0.0)
        ys[g] = jax.nn.gelu(ys[g] + _dot(s_prev.astype(BF16), mo_scr[g])).astype(o_ref.dtype)
    for h in range(halves):
        out = _piece_transpose([ys[gi][:, h * LANES:(h + 1) * LANES] for gi in range(S5_GB)])
        for i in range(S5_GB):
            t = S5_GB * h + i
            o_ref[0, :, t * LANES:(t + 1) * LANES] = out[i]


def _s5_core(x_tiles, bb, cp, rep, skip, pr, sc, sf, *, chunks_per_seq):
    tiles, rows, width = x_tiles.shape
    return pl.pallas_call(
        functools.partial(_s5_kernel, chunks_per_seq=chunks_per_seq),
        grid=(tiles,),
        in_specs=[
            pl.BlockSpec((1, rows, width), lambda i: (i, 0, 0)),
            pl.BlockSpec((S5_GB, S5_GROUP, S5_RI), lambda i: (i, 0, 0)),
            pl.BlockSpec((S5_GB, S5_STATE, LANES), lambda i: (i, 0, 0)),
            pl.BlockSpec((6, LANES, S5_TC), lambda i: (0, 0, 0)),
            pl.BlockSpec((S5_GB, S5_GROUP, S5_TC), lambda i: (i, 0, 0)),
            pl.BlockSpec((S5_GB, 2 * S5_T, S5_RI), lambda i: (i, 0, 0)),
            pl.BlockSpec((S5_GB, 2 * SUBLANES, S5_RI), lambda i: (i, 0, 0)),
            pl.BlockSpec((S5_GB, 2 * SUBLANES, S5_RI), lambda i: (i, 0, 0)),
        ],
        out_specs=pl.BlockSpec((1, rows, width), lambda i: (i, 0, 0)),
        out_shape=jax.ShapeDtypeStruct((tiles, rows, width), BF16),
        scratch_shapes=[pltpu.VMEM((S5_GB, S5_TC, S5_TC), BF16),
                        pltpu.VMEM((S5_GB, S5_TC, S5_RI), BF16),
                        pltpu.VMEM((S5_GB, S5_RI, S5_TC), BF16),
                        pltpu.VMEM((S5_GB, rows, S5_RI), F32),
                        pltpu.VMEM((S5_GB, 2, rows // SUBLANES, S5_RI), F32)],
        compiler_params=_params("parallel"),
        name="s5_core",
    )(x_tiles, bb, cp, rep, skip, pr, sc, sf)


def _s5_tables(a_re, a_im, log_dt, b_re, b_im, c_re, c_im, d):
    a_re, a_im = a_re.astype(F32), a_im.astype(F32)
    dt = jnp.exp(log_dt.astype(F32))[:, None]
    mag = jnp.exp(a_re * dt)
    lb_re, lb_im = mag * jnp.cos(a_im * dt), mag * jnp.sin(a_im * dt)
    den = a_re * a_re + a_im * a_im
    q_re = ((lb_re - 1.0) * a_re + lb_im * a_im) / den
    q_im = (lb_im * a_re - (lb_re - 1.0) * a_im) / den
    b_re = b_re.astype(F32).transpose(0, 2, 1)
    b_im = b_im.astype(F32).transpose(0, 2, 1)
    bb_re = q_re[:, None, :] * b_re - q_im[:, None, :] * b_im
    bb_im = q_re[:, None, :] * b_im + q_im[:, None, :] * b_re
    ks = jnp.arange(S5_T + 1, dtype=F32)[:, None, None]
    pmag = jnp.exp(ks * (a_re * dt))
    pk_re = pmag * jnp.cos(ks * (a_im * dt))
    pk_im = pmag * jnp.sin(ks * (a_im * dt))
    bb = jnp.concatenate([bb_re, bb_im], axis=2)
    pr_re, pr_im = pk_re[S5_T - 1::-1], pk_im[S5_T - 1::-1]
    pr = jnp.concatenate([jnp.concatenate([pr_re, pr_re], axis=2),
                          jnp.concatenate([-pr_im, pr_im], axis=2)], axis=0).transpose(1, 0, 2)
    pad32 = lambda a: jnp.pad(a, ((0, 0), (0, 0), (0, 2 * S5_GROUP - a.shape[2])))
    cp = jnp.concatenate([c_re.astype(F32).transpose(0, 2, 1), c_im.astype(F32).transpose(0, 2, 1),
                          pad32(pk_re.transpose(1, 2, 0)), pad32(pk_im.transpose(1, 2, 0)),
                          jnp.zeros((S5_GROUPS, S5_STATE, LANES - 6 * S5_GROUP), F32)], axis=2)
    lane = jnp.arange(S5_TC)[None, :]
    row = jnp.arange(LANES)[:, None]
    spread_c = lambda r0: (row >= r0) & (row < r0 + S5_GROUP) & (lane % S5_GROUP == row - r0)
    spread_k = lambda r0, k0: (row >= r0) & (row < r0 + 2 * S5_GROUP) & (lane // S5_GROUP + k0 == row - r0)
    rep = jnp.stack([spread_c(0), spread_c(S5_GROUP),
                     spread_k(2 * S5_GROUP, 0), spread_k(4 * S5_GROUP, 0),
                     spread_k(2 * S5_GROUP, 1), spread_k(4 * S5_GROUP, 1)]).astype(BF16)
    skip = d.astype(F32).reshape(S5_GROUPS, 1, S5_GROUP) * jnp.eye(S5_GROUP, dtype=F32)
    skip = jnp.pad(skip, ((0, 0), (0, 0), (0, S5_TC - S5_GROUP)))
    f_re, f_im = pk_re[S5_T], pk_im[S5_T]
    rows_a, rows_b = [], []
    for _ in range(S5_SCAN_STEPS):
        rows_a.append(jnp.concatenate([f_re, f_re], axis=1))
        rows_b.append(jnp.concatenate([-f_im, f_im], axis=1))
        f_re, f_im = f_re * f_re - f_im * f_im, 2.0 * f_re * f_im
    zero = jnp.zeros_like(rows_a[0])
    sc = jnp.stack(rows_a + [zero] + rows_b + [zero], axis=1)
    f_re, f_im = pk_re[S5_T], pk_im[S5_T]
    rows_a, rows_b = [], []
    for _ in range(SUBLANES):
        rows_a.append(jnp.concatenate([f_re, f_re], axis=1))
        rows_b.append(jnp.concatenate([-f_im, f_im], axis=1))
        f_re, f_im = (f_re * pk_re[S5_T] - f_im * pk_im[S5_T], f_re * pk_im[S5_T] + f_im * pk_re[S5_T])
    sf = jnp.stack(rows_a + rows_b, axis=1)
    return bb, cp, rep, skip, pr, sc, sf


def kernel(x, p, norm_mix, norm_ffn, norm_ple, norm_final, gm_w_in, gm_ln_g, gm_ln_b, gm_w_s, gm_b_s, gm_w_out, s5_w_in, s5_a_re, s5_a_im, s5_log_dt, s5_b_re, s5_b_im, s5_c_re, s5_c_im, s5_d, s5_w_out, ffn_w1, ffn_w3, ffn_w2, ple_w_gate, ple_w_proj):
    bsz, seq, d = x.shape
    depth = p.shape[0]
    m = bsz * seq
    xs = x.reshape(m, d)
    ps = p.reshape(depth, m, PLE_DIM)
    u = s5_out_b16 = None
    for i in range(depth):
        j = i // 2
        if i % 2 == 0:
            gm_in_b16, = _cast_call([(gm_w_in, j)])
            z, gm_out_b16 = _gm_in(xs, norm_mix[i], gm_in_b16, [(gm_w_out, j)], tm=512)
            xs, hn = _gm_mix(z, gm_ln_g, gm_ln_b, gm_w_s, gm_b_s, j, gm_out_b16, xs, norm_ffn[i], tm=512)
        else:
            tables = _s5_tables(s5_a_re[j], s5_a_im[j], s5_log_dt[j], s5_b_re[j], s5_b_im[j],
                                s5_c_re[j], s5_c_im[j], s5_d[j])
            y = _s5_core(u, *tables, chunks_per_seq=seq // S5_T)
            xs, hn = _s5_out(y, s5_out_b16, xs, norm_ffn[i], tm=512)
        a, w2_b16, wg_b16, wp_b16 = _ffn_up(
            hn, ffn_w1, ffn_w3, i, [(ffn_w2, i), (ple_w_gate, i), (ple_w_proj, i)], tm=2048, tf=512)
        if i + 1 < depth and (i + 1) % 2 == 1:
            jn = (i + 1) // 2
            xs, s5_in_b16 = _ffn_down(a, w2_b16, xs, [(s5_w_in, jn)], tm=256)
            xs, u, s5_out_b16 = _ple(xs, norm_ple[i], ps, i, wg_b16, wp_b16, [(s5_w_out, jn)], tm=512,
                                     s5_in=(norm_mix[i + 1], s5_in_b16))
        else:
            xs, = _ffn_down(a, w2_b16, xs, [], tm=256)
            xs, = _ple(xs, norm_ple[i], ps, i, wg_b16, wp_b16, [], tm=512,
                       final_gain=norm_final if i == depth - 1 else None)
    return xs.reshape(bsz, seq, d)
```

```python
import functools

import jax
import jax.numpy as jnp
import numpy as np
from jax.experimental import pallas as pl
from jax.experimental.pallas import tpu as pltpu

F32 = jnp.float32
BF16 = jnp.bfloat16

D_MODEL = 2048
PLE_DIM = 256
EPS = 1e-6
LANES = 128
SUBLANES = 8
BF16_ROWS = 16

GM_CHUNK = 128
GM_HEAD_DIM = 128
GM_HEADS = D_MODEL // GM_HEAD_DIM

S5_GROUP = 16
S5_GROUPS = D_MODEL // S5_GROUP
S5_STATE = 64
S5_T = 16
S5_TC = S5_T * S5_GROUP
S5_RI = 2 * S5_STATE
S5_GB = LANES // S5_GROUP
S5_TILES = D_MODEL // LANES
S5_SCAN_STEPS = 7

W_CHUNKS = 16
SUB_ROWS = 256
SUB_COLS = 512
VMEM_LIMIT = 56 * 1024 * 1024


def _params(*sem):
    return pltpu.CompilerParams(dimension_semantics=sem, vmem_limit_bytes=VMEM_LIMIT)


def _rms(xf, gain):
    ms = jnp.mean(xf * xf, axis=-1, keepdims=True)
    return xf * jax.lax.rsqrt(ms + EPS) * gain


def _dot(a, b):
    return jnp.dot(a, b, preferred_element_type=F32)


def _sub_blocks(tm):
    sub = min(tm, SUB_ROWS)
    return [slice(r, r + sub) for r in range(0, tm, sub)]


def _col_blocks(n):
    return [slice(c, c + SUB_COLS) for c in range(0, n, SUB_COLS)]


def _cast_rows(k, steps):
    rows = BF16_ROWS
    while k % rows or k // rows > steps:
        rows += BF16_ROWS
    return rows


def _cast_specs(casts, steps, step_of):
    ins, outs, shapes = [], [], []
    for arr, layer in casts:
        _, k, n = arr.shape
        rows = _cast_rows(k, steps)
        last = k // rows - 1
        ins.append(pl.BlockSpec(
            (None, rows, n), lambda *g, layer=layer, last=last: (layer, jnp.minimum(step_of(*g), last), 0)))
        outs.append(pl.BlockSpec((rows, n), lambda *g, last=last: (jnp.minimum(step_of(*g), last), 0)))
        shapes.append(jax.ShapeDtypeStruct((k, n), BF16))
    return ins, outs, shapes


def _run_casts(ci_refs, co_refs):
    for ci, co in zip(ci_refs, co_refs):
        co[...] = ci[...].astype(BF16)


def _cast_kernel(*refs):
    half = len(refs) // 2
    _run_casts(refs[:half], refs[half:])


def _cast_call(casts):
    cast_in, cast_out, cast_shape = _cast_specs(casts, W_CHUNKS, lambda s: s)
    return pl.pallas_call(
        _cast_kernel,
        grid=(W_CHUNKS,),
        in_specs=cast_in,
        out_specs=cast_out,
        out_shape=cast_shape,
        compiler_params=_params("arbitrary"),
        name="cast_weights",
    )(*[arr for arr, _ in casts])


def _resident_call(body, *, name, m, tm, weights, row_ins, consts, outs, casts=(), scratch=()):
    nw, nk, nr, nc, no = len(weights), len(casts), len(row_ins), len(consts), len(outs)
    steps = m // tm

    def kern(*refs):
        pos = 0

        def take(n):
            nonlocal pos
            pos += n
            return refs[pos - n:pos]

        w_refs, ci_refs, r_refs, c_refs = take(nw), take(nk), take(nr), take(nc)
        o_refs, co_refs, s_refs = take(no), take(nk), take(len(scratch))
        _run_casts(ci_refs, co_refs)
        body(w_refs, r_refs, c_refs, o_refs, s_refs)

    in_specs, args = [], []
    for arr in weights:
        in_specs.append(pl.BlockSpec(arr.shape, lambda s: (0, 0), pipeline_mode=pl.Buffered(1)))
        args.append(arr)
    cast_in, cast_out, cast_shape = _cast_specs(casts, steps, lambda s: s)
    in_specs += cast_in
    args += [arr for arr, _ in casts]
    for arr, block, index_fn in row_ins:
        in_specs.append(pl.BlockSpec(block, index_fn))
        args.append(arr)
    for arr in consts:
        in_specs.append(pl.BlockSpec((1, arr.shape[1]), lambda s: (0, 0)))
        args.append(arr)
    out_specs = [pl.BlockSpec(block, index_fn) for _, _, block, index_fn in outs]
    out_shape = [jax.ShapeDtypeStruct(shape, dtype) for shape, dtype, _, _ in outs]
    return pl.pallas_call(
        kern,
        grid=(steps,),
        in_specs=in_specs,
        out_specs=out_specs + cast_out,
        out_shape=out_shape + cast_shape,
        scratch_shapes=list(scratch),
        compiler_params=_params("arbitrary"),
        name=name,
    )(*args)


def _rows(tm, n):
    return (tm, n), (lambda i: (i, 0))


def _gm_in(x, gain, wb16, casts, *, tm):
    m, k = x.shape
    n = wb16.shape[1]

    def body(wb, r, c, o, scr):
        for rs in _sub_blocks(tm):
            h = _rms(r[0][rs, :], c[0][...]).astype(BF16)
            for cs in _col_blocks(n):
                o[0][rs, cs] = jax.nn.gelu(_dot(h, wb[0][:, cs])).astype(BF16)

    blk_x, idx = _rows(tm, k)
    blk_o, _ = _rows(tm, n)
    return _resident_call(
        body, name="gm_in", m=m, tm=tm, weights=[wb16], casts=casts,
        row_ins=[(x, blk_x, idx)], consts=[gain.reshape(1, k)],
        outs=[((m, n), BF16, blk_o, idx)])


def _gm_mix(z, ln_g, ln_b, w_s, b_s, layer, wb16, res, gain, *, tm):
    m, n = res.shape
    bs = jnp.broadcast_to(b_s[layer][:, :, None], (GM_HEADS, GM_CHUNK, GM_HEAD_DIM)).astype(F32)

    def body(wb, r, c, o, scr):
        u_ref, v_ref, res_ref, ws_ref, bs_ref = r
        row = jax.lax.broadcasted_iota(jnp.int32, (GM_CHUNK, GM_CHUNK), 0)
        col = jax.lax.broadcasted_iota(jnp.int32, (GM_CHUNK, GM_CHUNK), 1)
        wm = [jnp.where(row >= col, ws_ref[h], 0.0).astype(BF16) for h in range(GM_HEADS)]
        for rs in _sub_blocks(tm):
            for c0 in range(rs.start, rs.stop, GM_CHUNK):
                rows = slice(c0, c0 + GM_CHUNK)
                v = v_ref[rows, :].astype(F32)
                mu = jnp.mean(v, axis=-1, keepdims=True)
                vc = v - mu
                var = jnp.mean(vc * vc, axis=-1, keepdims=True)
                vn = (vc * jax.lax.rsqrt(var + EPS) * c[1][...] + c[2][...]).astype(BF16)
                for h in range(GM_HEADS):
                    cols = slice(h * GM_HEAD_DIM, (h + 1) * GM_HEAD_DIM)
                    sv = _dot(wm[h], vn[:, cols]) + bs_ref[h]
                    scr[0][rows, cols] = (u_ref[rows, cols].astype(F32) * sv).astype(BF16)
            xn = res_ref[rs, :] + _dot(scr[0][rs, :], wb[0][...])
            o[0][rs, :] = xn
            o[1][rs, :] = _rms(xn, c[0][...]).astype(BF16)

    blk_x, idx = _rows(tm, n)
    return _resident_call(
        body, name="gm_mix", m=m, tm=tm, weights=[wb16],
        row_ins=[(z, blk_x, idx), (z, blk_x, lambda i: (i, 1)), (res, blk_x, idx),
                 (w_s, (None, GM_HEADS, GM_CHUNK, GM_CHUNK), lambda i: (layer, 0, 0, 0)),
                 (bs, (GM_HEADS, GM_CHUNK, GM_HEAD_DIM), lambda i: (0, 0, 0))],
        consts=[gain.reshape(1, n), ln_g[layer].reshape(1, n), ln_b[layer].reshape(1, n)],
        outs=[((m, n), F32, blk_x, idx), ((m, n), BF16, blk_x, idx)],
        scratch=[pltpu.VMEM((tm, n), BF16)])


def _ffn_down(a, w2b16, res, casts, *, tm):
    m, n = res.shape

    def body(wb, r, c, o, scr):
        for rs in _sub_blocks(tm):
            o[0][rs, :] = r[1][rs, :] + _dot(r[0][rs, :], wb[0][...])

    blk_a, idx = _rows(tm, a.shape[1])
    blk_x, _ = _rows(tm, n)
    return _resident_call(
        body, name="ffn_down", m=m, tm=tm, weights=[w2b16], casts=casts,
        row_ins=[(a, blk_a, idx), (res, blk_x, idx)], consts=[],
        outs=[((m, n), F32, blk_x, idx)])


def _ple(x, gain, p, layer, wgb16, wpb16, casts, *, tm, final_gain=None, s5_in=None):
    assert final_gain is None or s5_in is None
    m, n = x.shape
    tc = tm // S5_T

    def body(wb, r, c, o, scr):
        for rs in _sub_blocks(tm):
            xf = r[0][rs, :]
            h = _rms(xf, c[0][...]).astype(BF16)
            pb = r[1][rs, :].astype(BF16)
            xn = jnp.concatenate(
                [xf[:, cs] + jax.nn.sigmoid(_dot(h, wb[0][:, cs])) * _dot(pb, wb[1][:, cs])
                 for cs in _col_blocks(n)], axis=1)
            if final_gain is not None:
                xn = _rms(xn, c[1][...])
            o[0][rs, :] = xn
            if s5_in is not None:
                u = _dot(_rms(xn, c[1][...]).astype(BF16), wb[2][...])
                for l in range(S5_TILES):
                    scr[0][l, rs, :] = u[:, l * LANES:(l + 1) * LANES]
                sub = rs.stop - rs.start
                cs = slice(rs.start // S5_T, rs.stop // S5_T)
                for t in range(S5_T):
                    for l in range(S5_TILES):
                        o[1][l, cs, t * LANES:(t + 1) * LANES] = (
                            scr[0][l, pl.ds(rs.start + t, sub // S5_T, stride=S5_T), :].astype(BF16))

    blk_x, idx = _rows(tm, n)
    blk_p, _ = _rows(tm, PLE_DIM)
    weights, consts = [wgb16, wpb16], [gain.reshape(1, n)]
    outs, scratch = [((m, n), F32, blk_x, idx)], []
    if final_gain is not None:
        consts.append(final_gain.reshape(1, n))
    if s5_in is not None:
        consts.append(s5_in[0].reshape(1, n))
        weights.append(s5_in[1])
        outs.append(((S5_TILES, m // S5_T, S5_T * LANES), BF16, (S5_TILES, tc, S5_T * LANES),
                     lambda i: (0, i, 0)))
        scratch.append(pltpu.VMEM((S5_TILES, tm, LANES), F32))
    return _resident_call(
        body, name="ple", m=m, tm=tm, weights=weights, casts=casts,
        row_ins=[(x, blk_x, idx), (p, (None,) + blk_p, lambda i: (layer, i, 0))], consts=consts,
        outs=outs, scratch=scratch)


def _s5_out(y_tiles, wb16, res, gain, *, tm):
    m, n = res.shape
    tc = tm // S5_T

    def body(wb, r, c, o, scr):
        for rs in _sub_blocks(tm):
            sub = rs.stop - rs.start
            cs = slice(rs.start // S5_T, rs.stop // S5_T)
            for t in range(S5_T):
                for l in range(S5_TILES):
                    scr[0][l, pl.ds(rs.start + t, sub // S5_T, stride=S5_T), :] = (
                        r[0][l, cs, t * LANES:(t + 1) * LANES].astype(F32))
            y = jnp.concatenate([scr[0][l, rs, :] for l in range(S5_TILES)], axis=1).astype(BF16)
            vg = _dot(y, wb[0][...])
            xn = r[1][rs, :] + vg[:, :n] * jax.nn.sigmoid(vg[:, n:])
            o[0][rs, :] = xn
            o[1][rs, :] = _rms(xn, c[0][...]).astype(BF16)

    blk_x, idx = _rows(tm, n)
    return _resident_call(
        body, name="s5_out", m=m, tm=tm, weights=[wb16],
        row_ins=[(y_tiles, (S5_TILES, tc, S5_T * LANES), lambda i: (0, i, 0)), (res, blk_x, idx)],
        consts=[gain.reshape(1, n)],
        outs=[((m, n), F32, blk_x, idx), ((m, n), BF16, blk_x, idx)],
        scratch=[pltpu.VMEM((S5_TILES, tm, LANES), F32)])


def _ffn_up_kernel(*refs, tm, n_casts):
    h_ref, w1_ref, w3_ref = refs[:3]
    ci_refs = refs[3:3 + n_casts]
    o_ref = refs[3 + n_casts]
    co_refs = refs[4 + n_casts:4 + 2 * n_casts]
    w1_scr, w3_scr = refs[4 + 2 * n_casts:]
    _run_casts(ci_refs, co_refs)

    @pl.when(pl.program_id(1) == 0)
    def _():
        w1_scr[...] = w1_ref[...].astype(BF16)
        w3_scr[...] = w3_ref[...].astype(BF16)

    for rs in _sub_blocks(tm):
        h = h_ref[rs, :]
        o_ref[rs, :] = (jax.nn.silu(_dot(h, w1_scr[...])) * _dot(h, w3_scr[...])).astype(o_ref.dtype)


def _ffn_up(h, w1, w3, layer, casts, *, tm, tf):
    m, d = h.shape
    ff = w1.shape[2]
    ni = m // tm
    cast_in, cast_out, cast_shape = _cast_specs(casts, (ff // tf) * ni, lambda f, i: f * ni + i)
    return pl.pallas_call(
        functools.partial(_ffn_up_kernel, tm=tm, n_casts=len(casts)),
        grid=(ff // tf, ni),
        in_specs=[
            pl.BlockSpec((tm, d), lambda f, i: (i, 0)),
            pl.BlockSpec((None, d, tf), lambda f, i: (layer, 0, f)),
            pl.BlockSpec((None, d, tf), lambda f, i: (layer, 0, f)),
        ] + cast_in,
        out_specs=[pl.BlockSpec((tm, tf), lambda f, i: (i, f))] + cast_out,
        out_shape=[jax.ShapeDtypeStruct((m, ff), BF16)] + cast_shape,
        scratch_shapes=[pltpu.VMEM((d, tf), BF16), pltpu.VMEM((d, tf), BF16)],
        compiler_params=_params("arbitrary", "arbitrary"),
        name="ffn_up",
    )(h, w1, w3, *[arr for arr, _ in casts])


def _piece_transpose(v):
    piece = jax.lax.broadcasted_iota(jnp.int32, v[0].shape, 1) // S5_GROUP
    for delta in (4, 2, 1):
        keep = (piece & delta) == 0
        shift = delta * S5_GROUP
        new = list(v)
        for i in range(S5_GB):
            if i & delta == 0:
                a, b = v[i], v[i + delta]
                new[i] = jnp.where(keep, a, pltpu.roll(b, shift, axis=1))
                new[i + delta] = jnp.where(keep, pltpu.roll(a, LANES - shift, axis=1), b)
        v = new
    return v


def _toeplitz_rows(strip, s):
    lo, hi = strip[:, :LANES], strip[:, LANES:]
    lane = jax.lax.broadcasted_iota(jnp.int32, lo.shape, 1)
    shift = (s * S5_GROUP) % LANES
    if s == 0:
        return strip
    if s * S5_GROUP < LANES:
        lo_r, hi_r = pltpu.roll(lo, shift, axis=1), pltpu.roll(hi, shift, axis=1)
        return jnp.concatenate([jnp.where(lane >= shift, lo_r, 0.0),
                                jnp.where(lane >= shift, hi_r, lo_r)], axis=1)
    lo_r = lo if shift == 0 else pltpu.roll(lo, shift, axis=1)
    return jnp.concatenate([jnp.zeros_like(lo), jnp.where(lane >= shift, lo_r, 0.0)], axis=1)


def _cmul_add(s, fa, fb, x):
    return s + fa * x + fb * pltpu.roll(x, S5_STATE, axis=1)


def _chunk_scan(ss, sc_ref, sf_ref, s_scr, c_scr, *, chunks_per_seq):
    groups = range(len(ss))
    rows = ss[0].shape[0]
    tiles = rows // SUBLANES
    r_idx = jax.lax.broadcasted_iota(jnp.int32, ss[0].shape, 0) % SUBLANES
    for k in range(3):
        d = 1 << k
        ss = [_cmul_add(ss[g], sc_ref[g, k:k + 1, :], sc_ref[g, 8 + k:9 + k, :],
                        jnp.where(r_idx >= d, pltpu.roll(ss[g], d, axis=0), 0.0)) for g in groups]
    for g in groups:
        s_scr[g] = ss[g]
    ts = [s_scr[g, pl.ds(SUBLANES - 1, tiles, stride=SUBLANES), :] for g in groups]
    j_idx = jax.lax.broadcasted_iota(jnp.int32, ts[0].shape, 0) % (chunks_per_seq // SUBLANES)
    for k in range(3, S5_SCAN_STEPS):
        d = 1 << (k - 3)
        ts = [_cmul_add(ts[g], sc_ref[g, k:k + 1, :], sc_ref[g, 8 + k:9 + k, :],
                        jnp.where(j_idx >= d, pltpu.roll(ts[g], d, axis=0), 0.0)) for g in groups]
    for g in groups:
        carry = jnp.where(j_idx >= 1, pltpu.roll(ts[g], 1, axis=0), 0.0)
        c_scr[g, 0] = carry
        c_scr[g, 1] = pltpu.roll(carry, S5_STATE, axis=1)
    outs = []
    for g in groups:
        fa, fb = sf_ref[g, :SUBLANES, :], sf_ref[g, SUBLANES:, :]
        out = []
        for j in range(tiles):
            cb = jnp.broadcast_to(c_scr[g, 0, j:j + 1, :], (SUBLANES, S5_RI))
            cs = jnp.broadcast_to(c_scr[g, 1, j:j + 1, :], (SUBLANES, S5_RI))
            out.append(ss[g][j * SUBLANES:(j + 1) * SUBLANES] + fa * cb + fb * cs)
        outs.append(jnp.concatenate(out, axis=0))
    return outs


def _spread(a, rep):
    hi = a.astype(BF16)
    lo = (a - hi.astype(F32)).astype(BF16)
    return _dot(hi, rep) + _dot(lo, rep)


def _s5_kernel(x_ref, bb_ref, ct_ref, pt_ref, rep_ref, skip_ref, min_ref, sc_ref, sf_ref, o_ref,
               mi_scr, mo_scr, s_scr, c_scr, *, chunks_per_seq):
    rows = x_ref.shape[1]
    n_idx = jax.lax.broadcasted_iota(jnp.int32, (rows, S5_RI), 0) % chunks_per_seq
    halves = S5_T // S5_GB
    gp = S5_GB * S5_STATE
    cl_re = _spread(ct_ref[0].reshape(gp, 2 * S5_GROUP), rep_ref[0])
    cl_im = _spread(ct_ref[1].reshape(gp, 2 * S5_GROUP), rep_ref[0])
    pt_re, pt_im = pt_ref[0].reshape(gp, 2 * S5_GROUP), pt_ref[1].reshape(gp, 2 * S5_GROUP)
    readout = []
    for k0 in range(2):
        pl_re, pl_im = _spread(pt_re, rep_ref[1 + k0]), _spread(pt_im, rep_ref[1 + k0])
        readout.append((cl_re * pl_re - cl_im * pl_im, -(cl_re * pl_im + cl_im * pl_re)))
    for gi in range(S5_GB):
        ps = slice(gi * S5_STATE, (gi + 1) * S5_STATE)
        tap = jnp.concatenate([readout[0][0][ps], readout[0][1][ps]], axis=0)
        mo_scr[gi] = jnp.concatenate([readout[1][0][ps], readout[1][1][ps]], axis=0).astype(BF16)
        strip = jnp.dot(bb_ref[gi], tap, preferred_element_type=F32,
                        precision=jax.lax.Precision.HIGHEST) + skip_ref[gi]
        for s in range(S5_T):
            mi_scr[gi, s * S5_GROUP:(s + 1) * S5_GROUP, :] = _toeplitz_rows(strip, s).astype(BF16)
    xin = [_piece_transpose([x_ref[0, :, (S5_GB * h + i) * LANES:(S5_GB * h + i + 1) * LANES]
                             for i in range(S5_GB)]) for h in range(halves)]
    groups = range(S5_GB)
    xg = [jnp.concatenate([xin[h][g] for h in range(halves)], axis=1) for g in groups]
    ys = [_dot(xg[g], mi_scr[g]) for g in groups]
    ss = [_dot(xg[g], min_ref[g]) for g in groups]
    ss = _chunk_scan(ss, sc_ref, sf_ref, s_scr, c_scr, chunks_per_seq=chunks_per_seq)
    for g in groups:
        s_prev = jnp.where(n_idx >= 1, pltpu.roll(ss[g], 1, axis=0), 0.0)
        ys[g] = jax.nn.gelu(ys[g] + _dot(s_prev.astype(BF16), mo_scr[g])).astype(o_ref.dtype)
    for h in range(halves):
        out = _piece_transpose([ys[gi][:, h * LANES:(h + 1) * LANES] for gi in range(S5_GB)])
        for i in range(S5_GB):
            t = S5_GB * h + i
            o_ref[0, :, t * LANES:(t + 1) * LANES] = out[i]


def _s5_core(x_tiles, bb, ct, pt, rep, skip, m_in, sc, sf, *, chunks_per_seq):
    tiles, rows, width = x_tiles.shape
    return pl.pallas_call(
        functools.partial(_s5_kernel, chunks_per_seq=chunks_per_seq),
        grid=(tiles,),
        in_specs=[
            pl.BlockSpec((1, rows, width), lambda i: (i, 0, 0)),
            pl.BlockSpec((S5_GB, S5_GROUP, S5_RI), lambda i: (i, 0, 0)),
            pl.BlockSpec((2, S5_GB, S5_STATE, 2 * S5_GROUP), lambda i: (0, i, 0, 0)),
            pl.BlockSpec((2, S5_GB, S5_STATE, 2 * S5_GROUP), lambda i: (0, i, 0, 0)),
            pl.BlockSpec((3, 2 * S5_GROUP, S5_TC), lambda i: (0, 0, 0)),
            pl.BlockSpec((S5_GB, S5_GROUP, S5_TC), lambda i: (i, 0, 0)),
            pl.BlockSpec((S5_GB, S5_TC, S5_RI), lambda i: (i, 0, 0)),
            pl.BlockSpec((S5_GB, 2 * SUBLANES, S5_RI), lambda i: (i, 0, 0)),
            pl.BlockSpec((S5_GB, 2 * SUBLANES, S5_RI), lambda i: (i, 0, 0)),
        ],
        out_specs=pl.BlockSpec((1, rows, width), lambda i: (i, 0, 0)),
        out_shape=jax.ShapeDtypeStruct((tiles, rows, width), BF16),
        scratch_shapes=[pltpu.VMEM((S5_GB, S5_TC, S5_TC), BF16),
                        pltpu.VMEM((S5_GB, S5_RI, S5_TC), BF16),
                        pltpu.VMEM((S5_GB, rows, S5_RI), F32),
                        pltpu.VMEM((S5_GB, 2, rows // SUBLANES, S5_RI), F32)],
        compiler_params=_params("parallel"),
        name="s5_core",
    )(x_tiles, bb, ct, pt, rep, skip, m_in, sc, sf)


def _s5_tables(a_re, a_im, log_dt, b_re, b_im, c_re, c_im, d):
    a_re, a_im = a_re.astype(F32), a_im.astype(F32)
    dt = jnp.exp(log_dt.astype(F32))[:, None]
    mag = jnp.exp(a_re * dt)
    lb_re, lb_im = mag * jnp.cos(a_im * dt), mag * jnp.sin(a_im * dt)
    den = a_re * a_re + a_im * a_im
    q_re = ((lb_re - 1.0) * a_re + lb_im * a_im) / den
    q_im = (lb_im * a_re - (lb_re - 1.0) * a_im) / den
    b_re = b_re.astype(F32).transpose(0, 2, 1)
    b_im = b_im.astype(F32).transpose(0, 2, 1)
    bb_re = q_re[:, None, :] * b_re - q_im[:, None, :] * b_im
    bb_im = q_re[:, None, :] * b_im + q_im[:, None, :] * b_re
    ks = jnp.arange(S5_T + 1, dtype=F32)[:, None, None]
    pmag = jnp.exp(ks * (a_re * dt))
    pk_re = pmag * jnp.cos(ks * (a_im * dt))
    pk_im = pmag * jnp.sin(ks * (a_im * dt))
    bb = jnp.concatenate([bb_re, bb_im], axis=2)
    pr_re, pr_im = pk_re[S5_T - 1::-1], pk_im[S5_T - 1::-1]
    e_re = pr_re[:, :, None, :] * bb_re - pr_im[:, :, None, :] * bb_im
    e_im = pr_re[:, :, None, :] * bb_im + pr_im[:, :, None, :] * bb_re
    m_in = jnp.concatenate([e_re, e_im], axis=3)
    m_in = m_in.transpose(1, 0, 2, 3).reshape(S5_GROUPS, S5_TC, S5_RI).astype(BF16)
    pad32 = lambda a: jnp.pad(a, ((0, 0), (0, 0), (0, 2 * S5_GROUP - a.shape[2])))
    ct = jnp.stack([pad32(c_re.astype(F32).transpose(0, 2, 1)), pad32(c_im.astype(F32).transpose(0, 2, 1))])
    pt = jnp.stack([pad32(pk_re.transpose(1, 2, 0)), pad32(pk_im.transpose(1, 2, 0))])
    lane = np.arange(S5_TC)[None, :]
    row = np.arange(2 * S5_GROUP)[:, None]
    rep = jnp.asarray(np.stack([lane % S5_GROUP == row, lane // S5_GROUP == row,
                                lane // S5_GROUP + 1 == row]), BF16)
    skip = d.astype(F32).reshape(S5_GROUPS, 1, S5_GROUP) * jnp.eye(S5_GROUP, dtype=F32)
    skip = jnp.pad(skip, ((0, 0), (0, 0), (0, S5_TC - S5_GROUP)))
    f_re, f_im = pk_re[S5_T], pk_im[S5_T]
    rows_a, rows_b = [], []
    for _ in range(S5_SCAN_STEPS):
        rows_a.append(jnp.concatenate([f_re, f_re], axis=1))
        rows_b.append(jnp.concatenate([-f_im, f_im], axis=1))
        f_re, f_im = f_re * f_re - f_im * f_im, 2.0 * f_re * f_im
    zero = jnp.zeros_like(rows_a[0])
    sc = jnp.stack(rows_a + [zero] + rows_b + [zero], axis=1)
    f_re, f_im = pk_re[S5_T], pk_im[S5_T]
    rows_a, rows_b = [], []
    for _ in range(SUBLANES):
        rows_a.append(jnp.concatenate([f_re, f_re], axis=1))
        rows_b.append(jnp.concatenate([-f_im, f_im], axis=1))
        f_re, f_im = (f_re * pk_re[S5_T] - f_im * pk_im[S5_T], f_re * pk_im[S5_T] + f_im * pk_re[S5_T])
    sf = jnp.stack(rows_a + rows_b, axis=1)
    return bb, ct, pt, rep, skip, m_in, sc, sf


def kernel(x, p, norm_mix, norm_ffn, norm_ple, norm_final, gm_w_in, gm_ln_g, gm_ln_b, gm_w_s, gm_b_s, gm_w_out, s5_w_in, s5_a_re, s5_a_im, s5_log_dt, s5_b_re, s5_b_im, s5_c_re, s5_c_im, s5_d, s5_w_out, ffn_w1, ffn_w3, ffn_w2, ple_w_gate, ple_w_proj):
    bsz, seq, d = x.shape
    depth = p.shape[0]
    m = bsz * seq
    xs = x.reshape(m, d)
    ps = p.reshape(depth, m, PLE_DIM)
    u = s5_out_b16 = None
    for i in range(depth):
        j = i // 2
        if i % 2 == 0:
            gm_in_b16, = _cast_call([(gm_w_in, j)])
            z, gm_out_b16 = _gm_in(xs, norm_mix[i], gm_in_b16, [(gm_w_out, j)], tm=512)
            xs, hn = _gm_mix(z, gm_ln_g, gm_ln_b, gm_w_s, gm_b_s, j, gm_out_b16, xs, norm_ffn[i], tm=512)
        else:
            tables = _s5_tables(s5_a_re[j], s5_a_im[j], s5_log_dt[j], s5_b_re[j], s5_b_im[j],
                                s5_c_re[j], s5_c_im[j], s5_d[j])
            y = _s5_core(u, *tables, chunks_per_seq=seq // S5_T)
            xs, hn = _s5_out(y, s5_out_b16, xs, norm_ffn[i], tm=512)
        a, w2_b16, wg_b16, wp_b16 = _ffn_up(
            hn, ffn_w1, ffn_w3, i, [(ffn_w2, i), (ple_w_gate, i), (ple_w_proj, i)], tm=2048, tf=512)
        if i + 1 < depth and (i + 1) % 2 == 1:
            jn = (i + 1) // 2
            xs, s5_in_b16 = _ffn_down(a, w2_b16, xs, [(s5_w_in, jn)], tm=256)
            xs, u, s5_out_b16 = _ple(xs, norm_ple[i], ps, i, wg_b16, wp_b16, [(s5_w_out, jn)], tm=512,
                                     s5_in=(norm_mix[i + 1], s5_in_b16))
        else:
            xs, = _ffn_down(a, w2_b16, xs, [], tm=256)
            xs, = _ple(xs, norm_ple[i], ps, i, wg_b16, wp_b16, [], tm=512,
                       final_gain=norm_final if i == depth - 1 else None)
    return xs.reshape(bsz, seq, d)
```

```python
import functools

import jax
import jax.numpy as jnp
import numpy as np
from jax.experimental import pallas as pl
from jax.experimental.pallas import tpu as pltpu

F32 = jnp.float32
BF16 = jnp.bfloat16

D_MODEL = 2048
PLE_DIM = 256
EPS = 1e-6
LANES = 128
SUBLANES = 8
BF16_ROWS = 16

GM_CHUNK = 128
GM_HEAD_DIM = 128
GM_HEADS = D_MODEL // GM_HEAD_DIM

S5_GROUP = 16
S5_GROUPS = D_MODEL // S5_GROUP
S5_STATE = 64
S5_T = 16
S5_TC = S5_T * S5_GROUP
S5_RI = 2 * S5_STATE
S5_GB = LANES // S5_GROUP
S5_TILES = D_MODEL // LANES
S5_SCAN_STEPS = 7

W_CHUNKS = 16
SUB_ROWS = 256
SUB_COLS = 512
VMEM_LIMIT = 56 * 1024 * 1024


def _params(*sem):
    return pltpu.CompilerParams(dimension_semantics=sem, vmem_limit_bytes=VMEM_LIMIT)


def _rms(xf, gain):
    ms = jnp.mean(xf * xf, axis=-1, keepdims=True)
    return xf * jax.lax.rsqrt(ms + EPS) * gain


def _dot(a, b):
    return jnp.dot(a, b, preferred_element_type=F32)


def _sub_blocks(tm):
    sub = min(tm, SUB_ROWS)
    return [slice(r, r + sub) for r in range(0, tm, sub)]


def _col_blocks(n):
    return [slice(c, c + SUB_COLS) for c in range(0, n, SUB_COLS)]


def _cast_rows(k, steps):
    rows = BF16_ROWS
    while k % rows or k // rows > steps:
        rows += BF16_ROWS
    return rows


def _cast_specs(casts, steps, step_of):
    ins, outs, shapes = [], [], []
    for arr, layer in casts:
        _, k, n = arr.shape
        rows = _cast_rows(k, steps)
        last = k // rows - 1
        ins.append(pl.BlockSpec(
            (None, rows, n), lambda *g, layer=layer, last=last: (layer, jnp.minimum(step_of(*g), last), 0)))
        outs.append(pl.BlockSpec((rows, n), lambda *g, last=last: (jnp.minimum(step_of(*g), last), 0)))
        shapes.append(jax.ShapeDtypeStruct((k, n), BF16))
    return ins, outs, shapes


def _run_casts(ci_refs, co_refs):
    for ci, co in zip(ci_refs, co_refs):
        co[...] = ci[...].astype(BF16)


def _cast_kernel(*refs):
    half = len(refs) // 2
    _run_casts(refs[:half], refs[half:])


def _cast_call(casts):
    cast_in, cast_out, cast_shape = _cast_specs(casts, W_CHUNKS, lambda s: s)
    return pl.pallas_call(
        _cast_kernel,
        grid=(W_CHUNKS,),
        in_specs=cast_in,
        out_specs=cast_out,
        out_shape=cast_shape,
        compiler_params=_params("arbitrary"),
        name="cast_weights",
    )(*[arr for arr, _ in casts])


def _resident_call(body, *, name, m, tm, weights, row_ins, consts, outs, casts=(), scratch=()):
    nw, nk, nr, nc, no = len(weights), len(casts), len(row_ins), len(consts), len(outs)
    steps = m // tm

    def kern(*refs):
        pos = 0

        def take(n):
            nonlocal pos
            pos += n
            return refs[pos - n:pos]

        w_refs, ci_refs, r_refs, c_refs = take(nw), take(nk), take(nr), take(nc)
        o_refs, co_refs, s_refs = take(no), take(nk), take(len(scratch))
        _run_casts(ci_refs, co_refs)
        body(w_refs, r_refs, c_refs, o_refs, s_refs)

    in_specs, args = [], []
    for w in weights:
        arr, ncols, col = w if isinstance(w, tuple) else (w, w.shape[1], 0)
        in_specs.append(pl.BlockSpec((arr.shape[0], ncols), lambda s, col=col: (0, col),
                                     pipeline_mode=pl.Buffered(1)))
        args.append(arr)
    cast_in, cast_out, cast_shape = _cast_specs(casts, steps, lambda s: s)
    in_specs += cast_in
    args += [arr for arr, _ in casts]
    for arr, block, index_fn in row_ins:
        in_specs.append(pl.BlockSpec(block, index_fn))
        args.append(arr)
    for arr in consts:
        in_specs.append(pl.BlockSpec((1, arr.shape[1]), lambda s: (0, 0)))
        args.append(arr)
    out_specs = [pl.BlockSpec(block, index_fn) for _, _, block, index_fn in outs]
    out_shape = [jax.ShapeDtypeStruct(shape, dtype) for shape, dtype, _, _ in outs]
    return pl.pallas_call(
        kern,
        grid=(steps,),
        in_specs=in_specs,
        out_specs=out_specs + cast_out,
        out_shape=out_shape + cast_shape,
        scratch_shapes=list(scratch),
        compiler_params=_params("arbitrary"),
        name=name,
    )(*args)


def _rows(tm, n):
    return (tm, n), (lambda i: (i, 0))


def _gm_in(x, gain, w_in_b16, casts, *, tm):
    m, k = x.shape
    n = w_in_b16.shape[1] // 2

    def body(wb, r, c, o, scr):
        for rs in _sub_blocks(tm):
            h = _rms(r[0][rs, :], c[0][...]).astype(BF16)
            for cs in _col_blocks(n):
                o[0][rs, cs] = jax.nn.gelu(_dot(h, wb[0][:, cs])).astype(BF16)

    blk_x, idx = _rows(tm, k)
    blk_o, _ = _rows(tm, n)
    return _resident_call(
        body, name="gm_in", m=m, tm=tm, weights=[(w_in_b16, n, 0)], casts=casts,
        row_ins=[(x, blk_x, idx)], consts=[gain.reshape(1, k)],
        outs=[((m, n), BF16, blk_o, idx)])


def _gm_mix(u, x, gain_mix, w_in_b16, ln_g, ln_b, w_s, b_s, layer, w_out_b16, gain, *, tm):
    m, n = x.shape
    bs = jnp.broadcast_to(b_s[layer][:, :, None], (GM_HEADS, GM_CHUNK, GM_HEAD_DIM)).astype(F32)

    def body(wb, r, c, o, scr):
        u_ref, x_ref, ws_ref, bs_ref = r
        row = jax.lax.broadcasted_iota(jnp.int32, (GM_CHUNK, GM_CHUNK), 0)
        col = jax.lax.broadcasted_iota(jnp.int32, (GM_CHUNK, GM_CHUNK), 1)
        wm = [jnp.where(row >= col, ws_ref[h], 0.0).astype(BF16) for h in range(GM_HEADS)]
        for rs in _sub_blocks(tm):
            xf = x_ref[rs, :]
            hx = _rms(xf, c[3][...]).astype(BF16)
            v_all = jax.nn.gelu(_dot(hx, wb[0][...]))
            for c0 in range(0, rs.stop - rs.start, GM_CHUNK):
                rows = slice(rs.start + c0, rs.start + c0 + GM_CHUNK)
                v = v_all[c0:c0 + GM_CHUNK]
                mu = jnp.mean(v, axis=-1, keepdims=True)
                vc = v - mu
                var = jnp.mean(vc * vc, axis=-1, keepdims=True)
                vn = (vc * jax.lax.rsqrt(var + EPS) * c[1][...] + c[2][...]).astype(BF16)
                for h in range(GM_HEADS):
                    cols = slice(h * GM_HEAD_DIM, (h + 1) * GM_HEAD_DIM)
                    sv = _dot(wm[h], vn[:, cols]) + bs_ref[h]
                    scr[0][rows, cols] = (u_ref[rows, cols].astype(F32) * sv).astype(BF16)
            xn = xf + _dot(scr[0][rs, :], wb[1][...])
            o[0][rs, :] = xn
            o[1][rs, :] = _rms(xn, c[0][...]).astype(BF16)

    blk_x, idx = _rows(tm, n)
    return _resident_call(
        body, name="gm_mix", m=m, tm=tm, weights=[(w_in_b16, n, 1), w_out_b16],
        row_ins=[(u, blk_x, idx), (x, blk_x, idx),
                 (w_s, (None, GM_HEADS, GM_CHUNK, GM_CHUNK), lambda i: (layer, 0, 0, 0)),
                 (bs, (GM_HEADS, GM_CHUNK, GM_HEAD_DIM), lambda i: (0, 0, 0))],
        consts=[gain.reshape(1, n), ln_g[layer].reshape(1, n), ln_b[layer].reshape(1, n),
                gain_mix.reshape(1, n)],
        outs=[((m, n), F32, blk_x, idx), ((m, n), BF16, blk_x, idx)],
        scratch=[pltpu.VMEM((tm, n), BF16)])


def _ffn_down(a, w2b16, res, casts, *, tm):
    m, n = res.shape

    def body(wb, r, c, o, scr):
        for rs in _sub_blocks(tm):
            o[0][rs, :] = r[1][rs, :] + _dot(r[0][rs, :], wb[0][...])

    blk_a, idx = _rows(tm, a.shape[1])
    blk_x, _ = _rows(tm, n)
    return _resident_call(
        body, name="ffn_down", m=m, tm=tm, weights=[w2b16], casts=casts,
        row_ins=[(a, blk_a, idx), (res, blk_x, idx)], consts=[],
        outs=[((m, n), F32, blk_x, idx)])


def _ple(x, gain, p, layer, wgb16, wpb16, casts, *, tm, final_gain=None, s5_in=None):
    assert final_gain is None or s5_in is None
    m, n = x.shape
    tc = tm // S5_T

    def body(wb, r, c, o, scr):
        for rs in _sub_blocks(tm):
            xf = r[0][rs, :]
            h = _rms(xf, c[0][...]).astype(BF16)
            pb = r[1][rs, :].astype(BF16)
            xn = jnp.concatenate(
                [xf[:, cs] + jax.nn.sigmoid(_dot(h, wb[0][:, cs])) * _dot(pb, wb[1][:, cs])
                 for cs in _col_blocks(n)], axis=1)
            if final_gain is not None:
                xn = _rms(xn, c[1][...])
            o[0][rs, :] = xn
            if s5_in is not None:
                u = _dot(_rms(xn, c[1][...]).astype(BF16), wb[2][...])
                for l in range(S5_TILES):
                    scr[0][l, rs, :] = u[:, l * LANES:(l + 1) * LANES]
                sub = rs.stop - rs.start
                cs = slice(rs.start // S5_T, rs.stop // S5_T)
                for t in range(S5_T):
                    for l in range(S5_TILES):
                        o[1][l, cs, t * LANES:(t + 1) * LANES] = (
                            scr[0][l, pl.ds(rs.start + t, sub // S5_T, stride=S5_T), :].astype(BF16))

    blk_x, idx = _rows(tm, n)
    blk_p, _ = _rows(tm, PLE_DIM)
    weights, consts = [wgb16, wpb16], [gain.reshape(1, n)]
    outs, scratch = [((m, n), F32, blk_x, idx)], []
    if final_gain is not None:
        consts.append(final_gain.reshape(1, n))
    if s5_in is not None:
        consts.append(s5_in[0].reshape(1, n))
        weights.append(s5_in[1])
        outs.append(((S5_TILES, m // S5_T, S5_T * LANES), BF16, (S5_TILES, tc, S5_T * LANES),
                     lambda i: (0, i, 0)))
        scratch.append(pltpu.VMEM((S5_TILES, tm, LANES), F32))
    return _resident_call(
        body, name="ple", m=m, tm=tm, weights=weights, casts=casts,
        row_ins=[(x, blk_x, idx), (p, (None,) + blk_p, lambda i: (layer, i, 0))], consts=consts,
        outs=outs, scratch=scratch)


def _s5_out(y_tiles, wb16, res, gain, *, tm):
    m, n = res.shape
    tc = tm // S5_T

    def body(wb, r, c, o, scr):
        for rs in _sub_blocks(tm):
            sub = rs.stop - rs.start
            cs = slice(rs.start // S5_T, rs.stop // S5_T)
            for t in range(S5_T):
                for l in range(S5_TILES):
                    scr[0][l, pl.ds(rs.start + t, sub // S5_T, stride=S5_T), :] = (
                        r[0][l, cs, t * LANES:(t + 1) * LANES].astype(F32))
            y = jnp.concatenate([scr[0][l, rs, :] for l in range(S5_TILES)], axis=1).astype(BF16)
            vg = _dot(y, wb[0][...])
            xn = r[1][rs, :] + vg[:, :n] * jax.nn.sigmoid(vg[:, n:])
            o[0][rs, :] = xn
            o[1][rs, :] = _rms(xn, c[0][...]).astype(BF16)

    blk_x, idx = _rows(tm, n)
    return _resident_call(
        body, name="s5_out", m=m, tm=tm, weights=[wb16],
        row_ins=[(y_tiles, (S5_TILES, tc, S5_T * LANES), lambda i: (0, i, 0)), (res, blk_x, idx)],
        consts=[gain.reshape(1, n)],
        outs=[((m, n), F32, blk_x, idx), ((m, n), BF16, blk_x, idx)],
        scratch=[pltpu.VMEM((S5_TILES, tm, LANES), F32)])


def _ffn_up_kernel(*refs, tm, n_casts):
    h_ref, w1_ref, w3_ref = refs[:3]
    ci_refs = refs[3:3 + n_casts]
    o_ref = refs[3 + n_casts]
    co_refs = refs[4 + n_casts:4 + 2 * n_casts]
    w1_scr, w3_scr = refs[4 + 2 * n_casts:]
    _run_casts(ci_refs, co_refs)

    @pl.when(pl.program_id(1) == 0)
    def _():
        w1_scr[...] = w1_ref[...].astype(BF16)
        w3_scr[...] = w3_ref[...].astype(BF16)

    for rs in _sub_blocks(tm):
        h = h_ref[rs, :]
        o_ref[rs, :] = (jax.nn.silu(_dot(h, w1_scr[...])) * _dot(h, w3_scr[...])).astype(o_ref.dtype)


def _ffn_up(h, w1, w3, layer, casts, *, tm, tf):
    m, d = h.shape
    ff = w1.shape[2]
    ni = m // tm
    cast_in, cast_out, cast_shape = _cast_specs(casts, (ff // tf) * ni, lambda f, i: f * ni + i)
    return pl.pallas_call(
        functools.partial(_ffn_up_kernel, tm=tm, n_casts=len(casts)),
        grid=(ff // tf, ni),
        in_specs=[
            pl.BlockSpec((tm, d), lambda f, i: (i, 0)),
            pl.BlockSpec((None, d, tf), lambda f, i: (layer, 0, f)),
            pl.BlockSpec((None, d, tf), lambda f, i: (layer, 0, f)),
        ] + cast_in,
        out_specs=[pl.BlockSpec((tm, tf), lambda f, i: (i, f))] + cast_out,
        out_shape=[jax.ShapeDtypeStruct((m, ff), BF16)] + cast_shape,
        scratch_shapes=[pltpu.VMEM((d, tf), BF16), pltpu.VMEM((d, tf), BF16)],
        compiler_params=_params("arbitrary", "arbitrary"),
        name="ffn_up",
    )(h, w1, w3, *[arr for arr, _ in casts])


def _piece_transpose(v):
    piece = jax.lax.broadcasted_iota(jnp.int32, v[0].shape, 1) // S5_GROUP
    for delta in (4, 2, 1):
        keep = (piece & delta) == 0
        shift = delta * S5_GROUP
        new = list(v)
        for i in range(S5_GB):
            if i & delta == 0:
                a, b = v[i], v[i + delta]
                new[i] = jnp.where(keep, a, pltpu.roll(b, shift, axis=1))
                new[i + delta] = jnp.where(keep, pltpu.roll(a, LANES - shift, axis=1), b)
        v = new
    return v


def _toeplitz_rows(strip, s):
    lo, hi = strip[:, :LANES], strip[:, LANES:]
    lane = jax.lax.broadcasted_iota(jnp.int32, lo.shape, 1)
    shift = (s * S5_GROUP) % LANES
    if s == 0:
        return strip
    if s * S5_GROUP < LANES:
        lo_r, hi_r = pltpu.roll(lo, shift, axis=1), pltpu.roll(hi, shift, axis=1)
        return jnp.concatenate([jnp.where(lane >= shift, lo_r, 0.0),
                                jnp.where(lane >= shift, hi_r, lo_r)], axis=1)
    lo_r = lo if shift == 0 else pltpu.roll(lo, shift, axis=1)
    return jnp.concatenate([jnp.zeros_like(lo), jnp.where(lane >= shift, lo_r, 0.0)], axis=1)


def _cmul_add(s, fa, fb, x):
    return s + fa * x + fb * pltpu.roll(x, S5_STATE, axis=1)


def _chunk_scan(ss, sc_ref, sf_ref, s_scr, c_scr, *, chunks_per_seq):
    groups = range(len(ss))
    rows = ss[0].shape[0]
    tiles = rows // SUBLANES
    r_idx = jax.lax.broadcasted_iota(jnp.int32, ss[0].shape, 0) % SUBLANES
    for k in range(3):
        d = 1 << k
        ss = [_cmul_add(ss[g], sc_ref[g, k:k + 1, :], sc_ref[g, 8 + k:9 + k, :],
                        jnp.where(r_idx >= d, pltpu.roll(ss[g], d, axis=0), 0.0)) for g in groups]
    for g in groups:
        s_scr[g] = ss[g]
    ts = [s_scr[g, pl.ds(SUBLANES - 1, tiles, stride=SUBLANES), :] for g in groups]
    j_idx = jax.lax.broadcasted_iota(jnp.int32, ts[0].shape, 0) % (chunks_per_seq // SUBLANES)
    for k in range(3, S5_SCAN_STEPS):
        d = 1 << (k - 3)
        ts = [_cmul_add(ts[g], sc_ref[g, k:k + 1, :], sc_ref[g, 8 + k:9 + k, :],
                        jnp.where(j_idx >= d, pltpu.roll(ts[g], d, axis=0), 0.0)) for g in groups]
    for g in groups:
        carry = jnp.where(j_idx >= 1, pltpu.roll(ts[g], 1, axis=0), 0.0)
        c_scr[g, 0] = carry
        c_scr[g, 1] = pltpu.roll(carry, S5_STATE, axis=1)
    outs = []
    for g in groups:
        fa, fb = sf_ref[g, :SUBLANES, :], sf_ref[g, SUBLANES:, :]
        out = []
        for j in range(tiles):
            cb = jnp.broadcast_to(c_scr[g, 0, j:j + 1, :], (SUBLANES, S5_RI))
            cs = jnp.broadcast_to(c_scr[g, 1, j:j + 1, :], (SUBLANES, S5_RI))
            out.append(ss[g][j * SUBLANES:(j + 1) * SUBLANES] + fa * cb + fb * cs)
        outs.append(jnp.concatenate(out, axis=0))
    return outs


def _spread(a, rep):
    hi = a.astype(BF16)
    lo = (a - hi.astype(F32)).astype(BF16)
    return _dot(hi, rep) + _dot(lo, rep)


def _s5_kernel(x_ref, bb_ref, ct_ref, pt_ref, rep_ref, skip_ref, min_ref, sc_ref, sf_ref, o_ref,
               mi_scr, mo_scr, s_scr, c_scr, *, chunks_per_seq):
    rows = x_ref.shape[1]
    n_idx = jax.lax.broadcasted_iota(jnp.int32, (rows, S5_RI), 0) % chunks_per_seq
    halves = S5_T // S5_GB
    gp = S5_GB * S5_STATE
    cl_re = _spread(ct_ref[0].reshape(gp, 2 * S5_GROUP), rep_ref[0])
    cl_im = _spread(ct_ref[1].reshape(gp, 2 * S5_GROUP), rep_ref[0])
    pt_re, pt_im = pt_ref[0].reshape(gp, 2 * S5_GROUP), pt_ref[1].reshape(gp, 2 * S5_GROUP)
    readout = []
    for k0 in range(2):
        pl_re, pl_im = _spread(pt_re, rep_ref[1 + k0]), _spread(pt_im, rep_ref[1 + k0])
        readout.append((cl_re * pl_re - cl_im * pl_im, -(cl_re * pl_im + cl_im * pl_re)))
    for gi in range(S5_GB):
        ps = slice(gi * S5_STATE, (gi + 1) * S5_STATE)
        tap = jnp.concatenate([readout[0][0][ps], readout[0][1][ps]], axis=0)
        mo_scr[gi] = jnp.concatenate([readout[1][0][ps], readout[1][1][ps]], axis=0).astype(BF16)
        strip = jnp.dot(bb_ref[gi], tap, preferred_element_type=F32,
                        precision=jax.lax.Precision.HIGHEST) + skip_ref[gi]
        for s in range(S5_T):
            mi_scr[gi, s * S5_GROUP:(s + 1) * S5_GROUP, :] = _toeplitz_rows(strip, s).astype(BF16)
    xin = [_piece_transpose([x_ref[0, :, (S5_GB * h + i) * LANES:(S5_GB * h + i + 1) * LANES]
                             for i in range(S5_GB)]) for h in range(halves)]
    groups = range(S5_GB)
    xg = [jnp.concatenate([xin[h][g] for h in range(halves)], axis=1) for g in groups]
    ys = [_dot(xg[g], mi_scr[g]) for g in groups]
    ss = [_dot(xg[g], min_ref[g]) for g in groups]
    ss = _chunk_scan(ss, sc_ref, sf_ref, s_scr, c_scr, chunks_per_seq=chunks_per_seq)
    for g in groups:
        s_prev = jnp.where(n_idx >= 1, pltpu.roll(ss[g], 1, axis=0), 0.0)
        ys[g] = jax.nn.gelu(ys[g] + _dot(s_prev.astype(BF16), mo_scr[g])).astype(o_ref.dtype)
    for h in range(halves):
        out = _piece_transpose([ys[gi][:, h * LANES:(h + 1) * LANES] for gi in range(S5_GB)])
        for i in range(S5_GB):
            t = S5_GB * h + i
            o_ref[0, :, t * LANES:(t + 1) * LANES] = out[i]


def _s5_core(x_tiles, bb, ct, pt, rep, skip, m_in, sc, sf, *, chunks_per_seq):
    tiles, rows, width = x_tiles.shape
    return pl.pallas_call(
        functools.partial(_s5_kernel, chunks_per_seq=chunks_per_seq),
        grid=(tiles,),
        in_specs=[
            pl.BlockSpec((1, rows, width), lambda i: (i, 0, 0)),
            pl.BlockSpec((S5_GB, S5_GROUP, S5_RI), lambda i: (i, 0, 0)),
            pl.BlockSpec((2, S5_GB, S5_STATE, 2 * S5_GROUP), lambda i: (0, i, 0, 0)),
            pl.BlockSpec((2, S5_GB, S5_STATE, 2 * S5_GROUP), lambda i: (0, i, 0, 0)),
            pl.BlockSpec((3, 2 * S5_GROUP, S5_TC), lambda i: (0, 0, 0)),
            pl.BlockSpec((S5_GB, S5_GROUP, S5_TC), lambda i: (i, 0, 0)),
            pl.BlockSpec((S5_GB, S5_TC, S5_RI), lambda i: (i, 0, 0)),
            pl.BlockSpec((S5_GB, 2 * SUBLANES, S5_RI), lambda i: (i, 0, 0)),
            pl.BlockSpec((S5_GB, 2 * SUBLANES, S5_RI), lambda i: (i, 0, 0)),
        ],
        out_specs=pl.BlockSpec((1, rows, width), lambda i: (i, 0, 0)),
        out_shape=jax.ShapeDtypeStruct((tiles, rows, width), BF16),
        scratch_shapes=[pltpu.VMEM((S5_GB, S5_TC, S5_TC), BF16),
                        pltpu.VMEM((S5_GB, S5_RI, S5_TC), BF16),
                        pltpu.VMEM((S5_GB, rows, S5_RI), F32),
                        pltpu.VMEM((S5_GB, 2, rows // SUBLANES, S5_RI), F32)],
        compiler_params=_params("parallel"),
        name="s5_core",
    )(x_tiles, bb, ct, pt, rep, skip, m_in, sc, sf)


def _s5_tables(a_re, a_im, log_dt, b_re, b_im, c_re, c_im, d):
    a_re, a_im = a_re.astype(F32), a_im.astype(F32)
    dt = jnp.exp(log_dt.astype(F32))[:, None]
    mag = jnp.exp(a_re * dt)
    lb_re, lb_im = mag * jnp.cos(a_im * dt), mag * jnp.sin(a_im * dt)
    den = a_re * a_re + a_im * a_im
    q_re = ((lb_re - 1.0) * a_re + lb_im * a_im) / den
    q_im = (lb_im * a_re - (lb_re - 1.0) * a_im) / den
    b_re = b_re.astype(F32).transpose(0, 2, 1)
    b_im = b_im.astype(F32).transpose(0, 2, 1)
    bb_re = q_re[:, None, :] * b_re - q_im[:, None, :] * b_im
    bb_im = q_re[:, None, :] * b_im + q_im[:, None, :] * b_re
    ks = jnp.arange(S5_T + 1, dtype=F32)[:, None, None]
    pmag = jnp.exp(ks * (a_re * dt))
    pk_re = pmag * jnp.cos(ks * (a_im * dt))
    pk_im = pmag * jnp.sin(ks * (a_im * dt))
    bb = jnp.concatenate([bb_re, bb_im], axis=2)
    pr_re, pr_im = pk_re[S5_T - 1::-1], pk_im[S5_T - 1::-1]
    e_re = pr_re[:, :, None, :] * bb_re - pr_im[:, :, None, :] * bb_im
    e_im = pr_re[:, :, None, :] * bb_im + pr_im[:, :, None, :] * bb_re
    m_in = jnp.concatenate([e_re, e_im], axis=3)
    m_in = m_in.transpose(1, 0, 2, 3).reshape(S5_GROUPS, S5_TC, S5_RI).astype(BF16)
    pad32 = lambda a: jnp.pad(a, ((0, 0), (0, 0), (0, 2 * S5_GROUP - a.shape[2])))
    ct = jnp.stack([pad32(c_re.astype(F32).transpose(0, 2, 1)), pad32(c_im.astype(F32).transpose(0, 2, 1))])
    pt = jnp.stack([pad32(pk_re.transpose(1, 2, 0)), pad32(pk_im.transpose(1, 2, 0))])
    lane = np.arange(S5_TC)[None, :]
    row = np.arange(2 * S5_GROUP)[:, None]
    rep = jnp.asarray(np.stack([lane % S5_GROUP == row, lane // S5_GROUP == row,
                                lane // S5_GROUP + 1 == row]), BF16)
    skip = d.astype(F32).reshape(S5_GROUPS, 1, S5_GROUP) * jnp.eye(S5_GROUP, dtype=F32)
    skip = jnp.pad(skip, ((0, 0), (0, 0), (0, S5_TC - S5_GROUP)))
    f_re, f_im = pk_re[S5_T], pk_im[S5_T]
    rows_a, rows_b = [], []
    for _ in range(S5_SCAN_STEPS):
        rows_a.append(jnp.concatenate([f_re, f_re], axis=1))
        rows_b.append(jnp.concatenate([-f_im, f_im], axis=1))
        f_re, f_im = f_re * f_re - f_im * f_im, 2.0 * f_re * f_im
    zero = jnp.zeros_like(rows_a[0])
    sc = jnp.stack(rows_a + [zero] + rows_b + [zero], axis=1)
    f_re, f_im = pk_re[S5_T], pk_im[S5_T]
    rows_a, rows_b = [], []
    for _ in range(SUBLANES):
        rows_a.append(jnp.concatenate([f_re, f_re], axis=1))
        rows_b.append(jnp.concatenate([-f_im, f_im], axis=1))
        f_re, f_im = (f_re * pk_re[S5_T] - f_im * pk_im[S5_T], f_re * pk_im[S5_T] + f_im * pk_re[S5_T])
    sf = jnp.stack(rows_a + rows_b, axis=1)
    return bb, ct, pt, rep, skip, m_in, sc, sf


def kernel(x, p, norm_mix, norm_ffn, norm_ple, norm_final, gm_w_in, gm_ln_g, gm_ln_b, gm_w_s, gm_b_s, gm_w_out, s5_w_in, s5_a_re, s5_a_im, s5_log_dt, s5_b_re, s5_b_im, s5_c_re, s5_c_im, s5_d, s5_w_out, ffn_w1, ffn_w3, ffn_w2, ple_w_gate, ple_w_proj):
    bsz, seq, d = x.shape
    depth = p.shape[0]
    m = bsz * seq
    xs = x.reshape(m, d)
    ps = p.reshape(depth, m, PLE_DIM)
    u = s5_out_b16 = None
    for i in range(depth):
        j = i // 2
        if i % 2 == 0:
            gm_in_b16, = _cast_call([(gm_w_in, j)])
            ug, gm_out_b16 = _gm_in(xs, norm_mix[i], gm_in_b16, [(gm_w_out, j)], tm=512)
            xs, hn = _gm_mix(ug, xs, norm_mix[i], gm_in_b16, gm_ln_g, gm_ln_b, gm_w_s, gm_b_s, j, gm_out_b16,
                             norm_ffn[i], tm=512)
        else:
            tables = _s5_tables(s5_a_re[j], s5_a_im[j], s5_log_dt[j], s5_b_re[j], s5_b_im[j],
                                s5_c_re[j], s5_c_im[j], s5_d[j])
            y = _s5_core(u, *tables, chunks_per_seq=seq // S5_T)
            xs, hn = _s5_out(y, s5_out_b16, xs, norm_ffn[i], tm=512)
        a, w2_b16, wg_b16, wp_b16 = _ffn_up(
            hn, ffn_w1, ffn_w3, i, [(ffn_w2, i), (ple_w_gate, i), (ple_w_proj, i)], tm=2048, tf=512)
        if i + 1 < depth and (i + 1) % 2 == 1:
            jn = (i + 1) // 2
            xs, s5_in_b16 = _ffn_down(a, w2_b16, xs, [(s5_w_in, jn)], tm=256)
            xs, u, s5_out_b16 = _ple(xs, norm_ple[i], ps, i, wg_b16, wp_b16, [(s5_w_out, jn)], tm=512,
                                     s5_in=(norm_mix[i + 1], s5_in_b16))
        else:
            xs, = _ffn_down(a, w2_b16, xs, [], tm=256)
            xs, = _ple(xs, norm_ple[i], ps, i, wg_b16, wp_b16, [], tm=512,
                       final_gain=norm_final if i == depth - 1 else None)
    return xs.reshape(bsz, seq, d)
```

```python
import functools

import jax
import jax.numpy as jnp
import numpy as np
from jax.experimental import pallas as pl
from jax.experimental.pallas import tpu as pltpu

F32 = jnp.float32
BF16 = jnp.bfloat16

D_MODEL = 2048
PLE_DIM = 256
EPS = 1e-6
LANES = 128
SUBLANES = 8
BF16_ROWS = 16

GM_CHUNK = 128
GM_HEAD_DIM = 128
GM_HEADS = D_MODEL // GM_HEAD_DIM

S5_GROUP = 16
S5_GROUPS = D_MODEL // S5_GROUP
S5_STATE = 64
S5_T = 16
S5_TC = S5_T * S5_GROUP
S5_RI = 2 * S5_STATE
S5_GB = LANES // S5_GROUP
S5_TILES = D_MODEL // LANES
S5_SCAN_STEPS = 7

W_CHUNKS = 16
SUB_ROWS = 256
SUB_COLS = 512
VMEM_LIMIT = 56 * 1024 * 1024


def _params(*sem):
    return pltpu.CompilerParams(dimension_semantics=sem, vmem_limit_bytes=VMEM_LIMIT)


def _rms(xf, gain):
    ms = jnp.mean(xf * xf, axis=-1, keepdims=True)
    return xf * jax.lax.rsqrt(ms + EPS) * gain


def _dot(a, b):
    return jnp.dot(a, b, preferred_element_type=F32)


def _sub_blocks(tm):
    sub = min(tm, SUB_ROWS)
    return [slice(r, r + sub) for r in range(0, tm, sub)]


def _col_blocks(n):
    return [slice(c, c + SUB_COLS) for c in range(0, n, SUB_COLS)]


def _cast_rows(k, steps):
    rows = BF16_ROWS
    while k % rows or k // rows > steps:
        rows += BF16_ROWS
    return rows


def _cast_specs(casts, steps, step_of):
    ins, outs, shapes = [], [], []
    for arr, layer in casts:
        _, k, n = arr.shape
        rows = _cast_rows(k, steps)
        last = k // rows - 1
        ins.append(pl.BlockSpec(
            (None, rows, n), lambda *g, layer=layer, last=last: (layer, jnp.minimum(step_of(*g), last), 0)))
        outs.append(pl.BlockSpec((rows, n), lambda *g, last=last: (jnp.minimum(step_of(*g), last), 0)))
        shapes.append(jax.ShapeDtypeStruct((k, n), BF16))
    return ins, outs, shapes


def _run_casts(ci_refs, co_refs):
    for ci, co in zip(ci_refs, co_refs):
        co[...] = ci[...].astype(BF16)


def _cast_kernel(*refs):
    half = len(refs) // 2
    _run_casts(refs[:half], refs[half:])


def _cast_call(casts):
    cast_in, cast_out, cast_shape = _cast_specs(casts, W_CHUNKS, lambda s: s)
    return pl.pallas_call(
        _cast_kernel,
        grid=(W_CHUNKS,),
        in_specs=cast_in,
        out_specs=cast_out,
        out_shape=cast_shape,
        compiler_params=_params("arbitrary"),
        name="cast_weights",
    )(*[arr for arr, _ in casts])


def _resident_call(body, *, name, m, tm, weights, row_ins, consts, outs, casts=(), scratch=()):
    nw, nk, nr, nc, no = len(weights), len(casts), len(row_ins), len(consts), len(outs)
    steps = m // tm

    def kern(*refs):
        pos = 0

        def take(n):
            nonlocal pos
            pos += n
            return refs[pos - n:pos]

        w_refs, ci_refs, r_refs, c_refs = take(nw), take(nk), take(nr), take(nc)
        o_refs, co_refs, s_refs = take(no), take(nk), take(len(scratch))
        _run_casts(ci_refs, co_refs)
        body(w_refs, r_refs, c_refs, o_refs, s_refs)

    in_specs, args = [], []
    for w in weights:
        arr, ncols, col = w if isinstance(w, tuple) else (w, w.shape[1], 0)
        in_specs.append(pl.BlockSpec((arr.shape[0], ncols), lambda s, col=col: (0, col),
                                     pipeline_mode=pl.Buffered(1)))
        args.append(arr)
    cast_in, cast_out, cast_shape = _cast_specs(casts, steps, lambda s: s)
    in_specs += cast_in
    args += [arr for arr, _ in casts]
    for arr, block, index_fn in row_ins:
        in_specs.append(pl.BlockSpec(block, index_fn))
        args.append(arr)
    for arr in consts:
        in_specs.append(pl.BlockSpec((1, arr.shape[1]), lambda s: (0, 0)))
        args.append(arr)
    out_specs = [pl.BlockSpec(block, index_fn) for _, _, block, index_fn in outs]
    out_shape = [jax.ShapeDtypeStruct(shape, dtype) for shape, dtype, _, _ in outs]
    return pl.pallas_call(
        kern,
        grid=(steps,),
        in_specs=in_specs,
        out_specs=out_specs + cast_out,
        out_shape=out_shape + cast_shape,
        scratch_shapes=list(scratch),
        compiler_params=_params("arbitrary"),
        name=name,
    )(*args)


def _rows(tm, n):
    return (tm, n), (lambda i: (i, 0))


def _gm_layer(x, gain_mix, w_in_b16, ln_g, ln_b, w_s, b_s, layer, w_out_b16, gain, *, tm):
    m, n = x.shape
    bs = jnp.broadcast_to(b_s[layer][:, :, None], (GM_HEADS, GM_CHUNK, GM_HEAD_DIM)).astype(F32)

    def body(wb, r, c, o, scr):
        x_ref, ws_ref, bs_ref = r
        row = jax.lax.broadcasted_iota(jnp.int32, (GM_CHUNK, GM_CHUNK), 0)
        col = jax.lax.broadcasted_iota(jnp.int32, (GM_CHUNK, GM_CHUNK), 1)
        wm = [jnp.where(row >= col, ws_ref[h], 0.0).astype(BF16) for h in range(GM_HEADS)]
        for rs in _sub_blocks(tm):
            xf = x_ref[rs, :]
            hx = _rms(xf, c[3][...]).astype(BF16)
            v_all = jax.nn.gelu(_dot(hx, wb[0][:, n:]))
            u_all = jax.nn.gelu(_dot(hx, wb[0][:, :n])).astype(BF16)
            for c0 in range(0, rs.stop - rs.start, GM_CHUNK):
                rows = slice(rs.start + c0, rs.start + c0 + GM_CHUNK)
                v = v_all[c0:c0 + GM_CHUNK]
                mu = jnp.mean(v, axis=-1, keepdims=True)
                vc = v - mu
                var = jnp.mean(vc * vc, axis=-1, keepdims=True)
                vn = (vc * jax.lax.rsqrt(var + EPS) * c[1][...] + c[2][...]).astype(BF16)
                for h in range(GM_HEADS):
                    cols = slice(h * GM_HEAD_DIM, (h + 1) * GM_HEAD_DIM)
                    sv = _dot(wm[h], vn[:, cols]) + bs_ref[h]
                    scr[0][rows, cols] = (u_all[c0:c0 + GM_CHUNK, cols].astype(F32) * sv).astype(BF16)
            xn = xf + _dot(scr[0][rs, :], wb[1][...])
            o[0][rs, :] = xn
            o[1][rs, :] = _rms(xn, c[0][...]).astype(BF16)

    blk_x, idx = _rows(tm, n)
    return _resident_call(
        body, name="gm_layer", m=m, tm=tm, weights=[w_in_b16, w_out_b16],
        row_ins=[(x, blk_x, idx),
                 (w_s, (None, GM_HEADS, GM_CHUNK, GM_CHUNK), lambda i: (layer, 0, 0, 0)),
                 (bs, (GM_HEADS, GM_CHUNK, GM_HEAD_DIM), lambda i: (0, 0, 0))],
        consts=[gain.reshape(1, n), ln_g[layer].reshape(1, n), ln_b[layer].reshape(1, n),
                gain_mix.reshape(1, n)],
        outs=[((m, n), F32, blk_x, idx), ((m, n), BF16, blk_x, idx)],
        scratch=[pltpu.VMEM((tm, n), BF16)])


def _gm_in(x, gain, w_in_b16, casts, *, tm):
    m, k = x.shape
    n = w_in_b16.shape[1] // 2

    def body(wb, r, c, o, scr):
        for rs in _sub_blocks(tm):
            h = _rms(r[0][rs, :], c[0][...]).astype(BF16)
            for cs in _col_blocks(n):
                o[0][rs, cs] = jax.nn.gelu(_dot(h, wb[0][:, cs])).astype(BF16)

    blk_x, idx = _rows(tm, k)
    blk_o, _ = _rows(tm, n)
    return _resident_call(
        body, name="gm_in", m=m, tm=tm, weights=[(w_in_b16, n, 0)], casts=casts,
        row_ins=[(x, blk_x, idx)], consts=[gain.reshape(1, k)],
        outs=[((m, n), BF16, blk_o, idx)])


def _gm_mix(u, x, gain_mix, w_in_b16, ln_g, ln_b, w_s, b_s, layer, w_out_b16, gain, *, tm):
    m, n = x.shape
    bs = jnp.broadcast_to(b_s[layer][:, :, None], (GM_HEADS, GM_CHUNK, GM_HEAD_DIM)).astype(F32)

    def body(wb, r, c, o, scr):
        u_ref, x_ref, ws_ref, bs_ref = r
        row = jax.lax.broadcasted_iota(jnp.int32, (GM_CHUNK, GM_CHUNK), 0)
        col = jax.lax.broadcasted_iota(jnp.int32, (GM_CHUNK, GM_CHUNK), 1)
        wm = [jnp.where(row >= col, ws_ref[h], 0.0).astype(BF16) for h in range(GM_HEADS)]
        for rs in _sub_blocks(tm):
            xf = x_ref[rs, :]
            hx = _rms(xf, c[3][...]).astype(BF16)
            v_all = jax.nn.gelu(_dot(hx, wb[0][...]))
            for c0 in range(0, rs.stop - rs.start, GM_CHUNK):
                rows = slice(rs.start + c0, rs.start + c0 + GM_CHUNK)
                v = v_all[c0:c0 + GM_CHUNK]
                mu = jnp.mean(v, axis=-1, keepdims=True)
                vc = v - mu
                var = jnp.mean(vc * vc, axis=-1, keepdims=True)
                vn = (vc * jax.lax.rsqrt(var + EPS) * c[1][...] + c[2][...]).astype(BF16)
                for h in range(GM_HEADS):
                    cols = slice(h * GM_HEAD_DIM, (h + 1) * GM_HEAD_DIM)
                    sv = _dot(wm[h], vn[:, cols]) + bs_ref[h]
                    scr[0][rows, cols] = (u_ref[rows, cols].astype(F32) * sv).astype(BF16)
            xn = xf + _dot(scr[0][rs, :], wb[1][...])
            o[0][rs, :] = xn
            o[1][rs, :] = _rms(xn, c[0][...]).astype(BF16)

    blk_x, idx = _rows(tm, n)
    return _resident_call(
        body, name="gm_mix", m=m, tm=tm, weights=[(w_in_b16, n, 1), w_out_b16],
        row_ins=[(u, blk_x, idx), (x, blk_x, idx),
                 (w_s, (None, GM_HEADS, GM_CHUNK, GM_CHUNK), lambda i: (layer, 0, 0, 0)),
                 (bs, (GM_HEADS, GM_CHUNK, GM_HEAD_DIM), lambda i: (0, 0, 0))],
        consts=[gain.reshape(1, n), ln_g[layer].reshape(1, n), ln_b[layer].reshape(1, n),
                gain_mix.reshape(1, n)],
        outs=[((m, n), F32, blk_x, idx), ((m, n), BF16, blk_x, idx)],
        scratch=[pltpu.VMEM((tm, n), BF16)])


def _ffn_down(a, w2b16, res, casts, *, tm):
    m, n = res.shape

    def body(wb, r, c, o, scr):
        for rs in _sub_blocks(tm):
            o[0][rs, :] = r[1][rs, :] + _dot(r[0][rs, :], wb[0][...])

    blk_a, idx = _rows(tm, a.shape[1])
    blk_x, _ = _rows(tm, n)
    return _resident_call(
        body, name="ffn_down", m=m, tm=tm, weights=[w2b16], casts=casts,
        row_ins=[(a, blk_a, idx), (res, blk_x, idx)], consts=[],
        outs=[((m, n), F32, blk_x, idx)])


def _ple(x, gain, p, layer, wgb16, wpb16, casts, *, tm, final_gain=None, s5_in=None):
    assert final_gain is None or s5_in is None
    m, n = x.shape
    tc = tm // S5_T

    def body(wb, r, c, o, scr):
        for rs in _sub_blocks(tm):
            xf = r[0][rs, :]
            h = _rms(xf, c[0][...]).astype(BF16)
            pb = r[1][rs, :].astype(BF16)
            xn = jnp.concatenate(
                [xf[:, cs] + jax.nn.sigmoid(_dot(h, wb[0][:, cs])) * _dot(pb, wb[1][:, cs])
                 for cs in _col_blocks(n)], axis=1)
            if final_gain is not None:
                xn = _rms(xn, c[1][...])
            o[0][rs, :] = xn
            if s5_in is not None:
                u = _dot(_rms(xn, c[1][...]).astype(BF16), wb[2][...])
                for l in range(S5_TILES):
                    scr[0][l, rs, :] = u[:, l * LANES:(l + 1) * LANES]
                sub = rs.stop - rs.start
                cs = slice(rs.start // S5_T, rs.stop // S5_T)
                for t in range(S5_T):
                    for l in range(S5_TILES):
                        o[1][l, cs, t * LANES:(t + 1) * LANES] = (
                            scr[0][l, pl.ds(rs.start + t, sub // S5_T, stride=S5_T), :].astype(BF16))

    blk_x, idx = _rows(tm, n)
    blk_p, _ = _rows(tm, PLE_DIM)
    weights, consts = [wgb16, wpb16], [gain.reshape(1, n)]
    outs, scratch = [((m, n), F32, blk_x, idx)], []
    if final_gain is not None:
        consts.append(final_gain.reshape(1, n))
    if s5_in is not None:
        consts.append(s5_in[0].reshape(1, n))
        weights.append(s5_in[1])
        outs.append(((S5_TILES, m // S5_T, S5_T * LANES), BF16, (S5_TILES, tc, S5_T * LANES),
                     lambda i: (0, i, 0)))
        scratch.append(pltpu.VMEM((S5_TILES, tm, LANES), F32))
    return _resident_call(
        body, name="ple", m=m, tm=tm, weights=weights, casts=casts,
        row_ins=[(x, blk_x, idx), (p, (None,) + blk_p, lambda i: (layer, i, 0))], consts=consts,
        outs=outs, scratch=scratch)


def _s5_out(y_tiles, wb16, res, gain, *, tm):
    m, n = res.shape
    tc = tm // S5_T

    def body(wb, r, c, o, scr):
        for rs in _sub_blocks(tm):
            sub = rs.stop - rs.start
            cs = slice(rs.start // S5_T, rs.stop // S5_T)
            for t in range(S5_T):
                for l in range(S5_TILES):
                    scr[0][l, pl.ds(rs.start + t, sub // S5_T, stride=S5_T), :] = (
                        r[0][l, cs, t * LANES:(t + 1) * LANES].astype(F32))
            y = jnp.concatenate([scr[0][l, rs, :] for l in range(S5_TILES)], axis=1).astype(BF16)
            vg = _dot(y, wb[0][...])
            xn = r[1][rs, :] + vg[:, :n] * jax.nn.sigmoid(vg[:, n:])
            o[0][rs, :] = xn
            o[1][rs, :] = _rms(xn, c[0][...]).astype(BF16)

    blk_x, idx = _rows(tm, n)
    return _resident_call(
        body, name="s5_out", m=m, tm=tm, weights=[wb16],
        row_ins=[(y_tiles, (S5_TILES, tc, S5_T * LANES), lambda i: (0, i, 0)), (res, blk_x, idx)],
        consts=[gain.reshape(1, n)],
        outs=[((m, n), F32, blk_x, idx), ((m, n), BF16, blk_x, idx)],
        scratch=[pltpu.VMEM((S5_TILES, tm, LANES), F32)])


def _ffn_up_kernel(*refs, tm, n_casts):
    h_ref, w1_ref, w3_ref = refs[:3]
    ci_refs = refs[3:3 + n_casts]
    o_ref = refs[3 + n_casts]
    co_refs = refs[4 + n_casts:4 + 2 * n_casts]
    w1_scr, w3_scr = refs[4 + 2 * n_casts:]
    _run_casts(ci_refs, co_refs)

    @pl.when(pl.program_id(1) == 0)
    def _():
        w1_scr[...] = w1_ref[...].astype(BF16)
        w3_scr[...] = w3_ref[...].astype(BF16)

    for rs in _sub_blocks(tm):
        h = h_ref[rs, :]
        o_ref[rs, :] = (jax.nn.silu(_dot(h, w1_scr[...])) * _dot(h, w3_scr[...])).astype(o_ref.dtype)


def _ffn_up(h, w1, w3, layer, casts, *, tm, tf):
    m, d = h.shape
    ff = w1.shape[2]
    ni = m // tm
    cast_in, cast_out, cast_shape = _cast_specs(casts, (ff // tf) * ni, lambda f, i: f * ni + i)
    return pl.pallas_call(
        functools.partial(_ffn_up_kernel, tm=tm, n_casts=len(casts)),
        grid=(ff // tf, ni),
        in_specs=[
            pl.BlockSpec((tm, d), lambda f, i: (i, 0)),
            pl.BlockSpec((None, d, tf), lambda f, i: (layer, 0, f)),
            pl.BlockSpec((None, d, tf), lambda f, i: (layer, 0, f)),
        ] + cast_in,
        out_specs=[pl.BlockSpec((tm, tf), lambda f, i: (i, f))] + cast_out,
        out_shape=[jax.ShapeDtypeStruct((m, ff), BF16)] + cast_shape,
        scratch_shapes=[pltpu.VMEM((d, tf), BF16), pltpu.VMEM((d, tf), BF16)],
        compiler_params=_params("arbitrary", "arbitrary"),
        name="ffn_up",
    )(h, w1, w3, *[arr for arr, _ in casts])


def _piece_transpose(v):
    piece = jax.lax.broadcasted_iota(jnp.int32, v[0].shape, 1) // S5_GROUP
    for delta in (4, 2, 1):
        keep = (piece & delta) == 0
        shift = delta * S5_GROUP
        new = list(v)
        for i in range(S5_GB):
            if i & delta == 0:
                a, b = v[i], v[i + delta]
                new[i] = jnp.where(keep, a, pltpu.roll(b, shift, axis=1))
                new[i + delta] = jnp.where(keep, pltpu.roll(a, LANES - shift, axis=1), b)
        v = new
    return v


def _toeplitz_rows(strip, s):
    lo, hi = strip[:, :LANES], strip[:, LANES:]
    lane = jax.lax.broadcasted_iota(jnp.int32, lo.shape, 1)
    shift = (s * S5_GROUP) % LANES
    if s == 0:
        return strip
    if s * S5_GROUP < LANES:
        lo_r, hi_r = pltpu.roll(lo, shift, axis=1), pltpu.roll(hi, shift, axis=1)
        return jnp.concatenate([jnp.where(lane >= shift, lo_r, 0.0),
                                jnp.where(lane >= shift, hi_r, lo_r)], axis=1)
    lo_r = lo if shift == 0 else pltpu.roll(lo, shift, axis=1)
    return jnp.concatenate([jnp.zeros_like(lo), jnp.where(lane >= shift, lo_r, 0.0)], axis=1)


def _cmul_add(s, fa, fb, x):
    return s + fa * x + fb * pltpu.roll(x, S5_STATE, axis=1)


def _chunk_scan(ss, sc_ref, sf_ref, s_scr, c_scr, *, chunks_per_seq):
    groups = range(len(ss))
    rows = ss[0].shape[0]
    tiles = rows // SUBLANES
    r_idx = jax.lax.broadcasted_iota(jnp.int32, ss[0].shape, 0) % SUBLANES
    for k in range(3):
        d = 1 << k
        ss = [_cmul_add(ss[g], sc_ref[g, k:k + 1, :], sc_ref[g, 8 + k:9 + k, :],
                        jnp.where(r_idx >= d, pltpu.roll(ss[g], d, axis=0), 0.0)) for g in groups]
    for g in groups:
        s_scr[g] = ss[g]
    ts = [s_scr[g, pl.ds(SUBLANES - 1, tiles, stride=SUBLANES), :] for g in groups]
    j_idx = jax.lax.broadcasted_iota(jnp.int32, ts[0].shape, 0) % (chunks_per_seq // SUBLANES)
    for k in range(3, S5_SCAN_STEPS):
        d = 1 << (k - 3)
        ts = [_cmul_add(ts[g], sc_ref[g, k:k + 1, :], sc_ref[g, 8 + k:9 + k, :],
                        jnp.where(j_idx >= d, pltpu.roll(ts[g], d, axis=0), 0.0)) for g in groups]
    for g in groups:
        carry = jnp.where(j_idx >= 1, pltpu.roll(ts[g], 1, axis=0), 0.0)
        c_scr[g, 0] = carry
        c_scr[g, 1] = pltpu.roll(carry, S5_STATE, axis=1)
    outs = []
    for g in groups:
        fa, fb = sf_ref[g, :SUBLANES, :], sf_ref[g, SUBLANES:, :]
        out = []
        for j in range(tiles):
            cb = jnp.broadcast_to(c_scr[g, 0, j:j + 1, :], (SUBLANES, S5_RI))
            cs = jnp.broadcast_to(c_scr[g, 1, j:j + 1, :], (SUBLANES, S5_RI))
            out.append(ss[g][j * SUBLANES:(j + 1) * SUBLANES] + fa * cb + fb * cs)
        outs.append(jnp.concatenate(out, axis=0))
    return outs


def _spread(a, rep):
    hi = a.astype(BF16)
    lo = (a - hi.astype(F32)).astype(BF16)
    return _dot(hi, rep) + _dot(lo, rep)


def _s5_kernel(x_ref, bb_ref, ct_ref, pt_ref, rep_ref, skip_ref, min_ref, sc_ref, sf_ref, o_ref,
               mi_scr, mo_scr, s_scr, c_scr, *, chunks_per_seq):
    rows = x_ref.shape[1]
    n_idx = jax.lax.broadcasted_iota(jnp.int32, (rows, S5_RI), 0) % chunks_per_seq
    halves = S5_T // S5_GB
    gp = S5_GB * S5_STATE
    cl_re = _spread(ct_ref[0].reshape(gp, 2 * S5_GROUP), rep_ref[0])
    cl_im = _spread(ct_ref[1].reshape(gp, 2 * S5_GROUP), rep_ref[0])
    pt_re, pt_im = pt_ref[0].reshape(gp, 2 * S5_GROUP), pt_ref[1].reshape(gp, 2 * S5_GROUP)
    readout = []
    for k0 in range(2):
        pl_re, pl_im = _spread(pt_re, rep_ref[1 + k0]), _spread(pt_im, rep_ref[1 + k0])
        readout.append((cl_re * pl_re - cl_im * pl_im, -(cl_re * pl_im + cl_im * pl_re)))
    for gi in range(S5_GB):
        ps = slice(gi * S5_STATE, (gi + 1) * S5_STATE)
        tap = jnp.concatenate([readout[0][0][ps], readout[0][1][ps]], axis=0)
        mo_scr[gi] = jnp.concatenate([readout[1][0][ps], readout[1][1][ps]], axis=0).astype(BF16)
        strip = jnp.dot(bb_ref[gi], tap, preferred_element_type=F32,
                        precision=jax.lax.Precision.HIGHEST) + skip_ref[gi]
        for s in range(S5_T):
            mi_scr[gi, s * S5_GROUP:(s + 1) * S5_GROUP, :] = _toeplitz_rows(strip, s).astype(BF16)
    xin = [_piece_transpose([x_ref[0, :, (S5_GB * h + i) * LANES:(S5_GB * h + i + 1) * LANES]
                             for i in range(S5_GB)]) for h in range(halves)]
    groups = range(S5_GB)
    xg = [jnp.concatenate([xin[h][g] for h in range(halves)], axis=1) for g in groups]
    ys = [_dot(xg[g], mi_scr[g]) for g in groups]
    ss = [_dot(xg[g], min_ref[g]) for g in groups]
    ss = _chunk_scan(ss, sc_ref, sf_ref, s_scr, c_scr, chunks_per_seq=chunks_per_seq)
    for g in groups:
        s_prev = jnp.where(n_idx >= 1, pltpu.roll(ss[g], 1, axis=0), 0.0)
        ys[g] = jax.nn.gelu(ys[g] + _dot(s_prev.astype(BF16), mo_scr[g])).astype(o_ref.dtype)
    for h in range(halves):
        out = _piece_transpose([ys[gi][:, h * LANES:(h + 1) * LANES] for gi in range(S5_GB)])
        for i in range(S5_GB):
            t = S5_GB * h + i
            o_ref[0, :, t * LANES:(t + 1) * LANES] = out[i]


def _s5_core(x_tiles, bb, ct, pt, rep, skip, m_in, sc, sf, *, chunks_per_seq):
    tiles, rows, width = x_tiles.shape
    return pl.pallas_call(
        functools.partial(_s5_kernel, chunks_per_seq=chunks_per_seq),
        grid=(tiles,),
        in_specs=[
            pl.BlockSpec((1, rows, width), lambda i: (i, 0, 0)),
            pl.BlockSpec((S5_GB, S5_GROUP, S5_RI), lambda i: (i, 0, 0)),
            pl.BlockSpec((2, S5_GB, S5_STATE, 2 * S5_GROUP), lambda i: (0, i, 0, 0)),
            pl.BlockSpec((2, S5_GB, S5_STATE, 2 * S5_GROUP), lambda i: (0, i, 0, 0)),
            pl.BlockSpec((3, 2 * S5_GROUP, S5_TC), lambda i: (0, 0, 0)),
            pl.BlockSpec((S5_GB, S5_GROUP, S5_TC), lambda i: (i, 0, 0)),
            pl.BlockSpec((S5_GB, S5_TC, S5_RI), lambda i: (i, 0, 0)),
            pl.BlockSpec((S5_GB, 2 * SUBLANES, S5_RI), lambda i: (i, 0, 0)),
            pl.BlockSpec((S5_GB, 2 * SUBLANES, S5_RI), lambda i: (i, 0, 0)),
        ],
        out_specs=pl.BlockSpec((1, rows, width), lambda i: (i, 0, 0)),
        out_shape=jax.ShapeDtypeStruct((tiles, rows, width), BF16),
        scratch_shapes=[pltpu.VMEM((S5_GB, S5_TC, S5_TC), BF16),
                        pltpu.VMEM((S5_GB, S5_RI, S5_TC), BF16),
                        pltpu.VMEM((S5_GB, rows, S5_RI), F32),
                        pltpu.VMEM((S5_GB, 2, rows // SUBLANES, S5_RI), F32)],
        compiler_params=_params("parallel"),
        name="s5_core",
    )(x_tiles, bb, ct, pt, rep, skip, m_in, sc, sf)


def _s5_tables(a_re, a_im, log_dt, b_re, b_im, c_re, c_im, d):
    a_re, a_im = a_re.astype(F32), a_im.astype(F32)
    dt = jnp.exp(log_dt.astype(F32))[:, None]
    mag = jnp.exp(a_re * dt)
    lb_re, lb_im = mag * jnp.cos(a_im * dt), mag * jnp.sin(a_im * dt)
    den = a_re * a_re + a_im * a_im
    q_re = ((lb_re - 1.0) * a_re + lb_im * a_im) / den
    q_im = (lb_im * a_re - (lb_re - 1.0) * a_im) / den
    b_re = b_re.astype(F32).transpose(0, 2, 1)
    b_im = b_im.astype(F32).transpose(0, 2, 1)
    bb_re = q_re[:, None, :] * b_re - q_im[:, None, :] * b_im
    bb_im = q_re[:, None, :] * b_im + q_im[:, None, :] * b_re
    ks = jnp.arange(S5_T + 1, dtype=F32)[:, None, None]
    pmag = jnp.exp(ks * (a_re * dt))
    pk_re = pmag * jnp.cos(ks * (a_im * dt))
    pk_im = pmag * jnp.sin(ks * (a_im * dt))
    bb = jnp.concatenate([bb_re, bb_im], axis=2)
    pr_re, pr_im = pk_re[S5_T - 1::-1], pk_im[S5_T - 1::-1]
    e_re = pr_re[:, :, None, :] * bb_re - pr_im[:, :, None, :] * bb_im
    e_im = pr_re[:, :, None, :] * bb_im + pr_im[:, :, None, :] * bb_re
    m_in = jnp.concatenate([e_re, e_im], axis=3)
    m_in = m_in.transpose(1, 0, 2, 3).reshape(S5_GROUPS, S5_TC, S5_RI).astype(BF16)
    pad32 = lambda a: jnp.pad(a, ((0, 0), (0, 0), (0, 2 * S5_GROUP - a.shape[2])))
    ct = jnp.stack([pad32(c_re.astype(F32).transpose(0, 2, 1)), pad32(c_im.astype(F32).transpose(0, 2, 1))])
    pt = jnp.stack([pad32(pk_re.transpose(1, 2, 0)), pad32(pk_im.transpose(1, 2, 0))])
    lane = np.arange(S5_TC)[None, :]
    row = np.arange(2 * S5_GROUP)[:, None]
    rep = jnp.asarray(np.stack([lane % S5_GROUP == row, lane // S5_GROUP == row,
                                lane // S5_GROUP + 1 == row]), BF16)
    skip = d.astype(F32).reshape(S5_GROUPS, 1, S5_GROUP) * jnp.eye(S5_GROUP, dtype=F32)
    skip = jnp.pad(skip, ((0, 0), (0, 0), (0, S5_TC - S5_GROUP)))
    f_re, f_im = pk_re[S5_T], pk_im[S5_T]
    rows_a, rows_b = [], []
    for _ in range(S5_SCAN_STEPS):
        rows_a.append(jnp.concatenate([f_re, f_re], axis=1))
        rows_b.append(jnp.concatenate([-f_im, f_im], axis=1))
        f_re, f_im = f_re * f_re - f_im * f_im, 2.0 * f_re * f_im
    zero = jnp.zeros_like(rows_a[0])
    sc = jnp.stack(rows_a + [zero] + rows_b + [zero], axis=1)
    f_re, f_im = pk_re[S5_T], pk_im[S5_T]
    rows_a, rows_b = [], []
    for _ in range(SUBLANES):
        rows_a.append(jnp.concatenate([f_re, f_re], axis=1))
        rows_b.append(jnp.concatenate([-f_im, f_im], axis=1))
        f_re, f_im = (f_re * pk_re[S5_T] - f_im * pk_im[S5_T], f_re * pk_im[S5_T] + f_im * pk_re[S5_T])
    sf = jnp.stack(rows_a + rows_b, axis=1)
    return bb, ct, pt, rep, skip, m_in, sc, sf


def kernel(x, p, norm_mix, norm_ffn, norm_ple, norm_final, gm_w_in, gm_ln_g, gm_ln_b, gm_w_s, gm_b_s, gm_w_out, s5_w_in, s5_a_re, s5_a_im, s5_log_dt, s5_b_re, s5_b_im, s5_c_re, s5_c_im, s5_d, s5_w_out, ffn_w1, ffn_w3, ffn_w2, ple_w_gate, ple_w_proj):
    bsz, seq, d = x.shape
    depth = p.shape[0]
    m = bsz * seq
    xs = x.reshape(m, d)
    ps = p.reshape(depth, m, PLE_DIM)
    u = s5_out_b16 = None
    for i in range(depth):
        j = i // 2
        if i % 2 == 0:
            gm_in_b16, gm_out_b16 = _cast_call([(gm_w_in, j), (gm_w_out, j)])
            xs, hn = _gm_layer(xs, norm_mix[i], gm_in_b16, gm_ln_g, gm_ln_b, gm_w_s, gm_b_s, j, gm_out_b16,
                               norm_ffn[i], tm=512)
        else:
            tables = _s5_tables(s5_a_re[j], s5_a_im[j], s5_log_dt[j], s5_b_re[j], s5_b_im[j],
                                s5_c_re[j], s5_c_im[j], s5_d[j])
            y = _s5_core(u, *tables, chunks_per_seq=seq // S5_T)
            xs, hn = _s5_out(y, s5_out_b16, xs, norm_ffn[i], tm=512)
        a, w2_b16, wg_b16, wp_b16 = _ffn_up(
            hn, ffn_w1, ffn_w3, i, [(ffn_w2, i), (ple_w_gate, i), (ple_w_proj, i)], tm=2048, tf=512)
        if i + 1 < depth and (i + 1) % 2 == 1:
            jn = (i + 1) // 2
            xs, s5_in_b16 = _ffn_down(a, w2_b16, xs, [(s5_w_in, jn)], tm=256)
            xs, u, s5_out_b16 = _ple(xs, norm_ple[i], ps, i, wg_b16, wp_b16, [(s5_w_out, jn)], tm=512,
                                     s5_in=(norm_mix[i + 1], s5_in_b16))
        else:
            xs, = _ffn_down(a, w2_b16, xs, [], tm=256)
            xs, = _ple(xs, norm_ple[i], ps, i, wg_b16, wp_b16, [], tm=512,
                       final_gain=norm_final if i == depth - 1 else None)
    return xs.reshape(bsz, seq, d)
```

```python
import functools

import jax
import jax.numpy as jnp
import numpy as np
from jax.experimental import pallas as pl
from jax.experimental.pallas import tpu as pltpu

F32 = jnp.float32
BF16 = jnp.bfloat16

D_MODEL = 2048
PLE_DIM = 256
EPS = 1e-6
LANES = 128
SUBLANES = 8
BF16_ROWS = 16

GM_CHUNK = 128
GM_HEAD_DIM = 128
GM_HEADS = D_MODEL // GM_HEAD_DIM

S5_GROUP = 16
S5_GROUPS = D_MODEL // S5_GROUP
S5_STATE = 64
S5_T = 16
S5_TC = S5_T * S5_GROUP
S5_RI = 2 * S5_STATE
S5_GB = LANES // S5_GROUP
S5_TILES = D_MODEL // LANES
S5_SCAN_STEPS = 7

W_CHUNKS = 16
SUB_ROWS = 256
SUB_COLS = 512
VMEM_LIMIT = 56 * 1024 * 1024


def _params(*sem):
    return pltpu.CompilerParams(dimension_semantics=sem, vmem_limit_bytes=VMEM_LIMIT)


def _rms(xf, gain):
    ms = jnp.mean(xf * xf, axis=-1, keepdims=True)
    return xf * jax.lax.rsqrt(ms + EPS) * gain


def _dot(a, b):
    return jnp.dot(a, b, preferred_element_type=F32)


def _sub_blocks(tm):
    sub = min(tm, SUB_ROWS)
    return [slice(r, r + sub) for r in range(0, tm, sub)]


def _col_blocks(n):
    return [slice(c, c + SUB_COLS) for c in range(0, n, SUB_COLS)]


def _cast_rows(k, steps):
    rows = BF16_ROWS
    while k % rows or k // rows > steps:
        rows += BF16_ROWS
    return rows


def _cast_specs(casts, steps, step_of):
    ins, outs, shapes = [], [], []
    for arr, layer in casts:
        _, k, n = arr.shape
        rows = _cast_rows(k, steps)
        last = k // rows - 1
        ins.append(pl.BlockSpec(
            (None, rows, n), lambda *g, layer=layer, last=last: (layer, jnp.minimum(step_of(*g), last), 0)))
        outs.append(pl.BlockSpec((rows, n), lambda *g, last=last: (jnp.minimum(step_of(*g), last), 0)))
        shapes.append(jax.ShapeDtypeStruct((k, n), BF16))
    return ins, outs, shapes


def _run_casts(ci_refs, co_refs):
    for ci, co in zip(ci_refs, co_refs):
        co[...] = ci[...].astype(BF16)


def _cast_kernel(*refs):
    half = len(refs) // 2
    _run_casts(refs[:half], refs[half:])


def _cast_call(casts):
    cast_in, cast_out, cast_shape = _cast_specs(casts, W_CHUNKS, lambda s: s)
    return pl.pallas_call(
        _cast_kernel,
        grid=(W_CHUNKS,),
        in_specs=cast_in,
        out_specs=cast_out,
        out_shape=cast_shape,
        compiler_params=_params("arbitrary"),
        name="cast_weights",
    )(*[arr for arr, _ in casts])


def _resident_call(body, *, name, m, tm, weights, row_ins, consts, outs, casts=(), scratch=()):
    nw, nk, nr, nc, no = len(weights), len(casts), len(row_ins), len(consts), len(outs)
    steps = m // tm

    def kern(*refs):
        pos = 0

        def take(n):
            nonlocal pos
            pos += n
            return refs[pos - n:pos]

        w_refs, ci_refs, r_refs, c_refs = take(nw), take(nk), take(nr), take(nc)
        o_refs, co_refs, s_refs = take(no), take(nk), take(len(scratch))
        _run_casts(ci_refs, co_refs)
        body(w_refs, r_refs, c_refs, o_refs, s_refs)

    in_specs, args = [], []
    for arr in weights:
        in_specs.append(pl.BlockSpec(arr.shape, lambda s: (0, 0), pipeline_mode=pl.Buffered(1)))
        args.append(arr)
    cast_in, cast_out, cast_shape = _cast_specs(casts, steps, lambda s: s)
    in_specs += cast_in
    args += [arr for arr, _ in casts]
    for arr, block, index_fn in row_ins:
        in_specs.append(pl.BlockSpec(block, index_fn))
        args.append(arr)
    for arr in consts:
        in_specs.append(pl.BlockSpec((1, arr.shape[1]), lambda s: (0, 0)))
        args.append(arr)
    out_specs = [pl.BlockSpec(block, index_fn) for _, _, block, index_fn in outs]
    out_shape = [jax.ShapeDtypeStruct(shape, dtype) for shape, dtype, _, _ in outs]
    return pl.pallas_call(
        kern,
        grid=(steps,),
        in_specs=in_specs,
        out_specs=out_specs + cast_out,
        out_shape=out_shape + cast_shape,
        scratch_shapes=list(scratch),
        compiler_params=_params("arbitrary"),
        name=name,
    )(*args)


def _rows(tm, n):
    return (tm, n), (lambda i: (i, 0))


def _gm_layer(x, gain_mix, w_in_b16, ln_g, ln_b, w_s, b_s, layer, w_out_b16, gain, *, tm):
    m, n = x.shape
    bs = jnp.broadcast_to(b_s[layer][:, :, None], (GM_HEADS, GM_CHUNK, GM_HEAD_DIM)).astype(F32)

    def body(wb, r, c, o, scr):
        x_ref, ws_ref, bs_ref = r
        row = jax.lax.broadcasted_iota(jnp.int32, (GM_CHUNK, GM_CHUNK), 0)
        col = jax.lax.broadcasted_iota(jnp.int32, (GM_CHUNK, GM_CHUNK), 1)
        wm = [jnp.where(row >= col, ws_ref[h], 0.0).astype(BF16) for h in range(GM_HEADS)]
        for rs in _sub_blocks(tm):
            xf = x_ref[rs, :]
            hx = _rms(xf, c[3][...]).astype(BF16)
            v_all = jax.nn.gelu(_dot(hx, wb[0][:, n:]))
            u_all = jax.nn.gelu(_dot(hx, wb[0][:, :n])).astype(BF16)
            for c0 in range(0, rs.stop - rs.start, GM_CHUNK):
                rows = slice(rs.start + c0, rs.start + c0 + GM_CHUNK)
                v = v_all[c0:c0 + GM_CHUNK]
                mu = jnp.mean(v, axis=-1, keepdims=True)
                vc = v - mu
                var = jnp.mean(vc * vc, axis=-1, keepdims=True)
                vn = (vc * jax.lax.rsqrt(var + EPS) * c[1][...] + c[2][...]).astype(BF16)
                for h in range(GM_HEADS):
                    cols = slice(h * GM_HEAD_DIM, (h + 1) * GM_HEAD_DIM)
                    sv = _dot(wm[h], vn[:, cols]) + bs_ref[h]
                    scr[0][rows, cols] = (u_all[c0:c0 + GM_CHUNK, cols].astype(F32) * sv).astype(BF16)
            xn = xf + _dot(scr[0][rs, :], wb[1][...])
            o[0][rs, :] = xn
            o[1][rs, :] = _rms(xn, c[0][...]).astype(BF16)

    blk_x, idx = _rows(tm, n)
    return _resident_call(
        body, name="gm_layer", m=m, tm=tm, weights=[w_in_b16, w_out_b16],
        row_ins=[(x, blk_x, idx),
                 (w_s, (None, GM_HEADS, GM_CHUNK, GM_CHUNK), lambda i: (layer, 0, 0, 0)),
                 (bs, (GM_HEADS, GM_CHUNK, GM_HEAD_DIM), lambda i: (0, 0, 0))],
        consts=[gain.reshape(1, n), ln_g[layer].reshape(1, n), ln_b[layer].reshape(1, n),
                gain_mix.reshape(1, n)],
        outs=[((m, n), F32, blk_x, idx), ((m, n), BF16, blk_x, idx)],
        scratch=[pltpu.VMEM((tm, n), BF16)])


def _ffn_down(a, w2b16, res, casts, *, tm):
    m, n = res.shape

    def body(wb, r, c, o, scr):
        for rs in _sub_blocks(tm):
            o[0][rs, :] = r[1][rs, :] + _dot(r[0][rs, :], wb[0][...])

    blk_a, idx = _rows(tm, a.shape[1])
    blk_x, _ = _rows(tm, n)
    return _resident_call(
        body, name="ffn_down", m=m, tm=tm, weights=[w2b16], casts=casts,
        row_ins=[(a, blk_a, idx), (res, blk_x, idx)], consts=[],
        outs=[((m, n), F32, blk_x, idx)])


def _ffn_down_ple(a, w2b16, res, gain, p, layer, wgb16, wpb16, final_gain, *, tm):
    m, n = res.shape

    def body(wb, r, c, o, scr):
        for rs in _sub_blocks(tm):
            x1 = r[1][rs, :] + _dot(r[0][rs, :], wb[0][...])
            h = _rms(x1, c[0][...]).astype(BF16)
            pb = r[2][rs, :].astype(BF16)
            xn = jnp.concatenate(
                [x1[:, cs] + jax.nn.sigmoid(_dot(h, wb[1][:, cs])) * _dot(pb, wb[2][:, cs])
                 for cs in _col_blocks(n)], axis=1)
            if final_gain is not None:
                xn = _rms(xn, c[1][...])
            o[0][rs, :] = xn

    blk_a, idx = _rows(tm, a.shape[1])
    blk_x, _ = _rows(tm, n)
    blk_p, _ = _rows(tm, PLE_DIM)
    consts = [gain.reshape(1, n)] + ([] if final_gain is None else [final_gain.reshape(1, n)])
    return _resident_call(
        body, name="ffn_down_ple", m=m, tm=tm, weights=[w2b16, wgb16, wpb16],
        row_ins=[(a, blk_a, idx), (res, blk_x, idx), (p, (None,) + blk_p, lambda i: (layer, i, 0))],
        consts=consts, outs=[((m, n), F32, blk_x, idx)])


def _ple_s5_in(x, gain, p, layer, wgb16, wpb16, gain_mix, w_in_b16, casts, *, tm):
    m, n = x.shape
    tc = tm // S5_T

    def body(wb, r, c, o, scr):
        for rs in _sub_blocks(tm):
            xf = r[0][rs, :]
            h = _rms(xf, c[0][...]).astype(BF16)
            pb = r[1][rs, :].astype(BF16)
            xn = jnp.concatenate(
                [xf[:, cs] + jax.nn.sigmoid(_dot(h, wb[0][:, cs])) * _dot(pb, wb[1][:, cs])
                 for cs in _col_blocks(n)], axis=1)
            o[0][rs, :] = xn
            u = _dot(_rms(xn, c[1][...]).astype(BF16), wb[2][...])
            for l in range(S5_TILES):
                scr[0][l, rs, :] = u[:, l * LANES:(l + 1) * LANES]
            sub = rs.stop - rs.start
            cs = slice(rs.start // S5_T, rs.stop // S5_T)
            for t in range(S5_T):
                for l in range(S5_TILES):
                    o[1][l, cs, t * LANES:(t + 1) * LANES] = (
                        scr[0][l, pl.ds(rs.start + t, sub // S5_T, stride=S5_T), :].astype(BF16))

    blk_x, idx = _rows(tm, n)
    blk_p, _ = _rows(tm, PLE_DIM)
    return _resident_call(
        body, name="ple_s5_in", m=m, tm=tm, weights=[wgb16, wpb16, w_in_b16], casts=casts,
        row_ins=[(x, blk_x, idx), (p, (None,) + blk_p, lambda i: (layer, i, 0))],
        consts=[gain.reshape(1, n), gain_mix.reshape(1, n)],
        outs=[((m, n), F32, blk_x, idx),
              ((S5_TILES, m // S5_T, S5_T * LANES), BF16, (S5_TILES, tc, S5_T * LANES), lambda i: (0, i, 0))],
        scratch=[pltpu.VMEM((S5_TILES, tm, LANES), F32)])


def _s5_out(y_tiles, wb16, res, gain, *, tm):
    m, n = res.shape
    tc = tm // S5_T

    def body(wb, r, c, o, scr):
        for rs in _sub_blocks(tm):
            sub = rs.stop - rs.start
            cs = slice(rs.start // S5_T, rs.stop // S5_T)
            for t in range(S5_T):
                for l in range(S5_TILES):
                    scr[0][l, pl.ds(rs.start + t, sub // S5_T, stride=S5_T), :] = (
                        r[0][l, cs, t * LANES:(t + 1) * LANES].astype(F32))
            y = jnp.concatenate([scr[0][l, rs, :] for l in range(S5_TILES)], axis=1).astype(BF16)
            vg = _dot(y, wb[0][...])
            xn = r[1][rs, :] + vg[:, :n] * jax.nn.sigmoid(vg[:, n:])
            o[0][rs, :] = xn
            o[1][rs, :] = _rms(xn, c[0][...]).astype(BF16)

    blk_x, idx = _rows(tm, n)
    return _resident_call(
        body, name="s5_out", m=m, tm=tm, weights=[wb16],
        row_ins=[(y_tiles, (S5_TILES, tc, S5_T * LANES), lambda i: (0, i, 0)), (res, blk_x, idx)],
        consts=[gain.reshape(1, n)],
        outs=[((m, n), F32, blk_x, idx), ((m, n), BF16, blk_x, idx)],
        scratch=[pltpu.VMEM((S5_TILES, tm, LANES), F32)])


def _ffn_up_kernel(*refs, tm, n_casts):
    h_ref, w1_ref, w3_ref = refs[:3]
    ci_refs = refs[3:3 + n_casts]
    o_ref = refs[3 + n_casts]
    co_refs = refs[4 + n_casts:4 + 2 * n_casts]
    w1_scr, w3_scr = refs[4 + 2 * n_casts:]
    _run_casts(ci_refs, co_refs)

    @pl.when(pl.program_id(1) == 0)
    def _():
        w1_scr[...] = w1_ref[...].astype(BF16)
        w3_scr[...] = w3_ref[...].astype(BF16)

    for rs in _sub_blocks(tm):
        h = h_ref[rs, :]
        o_ref[rs, :] = (jax.nn.silu(_dot(h, w1_scr[...])) * _dot(h, w3_scr[...])).astype(o_ref.dtype)


def _ffn_up(h, w1, w3, layer, casts, *, tm, tf):
    m, d = h.shape
    ff = w1.shape[2]
    ni = m // tm
    cast_in, cast_out, cast_shape = _cast_specs(casts, (ff // tf) * ni, lambda f, i: f * ni + i)
    return pl.pallas_call(
        functools.partial(_ffn_up_kernel, tm=tm, n_casts=len(casts)),
        grid=(ff // tf, ni),
        in_specs=[
            pl.BlockSpec((tm, d), lambda f, i: (i, 0)),
            pl.BlockSpec((None, d, tf), lambda f, i: (layer, 0, f)),
            pl.BlockSpec((None, d, tf), lambda f, i: (layer, 0, f)),
        ] + cast_in,
        out_specs=[pl.BlockSpec((tm, tf), lambda f, i: (i, f))] + cast_out,
        out_shape=[jax.ShapeDtypeStruct((m, ff), BF16)] + cast_shape,
        scratch_shapes=[pltpu.VMEM((d, tf), BF16), pltpu.VMEM((d, tf), BF16)],
        compiler_params=_params("arbitrary", "arbitrary"),
        name="ffn_up",
    )(h, w1, w3, *[arr for arr, _ in casts])


def _piece_transpose(v):
    piece = jax.lax.broadcasted_iota(jnp.int32, v[0].shape, 1) // S5_GROUP
    for delta in (4, 2, 1):
        keep = (piece & delta) == 0
        shift = delta * S5_GROUP
        new = list(v)
        for i in range(S5_GB):
            if i & delta == 0:
                a, b = v[i], v[i + delta]
                new[i] = jnp.where(keep, a, pltpu.roll(b, shift, axis=1))
                new[i + delta] = jnp.where(keep, pltpu.roll(a, LANES - shift, axis=1), b)
        v = new
    return v


def _toeplitz_rows(strip, s):
    lo, hi = strip[:, :LANES], strip[:, LANES:]
    lane = jax.lax.broadcasted_iota(jnp.int32, lo.shape, 1)
    shift = (s * S5_GROUP) % LANES
    if s == 0:
        return strip
    if s * S5_GROUP < LANES:
        lo_r, hi_r = pltpu.roll(lo, shift, axis=1), pltpu.roll(hi, shift, axis=1)
        return jnp.concatenate([jnp.where(lane >= shift, lo_r, 0.0),
                                jnp.where(lane >= shift, hi_r, lo_r)], axis=1)
    lo_r = lo if shift == 0 else pltpu.roll(lo, shift, axis=1)
    return jnp.concatenate([jnp.zeros_like(lo), jnp.where(lane >= shift, lo_r, 0.0)], axis=1)


def _cmul_add(s, fa, fb, x):
    return s + fa * x + fb * pltpu.roll(x, S5_STATE, axis=1)


def _chunk_scan(ss, sc_ref, sf_ref, s_scr, c_scr, *, chunks_per_seq):
    groups = range(len(ss))
    rows = ss[0].shape[0]
    tiles = rows // SUBLANES
    r_idx = jax.lax.broadcasted_iota(jnp.int32, ss[0].shape, 0) % SUBLANES
    for k in range(3):
        d = 1 << k
        ss = [_cmul_add(ss[g], sc_ref[g, k:k + 1, :], sc_ref[g, 8 + k:9 + k, :],
                        jnp.where(r_idx >= d, pltpu.roll(ss[g], d, axis=0), 0.0)) for g in groups]
    for g in groups:
        s_scr[g] = ss[g]
    ts = [s_scr[g, pl.ds(SUBLANES - 1, tiles, stride=SUBLANES), :] for g in groups]
    j_idx = jax.lax.broadcasted_iota(jnp.int32, ts[0].shape, 0) % (chunks_per_seq // SUBLANES)
    for k in range(3, S5_SCAN_STEPS):
        d = 1 << (k - 3)
        ts = [_cmul_add(ts[g], sc_ref[g, k:k + 1, :], sc_ref[g, 8 + k:9 + k, :],
                        jnp.where(j_idx >= d, pltpu.roll(ts[g], d, axis=0), 0.0)) for g in groups]
    for g in groups:
        carry = jnp.where(j_idx >= 1, pltpu.roll(ts[g], 1, axis=0), 0.0)
        c_scr[g, 0] = carry
        c_scr[g, 1] = pltpu.roll(carry, S5_STATE, axis=1)
    outs = []
    for g in groups:
        fa, fb = sf_ref[g, :SUBLANES, :], sf_ref[g, SUBLANES:, :]
        out = []
        for j in range(tiles):
            cb = jnp.broadcast_to(c_scr[g, 0, j:j + 1, :], (SUBLANES, S5_RI))
            cs = jnp.broadcast_to(c_scr[g, 1, j:j + 1, :], (SUBLANES, S5_RI))
            out.append(ss[g][j * SUBLANES:(j + 1) * SUBLANES] + fa * cb + fb * cs)
        outs.append(jnp.concatenate(out, axis=0))
    return outs


def _spread(a, rep):
    hi = a.astype(BF16)
    lo = (a - hi.astype(F32)).astype(BF16)
    return _dot(hi, rep) + _dot(lo, rep)


def _s5_kernel(x_ref, bb_ref, ct_ref, pt_ref, rep_ref, skip_ref, min_ref, sc_ref, sf_ref, o_ref,
               mi_scr, mo_scr, s_scr, c_scr, *, chunks_per_seq):
    rows = x_ref.shape[1]
    n_idx = jax.lax.broadcasted_iota(jnp.int32, (rows, S5_RI), 0) % chunks_per_seq
    halves = S5_T // S5_GB
    gp = S5_GB * S5_STATE
    cl_re = _spread(ct_ref[0].reshape(gp, 2 * S5_GROUP), rep_ref[0])
    cl_im = _spread(ct_ref[1].reshape(gp, 2 * S5_GROUP), rep_ref[0])
    pt_re, pt_im = pt_ref[0].reshape(gp, 2 * S5_GROUP), pt_ref[1].reshape(gp, 2 * S5_GROUP)
    readout = []
    for k0 in range(2):
        pl_re, pl_im = _spread(pt_re, rep_ref[1 + k0]), _spread(pt_im, rep_ref[1 + k0])
        readout.append((cl_re * pl_re - cl_im * pl_im, -(cl_re * pl_im + cl_im * pl_re)))
    for gi in range(S5_GB):
        ps = slice(gi * S5_STATE, (gi + 1) * S5_STATE)
        tap = jnp.concatenate([readout[0][0][ps], readout[0][1][ps]], axis=0)
        mo_scr[gi] = jnp.concatenate([readout[1][0][ps], readout[1][1][ps]], axis=0).astype(BF16)
        strip = jnp.dot(bb_ref[gi], tap, preferred_element_type=F32,
                        precision=jax.lax.Precision.HIGHEST) + skip_ref[gi]
        for s in range(S5_T):
            mi_scr[gi, s * S5_GROUP:(s + 1) * S5_GROUP, :] = _toeplitz_rows(strip, s).astype(BF16)
    xin = [_piece_transpose([x_ref[0, :, (S5_GB * h + i) * LANES:(S5_GB * h + i + 1) * LANES]
                             for i in range(S5_GB)]) for h in range(halves)]
    groups = range(S5_GB)
    xg = [jnp.concatenate([xin[h][g] for h in range(halves)], axis=1) for g in groups]
    ys = [_dot(xg[g], mi_scr[g]) for g in groups]
    ss = [_dot(xg[g], min_ref[g]) for g in groups]
    ss = _chunk_scan(ss, sc_ref, sf_ref, s_scr, c_scr, chunks_per_seq=chunks_per_seq)
    for g in groups:
        s_prev = jnp.where(n_idx >= 1, pltpu.roll(ss[g], 1, axis=0), 0.0)
        ys[g] = jax.nn.gelu(ys[g] + _dot(s_prev.astype(BF16), mo_scr[g])).astype(o_ref.dtype)
    for h in range(halves):
        out = _piece_transpose([ys[gi][:, h * LANES:(h + 1) * LANES] for gi in range(S5_GB)])
        for i in range(S5_GB):
            t = S5_GB * h + i
            o_ref[0, :, t * LANES:(t + 1) * LANES] = out[i]


def _s5_core(x_tiles, bb, ct, pt, rep, skip, m_in, sc, sf, *, chunks_per_seq):
    tiles, rows, width = x_tiles.shape
    return pl.pallas_call(
        functools.partial(_s5_kernel, chunks_per_seq=chunks_per_seq),
        grid=(tiles,),
        in_specs=[
            pl.BlockSpec((1, rows, width), lambda i: (i, 0, 0)),
            pl.BlockSpec((S5_GB, S5_GROUP, S5_RI), lambda i: (i, 0, 0)),
            pl.BlockSpec((2, S5_GB, S5_STATE, 2 * S5_GROUP), lambda i: (0, i, 0, 0)),
            pl.BlockSpec((2, S5_GB, S5_STATE, 2 * S5_GROUP), lambda i: (0, i, 0, 0)),
            pl.BlockSpec((3, 2 * S5_GROUP, S5_TC), lambda i: (0, 0, 0)),
            pl.BlockSpec((S5_GB, S5_GROUP, S5_TC), lambda i: (i, 0, 0)),
            pl.BlockSpec((S5_GB, S5_TC, S5_RI), lambda i: (i, 0, 0)),
            pl.BlockSpec((S5_GB, 2 * SUBLANES, S5_RI), lambda i: (i, 0, 0)),
            pl.BlockSpec((S5_GB, 2 * SUBLANES, S5_RI), lambda i: (i, 0, 0)),
        ],
        out_specs=pl.BlockSpec((1, rows, width), lambda i: (i, 0, 0)),
        out_shape=jax.ShapeDtypeStruct((tiles, rows, width), BF16),
        scratch_shapes=[pltpu.VMEM((S5_GB, S5_TC, S5_TC), BF16),
                        pltpu.VMEM((S5_GB, S5_RI, S5_TC), BF16),
                        pltpu.VMEM((S5_GB, rows, S5_RI), F32),
                        pltpu.VMEM((S5_GB, 2, rows // SUBLANES, S5_RI), F32)],
        compiler_params=_params("parallel"),
        name="s5_core",
    )(x_tiles, bb, ct, pt, rep, skip, m_in, sc, sf)


def _s5_tables(a_re, a_im, log_dt, b_re, b_im, c_re, c_im, d):
    a_re, a_im = a_re.astype(F32), a_im.astype(F32)
    dt = jnp.exp(log_dt.astype(F32))[:, None]
    mag = jnp.exp(a_re * dt)
    lb_re, lb_im = mag * jnp.cos(a_im * dt), mag * jnp.sin(a_im * dt)
    den = a_re * a_re + a_im * a_im
    q_re = ((lb_re - 1.0) * a_re + lb_im * a_im) / den
    q_im = (lb_im * a_re - (lb_re - 1.0) * a_im) / den
    b_re = b_re.astype(F32).transpose(0, 2, 1)
    b_im = b_im.astype(F32).transpose(0, 2, 1)
    bb_re = q_re[:, None, :] * b_re - q_im[:, None, :] * b_im
    bb_im = q_re[:, None, :] * b_im + q_im[:, None, :] * b_re
    ks = jnp.arange(S5_T + 1, dtype=F32)[:, None, None]
    pmag = jnp.exp(ks * (a_re * dt))
    pk_re = pmag * jnp.cos(ks * (a_im * dt))
    pk_im = pmag * jnp.sin(ks * (a_im * dt))
    bb = jnp.concatenate([bb_re, bb_im], axis=2)
    pr_re, pr_im = pk_re[S5_T - 1::-1], pk_im[S5_T - 1::-1]
    e_re = pr_re[:, :, None, :] * bb_re - pr_im[:, :, None, :] * bb_im
    e_im = pr_re[:, :, None, :] * bb_im + pr_im[:, :, None, :] * bb_re
    m_in = jnp.concatenate([e_re, e_im], axis=3)
    m_in = m_in.transpose(1, 0, 2, 3).reshape(S5_GROUPS, S5_TC, S5_RI).astype(BF16)
    pad32 = lambda a: jnp.pad(a, ((0, 0), (0, 0), (0, 2 * S5_GROUP - a.shape[2])))
    ct = jnp.stack([pad32(c_re.astype(F32).transpose(0, 2, 1)), pad32(c_im.astype(F32).transpose(0, 2, 1))])
    pt = jnp.stack([pad32(pk_re.transpose(1, 2, 0)), pad32(pk_im.transpose(1, 2, 0))])
    lane = np.arange(S5_TC)[None, :]
    row = np.arange(2 * S5_GROUP)[:, None]
    rep = jnp.asarray(np.stack([lane % S5_GROUP == row, lane // S5_GROUP == row,
                                lane // S5_GROUP + 1 == row]), BF16)
    skip = d.astype(F32).reshape(S5_GROUPS, 1, S5_GROUP) * jnp.eye(S5_GROUP, dtype=F32)
    skip = jnp.pad(skip, ((0, 0), (0, 0), (0, S5_TC - S5_GROUP)))
    f_re, f_im = pk_re[S5_T], pk_im[S5_T]
    rows_a, rows_b = [], []
    for _ in range(S5_SCAN_STEPS):
        rows_a.append(jnp.concatenate([f_re, f_re], axis=1))
        rows_b.append(jnp.concatenate([-f_im, f_im], axis=1))
        f_re, f_im = f_re * f_re - f_im * f_im, 2.0 * f_re * f_im
    zero = jnp.zeros_like(rows_a[0])
    sc = jnp.stack(rows_a + [zero] + rows_b + [zero], axis=1)
    f_re, f_im = pk_re[S5_T], pk_im[S5_T]
    rows_a, rows_b = [], []
    for _ in range(SUBLANES):
        rows_a.append(jnp.concatenate([f_re, f_re], axis=1))
        rows_b.append(jnp.concatenate([-f_im, f_im], axis=1))
        f_re, f_im = (f_re * pk_re[S5_T] - f_im * pk_im[S5_T], f_re * pk_im[S5_T] + f_im * pk_re[S5_T])
    sf = jnp.stack(rows_a + rows_b, axis=1)
    return bb, ct, pt, rep, skip, m_in, sc, sf


def kernel(x, p, norm_mix, norm_ffn, norm_ple, norm_final, gm_w_in, gm_ln_g, gm_ln_b, gm_w_s, gm_b_s, gm_w_out, s5_w_in, s5_a_re, s5_a_im, s5_log_dt, s5_b_re, s5_b_im, s5_c_re, s5_c_im, s5_d, s5_w_out, ffn_w1, ffn_w3, ffn_w2, ple_w_gate, ple_w_proj):
    bsz, seq, d = x.shape
    depth = p.shape[0]
    m = bsz * seq
    xs = x.reshape(m, d)
    ps = p.reshape(depth, m, PLE_DIM)
    u = s5_out_b16 = None
    for i in range(depth):
        j = i // 2
        if i % 2 == 0:
            gm_in_b16, gm_out_b16 = _cast_call([(gm_w_in, j), (gm_w_out, j)])
            xs, hn = _gm_layer(xs, norm_mix[i], gm_in_b16, gm_ln_g, gm_ln_b, gm_w_s, gm_b_s, j, gm_out_b16,
                               norm_ffn[i], tm=512)
        else:
            tables = _s5_tables(s5_a_re[j], s5_a_im[j], s5_log_dt[j], s5_b_re[j], s5_b_im[j],
                                s5_c_re[j], s5_c_im[j], s5_d[j])
            y = _s5_core(u, *tables, chunks_per_seq=seq // S5_T)
            xs, hn = _s5_out(y, s5_out_b16, xs, norm_ffn[i], tm=512)
        a, w2_b16, wg_b16, wp_b16 = _ffn_up(
            hn, ffn_w1, ffn_w3, i, [(ffn_w2, i), (ple_w_gate, i), (ple_w_proj, i)], tm=2048, tf=512)
        if i + 1 < depth and (i + 1) % 2 == 1:
            jn = (i + 1) // 2
            xs, s5_in_b16 = _ffn_down(a, w2_b16, xs, [(s5_w_in, jn)], tm=256)
            xs, u, s5_out_b16 = _ple_s5_in(xs, norm_ple[i], ps, i, wg_b16, wp_b16, norm_mix[i + 1], s5_in_b16,
                                           [(s5_w_out, jn)], tm=512)
        else:
            xs, = _ffn_down_ple(a, w2_b16, xs, norm_ple[i], ps, i, wg_b16, wp_b16,
                                norm_final if i == depth - 1 else None, tm=256)
    return xs.reshape(bsz, seq, d)
```

```python
import functools

import jax
import jax.numpy as jnp
import numpy as np
from jax.experimental import pallas as pl
from jax.experimental.pallas import tpu as pltpu

F32 = jnp.float32
BF16 = jnp.bfloat16

D_MODEL = 2048
PLE_DIM = 256
EPS = 1e-6
LANES = 128
SUBLANES = 8
BF16_ROWS = 16

GM_CHUNK = 128
GM_HEAD_DIM = 128
GM_HEADS = D_MODEL // GM_HEAD_DIM

S5_GROUP = 16
S5_GROUPS = D_MODEL // S5_GROUP
S5_STATE = 64
S5_T = 16
S5_TC = S5_T * S5_GROUP
S5_RI = 2 * S5_STATE
S5_GB = LANES // S5_GROUP
S5_TILES = D_MODEL // LANES
S5_SCAN_STEPS = 7

W_CHUNKS = 16
SUB_ROWS = 256
SUB_COLS = 512
VMEM_LIMIT = 56 * 1024 * 1024


def _params(*sem):
    return pltpu.CompilerParams(dimension_semantics=sem, vmem_limit_bytes=VMEM_LIMIT)


def _rms(xf, gain):
    ms = jnp.mean(xf * xf, axis=-1, keepdims=True)
    return xf * jax.lax.rsqrt(ms + EPS) * gain


def _dot(a, b):
    return jnp.dot(a, b, preferred_element_type=F32)


def _sub_blocks(tm):
    sub = min(tm, SUB_ROWS)
    return [slice(r, r + sub) for r in range(0, tm, sub)]


def _col_blocks(n):
    return [slice(c, c + SUB_COLS) for c in range(0, n, SUB_COLS)]


def _cast_rows(k, steps):
    rows = BF16_ROWS
    while k % rows or k // rows > steps:
        rows += BF16_ROWS
    return rows


def _cast_specs(casts, steps, step_of):
    ins, outs, shapes = [], [], []
    for arr, layer in casts:
        _, k, n = arr.shape
        rows = _cast_rows(k, steps)
        last = k // rows - 1
        ins.append(pl.BlockSpec(
            (None, rows, n), lambda *g, layer=layer, last=last: (layer, jnp.minimum(step_of(*g), last), 0)))
        outs.append(pl.BlockSpec((rows, n), lambda *g, last=last: (jnp.minimum(step_of(*g), last), 0)))
        shapes.append(jax.ShapeDtypeStruct((k, n), BF16))
    return ins, outs, shapes


def _run_casts(ci_refs, co_refs):
    for ci, co in zip(ci_refs, co_refs):
        co[...] = ci[...].astype(BF16)


def _cast_kernel(*refs):
    half = len(refs) // 2
    _run_casts(refs[:half], refs[half:])


def _cast_call(casts):
    cast_in, cast_out, cast_shape = _cast_specs(casts, W_CHUNKS, lambda s: s)
    return pl.pallas_call(
        _cast_kernel,
        grid=(W_CHUNKS,),
        in_specs=cast_in,
        out_specs=cast_out,
        out_shape=cast_shape,
        compiler_params=_params("arbitrary"),
        name="cast_weights",
    )(*[arr for arr, _ in casts])


def _resident_call(body, *, name, m, tm, weights, row_ins, consts, outs, casts=(), scratch=()):
    nw, nk, nr, nc, no = len(weights), len(casts), len(row_ins), len(consts), len(outs)
    steps = m // tm

    def kern(*refs):
        pos = 0

        def take(n):
            nonlocal pos
            pos += n
            return refs[pos - n:pos]

        w_refs, ci_refs, r_refs, c_refs = take(nw), take(nk), take(nr), take(nc)
        o_refs, co_refs, s_refs = take(no), take(nk), take(len(scratch))
        _run_casts(ci_refs, co_refs)
        body(w_refs, r_refs, c_refs, o_refs, s_refs)

    in_specs, args = [], []
    for arr in weights:
        in_specs.append(pl.BlockSpec(arr.shape, lambda s: (0, 0), pipeline_mode=pl.Buffered(1)))
        args.append(arr)
    cast_in, cast_out, cast_shape = _cast_specs(casts, steps, lambda s: s)
    in_specs += cast_in
    args += [arr for arr, _ in casts]
    for arr, block, index_fn in row_ins:
        in_specs.append(pl.BlockSpec(block, index_fn))
        args.append(arr)
    for arr in consts:
        in_specs.append(pl.BlockSpec((1, arr.shape[1]), lambda s: (0, 0)))
        args.append(arr)
    out_specs = [pl.BlockSpec(block, index_fn) for _, _, block, index_fn in outs]
    out_shape = [jax.ShapeDtypeStruct(shape, dtype) for shape, dtype, _, _ in outs]
    return pl.pallas_call(
        kern,
        grid=(steps,),
        in_specs=in_specs,
        out_specs=out_specs + cast_out,
        out_shape=out_shape + cast_shape,
        scratch_shapes=list(scratch),
        compiler_params=_params("arbitrary"),
        name=name,
    )(*args)


def _rows(tm, n):
    return (tm, n), (lambda i: (i, 0))


def _gm_layer(x, gain_mix, w_in_b16, ln_g, ln_b, w_s, b_s, layer, w_out_b16, gain, *, tm):
    m, n = x.shape
    bs = jnp.broadcast_to(b_s[layer][:, :, None], (GM_HEADS, GM_CHUNK, GM_HEAD_DIM)).astype(F32)

    def body(wb, r, c, o, scr):
        x_ref, ws_ref, bs_ref = r
        row = jax.lax.broadcasted_iota(jnp.int32, (GM_CHUNK, GM_CHUNK), 0)
        col = jax.lax.broadcasted_iota(jnp.int32, (GM_CHUNK, GM_CHUNK), 1)
        wm = [jnp.where(row >= col, ws_ref[h], 0.0).astype(BF16) for h in range(GM_HEADS)]
        for rs in _sub_blocks(tm):
            xf = x_ref[rs, :]
            hx = _rms(xf, c[3][...]).astype(BF16)
            v_all = jax.nn.gelu(_dot(hx, wb[0][:, n:]))
            u_all = jax.nn.gelu(_dot(hx, wb[0][:, :n])).astype(BF16)
            for c0 in range(0, rs.stop - rs.start, GM_CHUNK):
                rows = slice(rs.start + c0, rs.start + c0 + GM_CHUNK)
                v = v_all[c0:c0 + GM_CHUNK]
                mu = jnp.mean(v, axis=-1, keepdims=True)
                vc = v - mu
                var = jnp.mean(vc * vc, axis=-1, keepdims=True)
                vn = (vc * jax.lax.rsqrt(var + EPS) * c[1][...] + c[2][...]).astype(BF16)
                for h in range(GM_HEADS):
                    cols = slice(h * GM_HEAD_DIM, (h + 1) * GM_HEAD_DIM)
                    sv = _dot(wm[h], vn[:, cols]) + bs_ref[h]
                    scr[0][rows, cols] = (u_all[c0:c0 + GM_CHUNK, cols].astype(F32) * sv).astype(BF16)
            xn = xf + _dot(scr[0][rs, :], wb[1][...])
            o[0][rs, :] = xn
            o[1][rs, :] = _rms(xn, c[0][...]).astype(BF16)

    blk_x, idx = _rows(tm, n)
    return _resident_call(
        body, name="gm_layer", m=m, tm=tm, weights=[w_in_b16, w_out_b16],
        row_ins=[(x, blk_x, idx),
                 (w_s, (None, GM_HEADS, GM_CHUNK, GM_CHUNK), lambda i: (layer, 0, 0, 0)),
                 (bs, (GM_HEADS, GM_CHUNK, GM_HEAD_DIM), lambda i: (0, 0, 0))],
        consts=[gain.reshape(1, n), ln_g[layer].reshape(1, n), ln_b[layer].reshape(1, n),
                gain_mix.reshape(1, n)],
        outs=[((m, n), F32, blk_x, idx), ((m, n), BF16, blk_x, idx)],
        scratch=[pltpu.VMEM((tm, n), BF16)])


A_SLOTS = 3


def _ffn_down_kernel(*refs, tm, steps, n_casts):
    w_ref = refs[0]
    ci_refs = refs[1:1 + n_casts]
    res_ref, a_hbm, o_ref = refs[1 + n_casts:4 + n_casts]
    co_refs = refs[4 + n_casts:4 + 2 * n_casts]
    a_buf, sem = refs[4 + 2 * n_casts:]
    s = pl.program_id(0)
    _run_casts(ci_refs, co_refs)

    def tile_copy(step):
        slot = step % A_SLOTS
        return pltpu.make_async_copy(a_hbm.at[pl.ds(pl.multiple_of(step * tm, tm), tm), :],
                                     a_buf.at[slot], sem.at[slot])

    @pl.when(s == 0)
    def _():
        for first in range(A_SLOTS - 1):
            tile_copy(first).start()

    @pl.when(s + A_SLOTS - 1 < steps)
    def _():
        tile_copy(s + A_SLOTS - 1).start()

    tile_copy(s).wait()
    slot = s % A_SLOTS
    for rs in _sub_blocks(tm):
        o_ref[rs, :] = res_ref[rs, :] + _dot(a_buf[slot, rs, :], w_ref[...])


def _ffn_down(a, w2b16, res, casts, *, tm):
    m, n = res.shape
    steps = m // tm
    assert steps >= A_SLOTS
    cast_in, cast_out, cast_shape = _cast_specs(casts, steps, lambda s: s)
    return pl.pallas_call(
        functools.partial(_ffn_down_kernel, tm=tm, steps=steps, n_casts=len(casts)),
        grid=(steps,),
        in_specs=[pl.BlockSpec(w2b16.shape, lambda s: (0, 0), pipeline_mode=pl.Buffered(1))] + cast_in + [
            pl.BlockSpec((tm, n), lambda s: (s, 0)),
            pl.BlockSpec(memory_space=pl.ANY)],
        out_specs=[pl.BlockSpec((tm, n), lambda s: (s, 0))] + cast_out,
        out_shape=[jax.ShapeDtypeStruct((m, n), F32)] + cast_shape,
        scratch_shapes=[pltpu.VMEM((A_SLOTS, tm, a.shape[1]), BF16), pltpu.SemaphoreType.DMA((A_SLOTS,))],
        compiler_params=_params("arbitrary"),
        name="ffn_down",
    )(w2b16, *[arr for arr, _ in casts], res, a)


def _ffn_down_ple(a, w2b16, res, gain, p, layer, wgb16, wpb16, final_gain, *, tm):
    m, n = res.shape

    def body(wb, r, c, o, scr):
        for rs in _sub_blocks(tm):
            x1 = r[1][rs, :] + _dot(r[0][rs, :], wb[0][...])
            h = _rms(x1, c[0][...]).astype(BF16)
            pb = r[2][rs, :].astype(BF16)
            xn = jnp.concatenate(
                [x1[:, cs] + jax.nn.sigmoid(_dot(h, wb[1][:, cs])) * _dot(pb, wb[2][:, cs])
                 for cs in _col_blocks(n)], axis=1)
            if final_gain is not None:
                xn = _rms(xn, c[1][...])
            o[0][rs, :] = xn

    blk_a, idx = _rows(tm, a.shape[1])
    blk_x, _ = _rows(tm, n)
    blk_p, _ = _rows(tm, PLE_DIM)
    consts = [gain.reshape(1, n)] + ([] if final_gain is None else [final_gain.reshape(1, n)])
    return _resident_call(
        body, name="ffn_down_ple", m=m, tm=tm, weights=[w2b16, wgb16, wpb16],
        row_ins=[(a, blk_a, idx), (res, blk_x, idx), (p, (None,) + blk_p, lambda i: (layer, i, 0))],
        consts=consts, outs=[((m, n), F32, blk_x, idx)])


def _ple_s5_in(x, gain, p, layer, wgb16, wpb16, gain_mix, w_in_b16, casts, *, tm):
    m, n = x.shape
    tc = tm // S5_T

    def body(wb, r, c, o, scr):
        for rs in _sub_blocks(tm):
            xf = r[0][rs, :]
            h = _rms(xf, c[0][...]).astype(BF16)
            pb = r[1][rs, :].astype(BF16)
            xn = jnp.concatenate(
                [xf[:, cs] + jax.nn.sigmoid(_dot(h, wb[0][:, cs])) * _dot(pb, wb[1][:, cs])
                 for cs in _col_blocks(n)], axis=1)
            o[0][rs, :] = xn
            u = _dot(_rms(xn, c[1][...]).astype(BF16), wb[2][...])
            for l in range(S5_TILES):
                scr[0][l, rs, :] = u[:, l * LANES:(l + 1) * LANES]
            sub = rs.stop - rs.start
            cs = slice(rs.start // S5_T, rs.stop // S5_T)
            for t in range(S5_T):
                for l in range(S5_TILES):
                    o[1][l, cs, t * LANES:(t + 1) * LANES] = (
                        scr[0][l, pl.ds(rs.start + t, sub // S5_T, stride=S5_T), :].astype(BF16))

    blk_x, idx = _rows(tm, n)
    blk_p, _ = _rows(tm, PLE_DIM)
    return _resident_call(
        body, name="ple_s5_in", m=m, tm=tm, weights=[wgb16, wpb16, w_in_b16], casts=casts,
        row_ins=[(x, blk_x, idx), (p, (None,) + blk_p, lambda i: (layer, i, 0))],
        consts=[gain.reshape(1, n), gain_mix.reshape(1, n)],
        outs=[((m, n), F32, blk_x, idx),
              ((S5_TILES, m // S5_T, S5_T * LANES), BF16, (S5_TILES, tc, S5_T * LANES), lambda i: (0, i, 0))],
        scratch=[pltpu.VMEM((S5_TILES, tm, LANES), F32)])


def _s5_out(y_tiles, wb16, res, gain, *, tm):
    m, n = res.shape
    tc = tm // S5_T

    def body(wb, r, c, o, scr):
        for rs in _sub_blocks(tm):
            sub = rs.stop - rs.start
            cs = slice(rs.start // S5_T, rs.stop // S5_T)
            for t in range(S5_T):
                for l in range(S5_TILES):
                    scr[0][l, pl.ds(rs.start + t, sub // S5_T, stride=S5_T), :] = (
                        r[0][l, cs, t * LANES:(t + 1) * LANES].astype(F32))
            y = jnp.concatenate([scr[0][l, rs, :] for l in range(S5_TILES)], axis=1).astype(BF16)
            vg = _dot(y, wb[0][...])
            xn = r[1][rs, :] + vg[:, :n] * jax.nn.sigmoid(vg[:, n:])
            o[0][rs, :] = xn
            o[1][rs, :] = _rms(xn, c[0][...]).astype(BF16)

    blk_x, idx = _rows(tm, n)
    return _resident_call(
        body, name="s5_out", m=m, tm=tm, weights=[wb16],
        row_ins=[(y_tiles, (S5_TILES, tc, S5_T * LANES), lambda i: (0, i, 0)), (res, blk_x, idx)],
        consts=[gain.reshape(1, n)],
        outs=[((m, n), F32, blk_x, idx), ((m, n), BF16, blk_x, idx)],
        scratch=[pltpu.VMEM((S5_TILES, tm, LANES), F32)])


def _ffn_up_kernel(*refs, tm, n_casts):
    h_ref, w1_ref, w3_ref = refs[:3]
    ci_refs = refs[3:3 + n_casts]
    o_ref = refs[3 + n_casts]
    co_refs = refs[4 + n_casts:4 + 2 * n_casts]
    w1_scr, w3_scr = refs[4 + 2 * n_casts:]
    _run_casts(ci_refs, co_refs)

    @pl.when(pl.program_id(1) == 0)
    def _():
        w1_scr[...] = w1_ref[...].astype(BF16)
        w3_scr[...] = w3_ref[...].astype(BF16)

    for rs in _sub_blocks(tm):
        h = h_ref[rs, :]
        o_ref[rs, :] = (jax.nn.silu(_dot(h, w1_scr[...])) * _dot(h, w3_scr[...])).astype(o_ref.dtype)


def _ffn_up(h, w1, w3, layer, casts, *, tm, tf):
    m, d = h.shape
    ff = w1.shape[2]
    ni = m // tm
    cast_in, cast_out, cast_shape = _cast_specs(casts, (ff // tf) * ni, lambda f, i: f * ni + i)
    return pl.pallas_call(
        functools.partial(_ffn_up_kernel, tm=tm, n_casts=len(casts)),
        grid=(ff // tf, ni),
        in_specs=[
            pl.BlockSpec((tm, d), lambda f, i: (i, 0)),
            pl.BlockSpec((None, d, tf), lambda f, i: (layer, 0, f)),
            pl.BlockSpec((None, d, tf), lambda f, i: (layer, 0, f)),
        ] + cast_in,
        out_specs=[pl.BlockSpec((tm, tf), lambda f, i: (i, f))] + cast_out,
        out_shape=[jax.ShapeDtypeStruct((m, ff), BF16)] + cast_shape,
        scratch_shapes=[pltpu.VMEM((d, tf), BF16), pltpu.VMEM((d, tf), BF16)],
        compiler_params=_params("arbitrary", "arbitrary"),
        name="ffn_up",
    )(h, w1, w3, *[arr for arr, _ in casts])


def _piece_transpose(v):
    piece = jax.lax.broadcasted_iota(jnp.int32, v[0].shape, 1) // S5_GROUP
    for delta in (4, 2, 1):
        keep = (piece & delta) == 0
        shift = delta * S5_GROUP
        new = list(v)
        for i in range(S5_GB):
            if i & delta == 0:
                a, b = v[i], v[i + delta]
                new[i] = jnp.where(keep, a, pltpu.roll(b, shift, axis=1))
                new[i + delta] = jnp.where(keep, pltpu.roll(a, LANES - shift, axis=1), b)
        v = new
    return v


def _toeplitz_rows(strip, s):
    lo, hi = strip[:, :LANES], strip[:, LANES:]
    lane = jax.lax.broadcasted_iota(jnp.int32, lo.shape, 1)
    shift = (s * S5_GROUP) % LANES
    if s == 0:
        return strip
    if s * S5_GROUP < LANES:
        lo_r, hi_r = pltpu.roll(lo, shift, axis=1), pltpu.roll(hi, shift, axis=1)
        return jnp.concatenate([jnp.where(lane >= shift, lo_r, 0.0),
                                jnp.where(lane >= shift, hi_r, lo_r)], axis=1)
    lo_r = lo if shift == 0 else pltpu.roll(lo, shift, axis=1)
    return jnp.concatenate([jnp.zeros_like(lo), jnp.where(lane >= shift, lo_r, 0.0)], axis=1)


def _cmul_add(s, fa, fb, x):
    return s + fa * x + fb * pltpu.roll(x, S5_STATE, axis=1)


def _chunk_scan(ss, sc_ref, sf_ref, s_scr, c_scr, *, chunks_per_seq):
    groups = range(len(ss))
    rows = ss[0].shape[0]
    tiles = rows // SUBLANES
    r_idx = jax.lax.broadcasted_iota(jnp.int32, ss[0].shape, 0) % SUBLANES
    for k in range(3):
        d = 1 << k
        ss = [_cmul_add(ss[g], sc_ref[g, k:k + 1, :], sc_ref[g, 8 + k:9 + k, :],
                        jnp.where(r_idx >= d, pltpu.roll(ss[g], d, axis=0), 0.0)) for g in groups]
    for g in groups:
        s_scr[g] = ss[g]
    ts = [s_scr[g, pl.ds(SUBLANES - 1, tiles, stride=SUBLANES), :] for g in groups]
    j_idx = jax.lax.broadcasted_iota(jnp.int32, ts[0].shape, 0) % (chunks_per_seq // SUBLANES)
    for k in range(3, S5_SCAN_STEPS):
        d = 1 << (k - 3)
        ts = [_cmul_add(ts[g], sc_ref[g, k:k + 1, :], sc_ref[g, 8 + k:9 + k, :],
                        jnp.where(j_idx >= d, pltpu.roll(ts[g], d, axis=0), 0.0)) for g in groups]
    for g in groups:
        carry = jnp.where(j_idx >= 1, pltpu.roll(ts[g], 1, axis=0), 0.0)
        c_scr[g, 0] = carry
        c_scr[g, 1] = pltpu.roll(carry, S5_STATE, axis=1)
    outs = []
    for g in groups:
        fa, fb = sf_ref[g, :SUBLANES, :], sf_ref[g, SUBLANES:, :]
        out = []
        for j in range(tiles):
            cb = jnp.broadcast_to(c_scr[g, 0, j:j + 1, :], (SUBLANES, S5_RI))
            cs = jnp.broadcast_to(c_scr[g, 1, j:j + 1, :], (SUBLANES, S5_RI))
            out.append(ss[g][j * SUBLANES:(j + 1) * SUBLANES] + fa * cb + fb * cs)
        outs.append(jnp.concatenate(out, axis=0))
    return outs


def _spread(a, rep):
    hi = a.astype(BF16)
    lo = (a - hi.astype(F32)).astype(BF16)
    return _dot(hi, rep) + _dot(lo, rep)


def _s5_kernel(x_ref, bb_ref, ct_ref, pt_ref, rep_ref, skip_ref, min_ref, sc_ref, sf_ref, o_ref,
               mi_scr, mo_scr, s_scr, c_scr, *, chunks_per_seq):
    rows = x_ref.shape[1]
    n_idx = jax.lax.broadcasted_iota(jnp.int32, (rows, S5_RI), 0) % chunks_per_seq
    halves = S5_T // S5_GB
    gp = S5_GB * S5_STATE
    cl_re = _spread(ct_ref[0].reshape(gp, 2 * S5_GROUP), rep_ref[0])
    cl_im = _spread(ct_ref[1].reshape(gp, 2 * S5_GROUP), rep_ref[0])
    pt_re, pt_im = pt_ref[0].reshape(gp, 2 * S5_GROUP), pt_ref[1].reshape(gp, 2 * S5_GROUP)
    readout = []
    for k0 in range(2):
        pl_re, pl_im = _spread(pt_re, rep_ref[1 + k0]), _spread(pt_im, rep_ref[1 + k0])
        readout.append((cl_re * pl_re - cl_im * pl_im, -(cl_re * pl_im + cl_im * pl_re)))
    for gi in range(S5_GB):
        ps = slice(gi * S5_STATE, (gi + 1) * S5_STATE)
        tap = jnp.concatenate([readout[0][0][ps], readout[0][1][ps]], axis=0)
        mo_scr[gi] = jnp.concatenate([readout[1][0][ps], readout[1][1][ps]], axis=0).astype(BF16)
        strip = jnp.dot(bb_ref[gi], tap, preferred_element_type=F32,
                        precision=jax.lax.Precision.HIGHEST) + skip_ref[gi]
        for s in range(S5_T):
            mi_scr[gi, s * S5_GROUP:(s + 1) * S5_GROUP, :] = _toeplitz_rows(strip, s).astype(BF16)
    xin = [_piece_transpose([x_ref[0, :, (S5_GB * h + i) * LANES:(S5_GB * h + i + 1) * LANES]
                             for i in range(S5_GB)]) for h in range(halves)]
    groups = range(S5_GB)
    xg = [jnp.concatenate([xin[h][g] for h in range(halves)], axis=1) for g in groups]
    ys = [_dot(xg[g], mi_scr[g]) for g in groups]
    ss = [_dot(xg[g], min_ref[g]) for g in groups]
    ss = _chunk_scan(ss, sc_ref, sf_ref, s_scr, c_scr, chunks_per_seq=chunks_per_seq)
    for g in groups:
        s_prev = jnp.where(n_idx >= 1, pltpu.roll(ss[g], 1, axis=0), 0.0)
        ys[g] = jax.nn.gelu(ys[g] + _dot(s_prev.astype(BF16), mo_scr[g])).astype(o_ref.dtype)
    for h in range(halves):
        out = _piece_transpose([ys[gi][:, h * LANES:(h + 1) * LANES] for gi in range(S5_GB)])
        for i in range(S5_GB):
            t = S5_GB * h + i
            o_ref[0, :, t * LANES:(t + 1) * LANES] = out[i]


def _s5_core(x_tiles, bb, ct, pt, rep, skip, m_in, sc, sf, *, chunks_per_seq):
    tiles, rows, width = x_tiles.shape
    return pl.pallas_call(
        functools.partial(_s5_kernel, chunks_per_seq=chunks_per_seq),
        grid=(tiles,),
        in_specs=[
            pl.BlockSpec((1, rows, width), lambda i: (i, 0, 0)),
            pl.BlockSpec((S5_GB, S5_GROUP, S5_RI), lambda i: (i, 0, 0)),
            pl.BlockSpec((2, S5_GB, S5_STATE, 2 * S5_GROUP), lambda i: (0, i, 0, 0)),
            pl.BlockSpec((2, S5_GB, S5_STATE, 2 * S5_GROUP), lambda i: (0, i, 0, 0)),
            pl.BlockSpec((3, 2 * S5_GROUP, S5_TC), lambda i: (0, 0, 0)),
            pl.BlockSpec((S5_GB, S5_GROUP, S5_TC), lambda i: (i, 0, 0)),
            pl.BlockSpec((S5_GB, S5_TC, S5_RI), lambda i: (i, 0, 0)),
            pl.BlockSpec((S5_GB, 2 * SUBLANES, S5_RI), lambda i: (i, 0, 0)),
            pl.BlockSpec((S5_GB, 2 * SUBLANES, S5_RI), lambda i: (i, 0, 0)),
        ],
        out_specs=pl.BlockSpec((1, rows, width), lambda i: (i, 0, 0)),
        out_shape=jax.ShapeDtypeStruct((tiles, rows, width), BF16),
        scratch_shapes=[pltpu.VMEM((S5_GB, S5_TC, S5_TC), BF16),
                        pltpu.VMEM((S5_GB, S5_RI, S5_TC), BF16),
                        pltpu.VMEM((S5_GB, rows, S5_RI), F32),
                        pltpu.VMEM((S5_GB, 2, rows // SUBLANES, S5_RI), F32)],
        compiler_params=_params("parallel"),
        name="s5_core",
    )(x_tiles, bb, ct, pt, rep, skip, m_in, sc, sf)


def _s5_tables(a_re, a_im, log_dt, b_re, b_im, c_re, c_im, d):
    a_re, a_im = a_re.astype(F32), a_im.astype(F32)
    dt = jnp.exp(log_dt.astype(F32))[:, None]
    mag = jnp.exp(a_re * dt)
    lb_re, lb_im = mag * jnp.cos(a_im * dt), mag * jnp.sin(a_im * dt)
    den = a_re * a_re + a_im * a_im
    q_re = ((lb_re - 1.0) * a_re + lb_im * a_im) / den
    q_im = (lb_im * a_re - (lb_re - 1.0) * a_im) / den
    b_re = b_re.astype(F32).transpose(0, 2, 1)
    b_im = b_im.astype(F32).transpose(0, 2, 1)
    bb_re = q_re[:, None, :] * b_re - q_im[:, None, :] * b_im
    bb_im = q_re[:, None, :] * b_im + q_im[:, None, :] * b_re
    ks = jnp.arange(S5_T + 1, dtype=F32)[:, None, None]
    pmag = jnp.exp(ks * (a_re * dt))
    pk_re = pmag * jnp.cos(ks * (a_im * dt))
    pk_im = pmag * jnp.sin(ks * (a_im * dt))
    bb = jnp.concatenate([bb_re, bb_im], axis=2)
    pr_re, pr_im = pk_re[S5_T - 1::-1], pk_im[S5_T - 1::-1]
    e_re = pr_re[:, :, None, :] * bb_re - pr_im[:, :, None, :] * bb_im
    e_im = pr_re[:, :, None, :] * bb_im + pr_im[:, :, None, :] * bb_re
    m_in = jnp.concatenate([e_re, e_im], axis=3)
    m_in = m_in.transpose(1, 0, 2, 3).reshape(S5_GROUPS, S5_TC, S5_RI).astype(BF16)
    pad32 = lambda a: jnp.pad(a, ((0, 0), (0, 0), (0, 2 * S5_GROUP - a.shape[2])))
    ct = jnp.stack([pad32(c_re.astype(F32).transpose(0, 2, 1)), pad32(c_im.astype(F32).transpose(0, 2, 1))])
    pt = jnp.stack([pad32(pk_re.transpose(1, 2, 0)), pad32(pk_im.transpose(1, 2, 0))])
    lane = np.arange(S5_TC)[None, :]
    row = np.arange(2 * S5_GROUP)[:, None]
    rep = jnp.asarray(np.stack([lane % S5_GROUP == row, lane // S5_GROUP == row,
                                lane // S5_GROUP + 1 == row]), BF16)
    skip = d.astype(F32).reshape(S5_GROUPS, 1, S5_GROUP) * jnp.eye(S5_GROUP, dtype=F32)
    skip = jnp.pad(skip, ((0, 0), (0, 0), (0, S5_TC - S5_GROUP)))
    f_re, f_im = pk_re[S5_T], pk_im[S5_T]
    rows_a, rows_b = [], []
    for _ in range(S5_SCAN_STEPS):
        rows_a.append(jnp.concatenate([f_re, f_re], axis=1))
        rows_b.append(jnp.concatenate([-f_im, f_im], axis=1))
        f_re, f_im = f_re * f_re - f_im * f_im, 2.0 * f_re * f_im
    zero = jnp.zeros_like(rows_a[0])
    sc = jnp.stack(rows_a + [zero] + rows_b + [zero], axis=1)
    f_re, f_im = pk_re[S5_T], pk_im[S5_T]
    rows_a, rows_b = [], []
    for _ in range(SUBLANES):
        rows_a.append(jnp.concatenate([f_re, f_re], axis=1))
        rows_b.append(jnp.concatenate([-f_im, f_im], axis=1))
        f_re, f_im = (f_re * pk_re[S5_T] - f_im * pk_im[S5_T], f_re * pk_im[S5_T] + f_im * pk_re[S5_T])
    sf = jnp.stack(rows_a + rows_b, axis=1)
    return bb, ct, pt, rep, skip, m_in, sc, sf


def kernel(x, p, norm_mix, norm_ffn, norm_ple, norm_final, gm_w_in, gm_ln_g, gm_ln_b, gm_w_s, gm_b_s, gm_w_out, s5_w_in, s5_a_re, s5_a_im, s5_log_dt, s5_b_re, s5_b_im, s5_c_re, s5_c_im, s5_d, s5_w_out, ffn_w1, ffn_w3, ffn_w2, ple_w_gate, ple_w_proj):
    bsz, seq, d = x.shape
    depth = p.shape[0]
    m = bsz * seq
    xs = x.reshape(m, d)
    ps = p.reshape(depth, m, PLE_DIM)
    u = s5_out_b16 = None
    for i in range(depth):
        j = i // 2
        if i % 2 == 0:
            gm_in_b16, gm_out_b16 = _cast_call([(gm_w_in, j), (gm_w_out, j)])
            xs, hn = _gm_layer(xs, norm_mix[i], gm_in_b16, gm_ln_g, gm_ln_b, gm_w_s, gm_b_s, j, gm_out_b16,
                               norm_ffn[i], tm=512)
        else:
            tables = _s5_tables(s5_a_re[j], s5_a_im[j], s5_log_dt[j], s5_b_re[j], s5_b_im[j],
                                s5_c_re[j], s5_c_im[j], s5_d[j])
            y = _s5_core(u, *tables, chunks_per_seq=seq // S5_T)
            xs, hn = _s5_out(y, s5_out_b16, xs, norm_ffn[i], tm=512)
        a, w2_b16, wg_b16, wp_b16 = _ffn_up(
            hn, ffn_w1, ffn_w3, i, [(ffn_w2, i), (ple_w_gate, i), (ple_w_proj, i)], tm=2048, tf=512)
        if i + 1 < depth and (i + 1) % 2 == 1:
            jn = (i + 1) // 2
            xs, s5_in_b16 = _ffn_down(a, w2_b16, xs, [(s5_w_in, jn)], tm=256)
            xs, u, s5_out_b16 = _ple_s5_in(xs, norm_ple[i], ps, i, wg_b16, wp_b16, norm_mix[i + 1], s5_in_b16,
                                           [(s5_w_out, jn)], tm=512)
        else:
            xs, = _ffn_down_ple(a, w2_b16, xs, norm_ple[i], ps, i, wg_b16, wp_b16,
                                norm_final if i == depth - 1 else None, tm=256)
    return xs.reshape(bsz, seq, d)
```

```python
import functools

import jax
import jax.numpy as jnp
import numpy as np
from jax.experimental import pallas as pl
from jax.experimental.pallas import tpu as pltpu

F32 = jnp.float32
BF16 = jnp.bfloat16

D_MODEL = 2048
PLE_DIM = 256
EPS = 1e-6
LANES = 128
SUBLANES = 8
BF16_ROWS = 16

GM_CHUNK = 128
GM_HEAD_DIM = 128
GM_HEADS = D_MODEL // GM_HEAD_DIM

S5_GROUP = 16
S5_GROUPS = D_MODEL // S5_GROUP
S5_STATE = 64
S5_T = 16
S5_TC = S5_T * S5_GROUP
S5_RI = 2 * S5_STATE
S5_GB = LANES // S5_GROUP
S5_TILES = D_MODEL // LANES
S5_SCAN_STEPS = 7

W_CHUNKS = 16
SUB_ROWS = 256
SUB_COLS = 512
VMEM_LIMIT = 56 * 1024 * 1024


def _params(*sem):
    return pltpu.CompilerParams(dimension_semantics=sem, vmem_limit_bytes=VMEM_LIMIT)


def _rms(xf, gain):
    ms = jnp.mean(xf * xf, axis=-1, keepdims=True)
    return xf * jax.lax.rsqrt(ms + EPS) * gain


def _dot(a, b):
    return jnp.dot(a, b, preferred_element_type=F32)


def _sub_blocks(tm):
    sub = min(tm, SUB_ROWS)
    return [slice(r, r + sub) for r in range(0, tm, sub)]


def _col_blocks(n):
    return [slice(c, c + SUB_COLS) for c in range(0, n, SUB_COLS)]


def _cast_rows(k, steps):
    rows = BF16_ROWS
    while k % rows or k // rows > steps:
        rows += BF16_ROWS
    return rows


def _cast_specs(casts, steps, step_of):
    ins, outs, shapes = [], [], []
    for arr, layer in casts:
        _, k, n = arr.shape
        rows = _cast_rows(k, steps)
        last = k // rows - 1
        ins.append(pl.BlockSpec(
            (None, rows, n), lambda *g, layer=layer, last=last: (layer, jnp.minimum(step_of(*g), last), 0)))
        outs.append(pl.BlockSpec((rows, n), lambda *g, last=last: (jnp.minimum(step_of(*g), last), 0)))
        shapes.append(jax.ShapeDtypeStruct((k, n), BF16))
    return ins, outs, shapes


def _run_casts(ci_refs, co_refs):
    for ci, co in zip(ci_refs, co_refs):
        co[...] = ci[...].astype(BF16)


def _cast_kernel(*refs):
    half = len(refs) // 2
    _run_casts(refs[:half], refs[half:])


def _cast_call(casts):
    cast_in, cast_out, cast_shape = _cast_specs(casts, W_CHUNKS, lambda s: s)
    return pl.pallas_call(
        _cast_kernel,
        grid=(W_CHUNKS,),
        in_specs=cast_in,
        out_specs=cast_out,
        out_shape=cast_shape,
        compiler_params=_params("arbitrary"),
        name="cast_weights",
    )(*[arr for arr, _ in casts])


def _resident_call(body, *, name, m, tm, weights, row_ins, consts, outs, casts=(), scratch=()):
    nw, nk, nr, nc, no = len(weights), len(casts), len(row_ins), len(consts), len(outs)
    steps = m // tm

    def kern(*refs):
        pos = 0

        def take(n):
            nonlocal pos
            pos += n
            return refs[pos - n:pos]

        w_refs, ci_refs, r_refs, c_refs = take(nw), take(nk), take(nr), take(nc)
        o_refs, co_refs, s_refs = take(no), take(nk), take(len(scratch))
        _run_casts(ci_refs, co_refs)
        body(w_refs, r_refs, c_refs, o_refs, s_refs)

    in_specs, args = [], []
    for arr in weights:
        in_specs.append(pl.BlockSpec(arr.shape, lambda s: (0, 0), pipeline_mode=pl.Buffered(1)))
        args.append(arr)
    cast_in, cast_out, cast_shape = _cast_specs(casts, steps, lambda s: s)
    in_specs += cast_in
    args += [arr for arr, _ in casts]
    for arr, block, index_fn in row_ins:
        in_specs.append(pl.BlockSpec(block, index_fn))
        args.append(arr)
    for arr in consts:
        in_specs.append(pl.BlockSpec((1, arr.shape[1]), lambda s: (0, 0), pipeline_mode=pl.Buffered(1)))
        args.append(arr)
    out_specs = [pl.BlockSpec(block, index_fn) for _, _, block, index_fn in outs]
    out_shape = [jax.ShapeDtypeStruct(shape, dtype) for shape, dtype, _, _ in outs]
    return pl.pallas_call(
        kern,
        grid=(steps,),
        in_specs=in_specs,
        out_specs=out_specs + cast_out,
        out_shape=out_shape + cast_shape,
        scratch_shapes=list(scratch),
        compiler_params=_params("arbitrary"),
        name=name,
    )(*args)


def _rows(tm, n):
    return (tm, n), (lambda i: (i, 0))


def _gm_layer(x, gain_mix, w_in_b16, ln_g, ln_b, w_s, b_s, layer, w_out_b16, gain, *, tm):
    m, n = x.shape
    bs = jnp.broadcast_to(b_s[layer][:, :, None], (GM_HEADS, GM_CHUNK, GM_HEAD_DIM)).astype(F32)

    def body(wb, r, c, o, scr):
        x_ref, ws_ref, bs_ref = r
        row = jax.lax.broadcasted_iota(jnp.int32, (GM_CHUNK, GM_CHUNK), 0)
        col = jax.lax.broadcasted_iota(jnp.int32, (GM_CHUNK, GM_CHUNK), 1)
        wm = [jnp.where(row >= col, ws_ref[h], 0.0).astype(BF16) for h in range(GM_HEADS)]
        for rs in _sub_blocks(tm):
            xf = x_ref[rs, :]
            hx = _rms(xf, c[3][...]).astype(BF16)
            v_all = jax.nn.gelu(_dot(hx, wb[0][:, n:]))
            u_all = jax.nn.gelu(_dot(hx, wb[0][:, :n])).astype(BF16)
            for c0 in range(0, rs.stop - rs.start, GM_CHUNK):
                rows = slice(rs.start + c0, rs.start + c0 + GM_CHUNK)
                v = v_all[c0:c0 + GM_CHUNK]
                mu = jnp.mean(v, axis=-1, keepdims=True)
                vc = v - mu
                var = jnp.mean(vc * vc, axis=-1, keepdims=True)
                vn = (vc * jax.lax.rsqrt(var + EPS) * c[1][...] + c[2][...]).astype(BF16)
                for h in range(GM_HEADS):
                    cols = slice(h * GM_HEAD_DIM, (h + 1) * GM_HEAD_DIM)
                    sv = _dot(wm[h], vn[:, cols]) + bs_ref[h]
                    scr[0][rows, cols] = (u_all[c0:c0 + GM_CHUNK, cols].astype(F32) * sv).astype(BF16)
            xn = xf + _dot(scr[0][rs, :], wb[1][...])
            o[0][rs, :] = xn
            o[1][rs, :] = _rms(xn, c[0][...]).astype(BF16)

    blk_x, idx = _rows(tm, n)
    return _resident_call(
        body, name="gm_layer", m=m, tm=tm, weights=[w_in_b16, w_out_b16],
        row_ins=[(x, blk_x, idx),
                 (w_s, (None, GM_HEADS, GM_CHUNK, GM_CHUNK), lambda i: (layer, 0, 0, 0)),
                 (bs, (GM_HEADS, GM_CHUNK, GM_HEAD_DIM), lambda i: (0, 0, 0))],
        consts=[gain.reshape(1, n), ln_g[layer].reshape(1, n), ln_b[layer].reshape(1, n),
                gain_mix.reshape(1, n)],
        outs=[((m, n), F32, blk_x, idx), ((m, n), BF16, blk_x, idx)],
        scratch=[pltpu.VMEM((tm, n), BF16)])


def _ffn_down(a, w2b16, res, casts, *, tm):
    m, n = res.shape

    def body(wb, r, c, o, scr):
        for rs in _sub_blocks(tm):
            o[0][rs, :] = r[1][rs, :] + _dot(r[0][rs, :], wb[0][...])

    blk_a, idx = _rows(tm, a.shape[1])
    blk_x, _ = _rows(tm, n)
    return _resident_call(
        body, name="ffn_down", m=m, tm=tm, weights=[w2b16], casts=casts,
        row_ins=[(a, blk_a, idx), (res, blk_x, idx)], consts=[],
        outs=[((m, n), F32, blk_x, idx)])


def _ffn_down_ple(a, w2b16, res, gain, p, layer, wgb16, wpb16, final_gain, *, tm):
    m, n = res.shape

    def body(wb, r, c, o, scr):
        for rs in _sub_blocks(tm):
            x1 = r[1][rs, :] + _dot(r[0][rs, :], wb[0][...])
            h = _rms(x1, c[0][...]).astype(BF16)
            pb = r[2][rs, :].astype(BF16)
            xn = jnp.concatenate(
                [x1[:, cs] + jax.nn.sigmoid(_dot(h, wb[1][:, cs])) * _dot(pb, wb[2][:, cs])
                 for cs in _col_blocks(n)], axis=1)
            if final_gain is not None:
                xn = _rms(xn, c[1][...])
            o[0][rs, :] = xn

    blk_a, idx = _rows(tm, a.shape[1])
    blk_x, _ = _rows(tm, n)
    blk_p, _ = _rows(tm, PLE_DIM)
    consts = [gain.reshape(1, n)] + ([] if final_gain is None else [final_gain.reshape(1, n)])
    return _resident_call(
        body, name="ffn_down_ple", m=m, tm=tm, weights=[w2b16, wgb16, wpb16],
        row_ins=[(a, blk_a, idx), (res, blk_x, idx), (p, (None,) + blk_p, lambda i: (layer, i, 0))],
        consts=consts, outs=[((m, n), F32, blk_x, idx)])


def _ple_s5_in(x, gain, p, layer, wgb16, wpb16, gain_mix, w_in_b16, casts, *, tm):
    m, n = x.shape
    tc = tm // S5_T

    def body(wb, r, c, o, scr):
        for rs in _sub_blocks(tm):
            xf = r[0][rs, :]
            h = _rms(xf, c[0][...]).astype(BF16)
            pb = r[1][rs, :].astype(BF16)
            xn = jnp.concatenate(
                [xf[:, cs] + jax.nn.sigmoid(_dot(h, wb[0][:, cs])) * _dot(pb, wb[1][:, cs])
                 for cs in _col_blocks(n)], axis=1)
            o[0][rs, :] = xn
            u = _dot(_rms(xn, c[1][...]).astype(BF16), wb[2][...])
            for l in range(S5_TILES):
                scr[0][l, rs, :] = u[:, l * LANES:(l + 1) * LANES]
            sub = rs.stop - rs.start
            cs = slice(rs.start // S5_T, rs.stop // S5_T)
            for t in range(S5_T):
                for l in range(S5_TILES):
                    o[1][l, cs, t * LANES:(t + 1) * LANES] = (
                        scr[0][l, pl.ds(rs.start + t, sub // S5_T, stride=S5_T), :].astype(BF16))

    blk_x, idx = _rows(tm, n)
    blk_p, _ = _rows(tm, PLE_DIM)
    return _resident_call(
        body, name="ple_s5_in", m=m, tm=tm, weights=[wgb16, wpb16, w_in_b16], casts=casts,
        row_ins=[(x, blk_x, idx), (p, (None,) + blk_p, lambda i: (layer, i, 0))],
        consts=[gain.reshape(1, n), gain_mix.reshape(1, n)],
        outs=[((m, n), F32, blk_x, idx),
              ((S5_TILES, m // S5_T, S5_T * LANES), BF16, (S5_TILES, tc, S5_T * LANES), lambda i: (0, i, 0))],
        scratch=[pltpu.VMEM((S5_TILES, tm, LANES), F32)])


def _s5_out(y_tiles, wb16, res, gain, *, tm):
    m, n = res.shape
    tc = tm // S5_T

    def body(wb, r, c, o, scr):
        for rs in _sub_blocks(tm):
            sub = rs.stop - rs.start
            cs = slice(rs.start // S5_T, rs.stop // S5_T)
            for t in range(S5_T):
                for l in range(S5_TILES):
                    scr[0][l, pl.ds(rs.start + t, sub // S5_T, stride=S5_T), :] = (
                        r[0][l, cs, t * LANES:(t + 1) * LANES].astype(F32))
            y = jnp.concatenate([scr[0][l, rs, :] for l in range(S5_TILES)], axis=1).astype(BF16)
            vg = _dot(y, wb[0][...])
            xn = r[1][rs, :] + vg[:, :n] * jax.nn.sigmoid(vg[:, n:])
            o[0][rs, :] = xn
            o[1][rs, :] = _rms(xn, c[0][...]).astype(BF16)

    blk_x, idx = _rows(tm, n)
    return _resident_call(
        body, name="s5_out", m=m, tm=tm, weights=[wb16],
        row_ins=[(y_tiles, (S5_TILES, tc, S5_T * LANES), lambda i: (0, i, 0)), (res, blk_x, idx)],
        consts=[gain.reshape(1, n)],
        outs=[((m, n), F32, blk_x, idx), ((m, n), BF16, blk_x, idx)],
        scratch=[pltpu.VMEM((S5_TILES, tm, LANES), F32)])


def _ffn_up_kernel(*refs, tm, n_casts):
    h_ref, w1_ref, w3_ref = refs[:3]
    ci_refs = refs[3:3 + n_casts]
    o_ref = refs[3 + n_casts]
    co_refs = refs[4 + n_casts:4 + 2 * n_casts]
    w1_scr, w3_scr = refs[4 + 2 * n_casts:]
    _run_casts(ci_refs, co_refs)

    @pl.when(pl.program_id(1) == 0)
    def _():
        w1_scr[...] = w1_ref[...].astype(BF16)
        w3_scr[...] = w3_ref[...].astype(BF16)

    for rs in _sub_blocks(tm):
        h = h_ref[rs, :]
        o_ref[rs, :] = (jax.nn.silu(_dot(h, w1_scr[...])) * _dot(h, w3_scr[...])).astype(o_ref.dtype)


def _ffn_up(h, w1, w3, layer, casts, *, tm, tf):
    m, d = h.shape
    ff = w1.shape[2]
    ni = m // tm
    cast_in, cast_out, cast_shape = _cast_specs(casts, (ff // tf) * ni, lambda f, i: f * ni + i)
    return pl.pallas_call(
        functools.partial(_ffn_up_kernel, tm=tm, n_casts=len(casts)),
        grid=(ff // tf, ni),
        in_specs=[
            pl.BlockSpec((tm, d), lambda f, i: (i, 0)),
            pl.BlockSpec((None, d, tf), lambda f, i: (layer, 0, f)),
            pl.BlockSpec((None, d, tf), lambda f, i: (layer, 0, f)),
        ] + cast_in,
        out_specs=[pl.BlockSpec((tm, tf), lambda f, i: (i, f))] + cast_out,
        out_shape=[jax.ShapeDtypeStruct((m, ff), BF16)] + cast_shape,
        scratch_shapes=[pltpu.VMEM((d, tf), BF16), pltpu.VMEM((d, tf), BF16)],
        compiler_params=_params("arbitrary", "arbitrary"),
        name="ffn_up",
    )(h, w1, w3, *[arr for arr, _ in casts])


def _piece_transpose(v):
    piece = jax.lax.broadcasted_iota(jnp.int32, v[0].shape, 1) // S5_GROUP
    for delta in (4, 2, 1):
        keep = (piece & delta) == 0
        shift = delta * S5_GROUP
        new = list(v)
        for i in range(S5_GB):
            if i & delta == 0:
                a, b = v[i], v[i + delta]
                new[i] = jnp.where(keep, a, pltpu.roll(b, shift, axis=1))
                new[i + delta] = jnp.where(keep, pltpu.roll(a, LANES - shift, axis=1), b)
        v = new
    return v


def _toeplitz_rows(strip, s):
    lo, hi = strip[:, :LANES], strip[:, LANES:]
    lane = jax.lax.broadcasted_iota(jnp.int32, lo.shape, 1)
    shift = (s * S5_GROUP) % LANES
    if s == 0:
        return strip
    if s * S5_GROUP < LANES:
        lo_r, hi_r = pltpu.roll(lo, shift, axis=1), pltpu.roll(hi, shift, axis=1)
        return jnp.concatenate([jnp.where(lane >= shift, lo_r, 0.0),
                                jnp.where(lane >= shift, hi_r, lo_r)], axis=1)
    lo_r = lo if shift == 0 else pltpu.roll(lo, shift, axis=1)
    return jnp.concatenate([jnp.zeros_like(lo), jnp.where(lane >= shift, lo_r, 0.0)], axis=1)


def _cmul_add(s, fa, fb, x):
    return s + fa * x + fb * pltpu.roll(x, S5_STATE, axis=1)


def _chunk_scan(ss, sc_ref, sf_ref, s_scr, c_scr, *, chunks_per_seq):
    groups = range(len(ss))
    rows = ss[0].shape[0]
    tiles = rows // SUBLANES
    r_idx = jax.lax.broadcasted_iota(jnp.int32, ss[0].shape, 0) % SUBLANES
    for k in range(3):
        d = 1 << k
        ss = [_cmul_add(ss[g], sc_ref[g, k:k + 1, :], sc_ref[g, 8 + k:9 + k, :],
                        jnp.where(r_idx >= d, pltpu.roll(ss[g], d, axis=0), 0.0)) for g in groups]
    for g in groups:
        s_scr[g] = ss[g]
    ts = [s_scr[g, pl.ds(SUBLANES - 1, tiles, stride=SUBLANES), :] for g in groups]
    j_idx = jax.lax.broadcasted_iota(jnp.int32, ts[0].shape, 0) % (chunks_per_seq // SUBLANES)
    for k in range(3, S5_SCAN_STEPS):
        d = 1 << (k - 3)
        ts = [_cmul_add(ts[g], sc_ref[g, k:k + 1, :], sc_ref[g, 8 + k:9 + k, :],
                        jnp.where(j_idx >= d, pltpu.roll(ts[g], d, axis=0), 0.0)) for g in groups]
    for g in groups:
        carry = jnp.where(j_idx >= 1, pltpu.roll(ts[g], 1, axis=0), 0.0)
        c_scr[g, 0] = carry
        c_scr[g, 1] = pltpu.roll(carry, S5_STATE, axis=1)
    outs = []
    for g in groups:
        fa, fb = sf_ref[g, :SUBLANES, :], sf_ref[g, SUBLANES:, :]
        out = []
        for j in range(tiles):
            cb = jnp.broadcast_to(c_scr[g, 0, j:j + 1, :], (SUBLANES, S5_RI))
            cs = jnp.broadcast_to(c_scr[g, 1, j:j + 1, :], (SUBLANES, S5_RI))
            out.append(ss[g][j * SUBLANES:(j + 1) * SUBLANES] + fa * cb + fb * cs)
        outs.append(jnp.concatenate(out, axis=0))
    return outs


def _spread(a, rep):
    hi = a.astype(BF16)
    lo = (a - hi.astype(F32)).astype(BF16)
    return _dot(hi, rep) + _dot(lo, rep)


def _s5_kernel(x_ref, bb_ref, ct_ref, pt_ref, rep_ref, skip_ref, min_ref, sc_ref, sf_ref, o_ref,
               mi_scr, mo_scr, s_scr, c_scr, *, chunks_per_seq):
    rows = x_ref.shape[1]
    n_idx = jax.lax.broadcasted_iota(jnp.int32, (rows, S5_RI), 0) % chunks_per_seq
    halves = S5_T // S5_GB
    gp = S5_GB * S5_STATE
    cl_re = _spread(ct_ref[0].reshape(gp, 2 * S5_GROUP), rep_ref[0])
    cl_im = _spread(ct_ref[1].reshape(gp, 2 * S5_GROUP), rep_ref[0])
    pt_re, pt_im = pt_ref[0].reshape(gp, 2 * S5_GROUP), pt_ref[1].reshape(gp, 2 * S5_GROUP)
    readout = []
    for k0 in range(2):
        pl_re, pl_im = _spread(pt_re, rep_ref[1 + k0]), _spread(pt_im, rep_ref[1 + k0])
        readout.append((cl_re * pl_re - cl_im * pl_im, -(cl_re * pl_im + cl_im * pl_re)))
    for gi in range(S5_GB):
        ps = slice(gi * S5_STATE, (gi + 1) * S5_STATE)
        tap = jnp.concatenate([readout[0][0][ps], readout[0][1][ps]], axis=0)
        mo_scr[gi] = jnp.concatenate([readout[1][0][ps], readout[1][1][ps]], axis=0).astype(BF16)
        strip = jnp.dot(bb_ref[gi], tap, preferred_element_type=F32,
                        precision=jax.lax.Precision.HIGHEST) + skip_ref[gi]
        for s in range(S5_T):
            mi_scr[gi, s * S5_GROUP:(s + 1) * S5_GROUP, :] = _toeplitz_rows(strip, s).astype(BF16)
    xin = [_piece_transpose([x_ref[0, :, (S5_GB * h + i) * LANES:(S5_GB * h + i + 1) * LANES]
                             for i in range(S5_GB)]) for h in range(halves)]
    groups = range(S5_GB)
    xg = [jnp.concatenate([xin[h][g] for h in range(halves)], axis=1) for g in groups]
    ys = [_dot(xg[g], mi_scr[g]) for g in groups]
    ss = [_dot(xg[g], min_ref[g]) for g in groups]
    ss = _chunk_scan(ss, sc_ref, sf_ref, s_scr, c_scr, chunks_per_seq=chunks_per_seq)
    for g in groups:
        s_prev = jnp.where(n_idx >= 1, pltpu.roll(ss[g], 1, axis=0), 0.0)
        ys[g] = jax.nn.gelu(ys[g] + _dot(s_prev.astype(BF16), mo_scr[g])).astype(o_ref.dtype)
    for h in range(halves):
        out = _piece_transpose([ys[gi][:, h * LANES:(h + 1) * LANES] for gi in range(S5_GB)])
        for i in range(S5_GB):
            t = S5_GB * h + i
            o_ref[0, :, t * LANES:(t + 1) * LANES] = out[i]


def _s5_core(x_tiles, bb, ct, pt, rep, skip, m_in, sc, sf, *, chunks_per_seq):
    tiles, rows, width = x_tiles.shape
    return pl.pallas_call(
        functools.partial(_s5_kernel, chunks_per_seq=chunks_per_seq),
        grid=(tiles,),
        in_specs=[
            pl.BlockSpec((1, rows, width), lambda i: (i, 0, 0)),
            pl.BlockSpec((S5_GB, S5_GROUP, S5_RI), lambda i: (i, 0, 0)),
            pl.BlockSpec((2, S5_GB, S5_STATE, 2 * S5_GROUP), lambda i: (0, i, 0, 0)),
            pl.BlockSpec((2, S5_GB, S5_STATE, 2 * S5_GROUP), lambda i: (0, i, 0, 0)),
            pl.BlockSpec((3, 2 * S5_GROUP, S5_TC), lambda i: (0, 0, 0)),
            pl.BlockSpec((S5_GB, S5_GROUP, S5_TC), lambda i: (i, 0, 0)),
            pl.BlockSpec((S5_GB, S5_TC, S5_RI), lambda i: (i, 0, 0)),
            pl.BlockSpec((S5_GB, 2 * SUBLANES, S5_RI), lambda i: (i, 0, 0)),
            pl.BlockSpec((S5_GB, 2 * SUBLANES, S5_RI), lambda i: (i, 0, 0)),
        ],
        out_specs=pl.BlockSpec((1, rows, width), lambda i: (i, 0, 0)),
        out_shape=jax.ShapeDtypeStruct((tiles, rows, width), BF16),
        scratch_shapes=[pltpu.VMEM((S5_GB, S5_TC, S5_TC), BF16),
                        pltpu.VMEM((S5_GB, S5_RI, S5_TC), BF16),
                        pltpu.VMEM((S5_GB, rows, S5_RI), F32),
                        pltpu.VMEM((S5_GB, 2, rows // SUBLANES, S5_RI), F32)],
        compiler_params=_params("parallel"),
        name="s5_core",
    )(x_tiles, bb, ct, pt, rep, skip, m_in, sc, sf)


def _s5_tables(a_re, a_im, log_dt, b_re, b_im, c_re, c_im, d):
    a_re, a_im = a_re.astype(F32), a_im.astype(F32)
    dt = jnp.exp(log_dt.astype(F32))[:, None]
    mag = jnp.exp(a_re * dt)
    lb_re, lb_im = mag * jnp.cos(a_im * dt), mag * jnp.sin(a_im * dt)
    den = a_re * a_re + a_im * a_im
    q_re = ((lb_re - 1.0) * a_re + lb_im * a_im) / den
    q_im = (lb_im * a_re - (lb_re - 1.0) * a_im) / den
    b_re = b_re.astype(F32).transpose(0, 2, 1)
    b_im = b_im.astype(F32).transpose(0, 2, 1)
    bb_re = q_re[:, None, :] * b_re - q_im[:, None, :] * b_im
    bb_im = q_re[:, None, :] * b_im + q_im[:, None, :] * b_re
    ks = jnp.arange(S5_T + 1, dtype=F32)[:, None, None]
    pmag = jnp.exp(ks * (a_re * dt))
    pk_re = pmag * jnp.cos(ks * (a_im * dt))
    pk_im = pmag * jnp.sin(ks * (a_im * dt))
    bb = jnp.concatenate([bb_re, bb_im], axis=2)
    pr_re, pr_im = pk_re[S5_T - 1::-1], pk_im[S5_T - 1::-1]
    e_re = pr_re[:, :, None, :] * bb_re - pr_im[:, :, None, :] * bb_im
    e_im = pr_re[:, :, None, :] * bb_im + pr_im[:, :, None, :] * bb_re
    m_in = jnp.concatenate([e_re, e_im], axis=3)
    m_in = m_in.transpose(1, 0, 2, 3).reshape(S5_GROUPS, S5_TC, S5_RI).astype(BF16)
    pad32 = lambda a: jnp.pad(a, ((0, 0), (0, 0), (0, 2 * S5_GROUP - a.shape[2])))
    ct = jnp.stack([pad32(c_re.astype(F32).transpose(0, 2, 1)), pad32(c_im.astype(F32).transpose(0, 2, 1))])
    pt = jnp.stack([pad32(pk_re.transpose(1, 2, 0)), pad32(pk_im.transpose(1, 2, 0))])
    lane = np.arange(S5_TC)[None, :]
    row = np.arange(2 * S5_GROUP)[:, None]
    rep = jnp.asarray(np.stack([lane % S5_GROUP == row, lane // S5_GROUP == row,
                                lane // S5_GROUP + 1 == row]), BF16)
    skip = d.astype(F32).reshape(S5_GROUPS, 1, S5_GROUP) * jnp.eye(S5_GROUP, dtype=F32)
    skip = jnp.pad(skip, ((0, 0), (0, 0), (0, S5_TC - S5_GROUP)))
    f_re, f_im = pk_re[S5_T], pk_im[S5_T]
    rows_a, rows_b = [], []
    for _ in range(S5_SCAN_STEPS):
        rows_a.append(jnp.concatenate([f_re, f_re], axis=1))
        rows_b.append(jnp.concatenate([-f_im, f_im], axis=1))
        f_re, f_im = f_re * f_re - f_im * f_im, 2.0 * f_re * f_im
    zero = jnp.zeros_like(rows_a[0])
    sc = jnp.stack(rows_a + [zero] + rows_b + [zero], axis=1)
    f_re, f_im = pk_re[S5_T], pk_im[S5_T]
    rows_a, rows_b = [], []
    for _ in range(SUBLANES):
        rows_a.append(jnp.concatenate([f_re, f_re], axis=1))
        rows_b.append(jnp.concatenate([-f_im, f_im], axis=1))
        f_re, f_im = (f_re * pk_re[S5_T] - f_im * pk_im[S5_T], f_re * pk_im[S5_T] + f_im * pk_re[S5_T])
    sf = jnp.stack(rows_a + rows_b, axis=1)
    return bb, ct, pt, rep, skip, m_in, sc, sf


def kernel(x, p, norm_mix, norm_ffn, norm_ple, norm_final, gm_w_in, gm_ln_g, gm_ln_b, gm_w_s, gm_b_s, gm_w_out, s5_w_in, s5_a_re, s5_a_im, s5_log_dt, s5_b_re, s5_b_im, s5_c_re, s5_c_im, s5_d, s5_w_out, ffn_w1, ffn_w3, ffn_w2, ple_w_gate, ple_w_proj):
    bsz, seq, d = x.shape
    depth = p.shape[0]
    m = bsz * seq
    xs = x.reshape(m, d)
    ps = p.reshape(depth, m, PLE_DIM)
    u = s5_out_b16 = None
    for i in range(depth):
        j = i // 2
        if i % 2 == 0:
            gm_in_b16, gm_out_b16 = _cast_call([(gm_w_in, j), (gm_w_out, j)])
            xs, hn = _gm_layer(xs, norm_mix[i], gm_in_b16, gm_ln_g, gm_ln_b, gm_w_s, gm_b_s, j, gm_out_b16,
                               norm_ffn[i], tm=512)
        else:
            tables = _s5_tables(s5_a_re[j], s5_a_im[j], s5_log_dt[j], s5_b_re[j], s5_b_im[j],
                                s5_c_re[j], s5_c_im[j], s5_d[j])
            y = _s5_core(u, *tables, chunks_per_seq=seq // S5_T)
            xs, hn = _s5_out(y, s5_out_b16, xs, norm_ffn[i], tm=512)
        a, w2_b16, wg_b16, wp_b16 = _ffn_up(
            hn, ffn_w1, ffn_w3, i, [(ffn_w2, i), (ple_w_gate, i), (ple_w_proj, i)], tm=2048, tf=512)
        if i + 1 < depth and (i + 1) % 2 == 1:
            jn = (i + 1) // 2
            xs, s5_in_b16 = _ffn_down(a, w2_b16, xs, [(s5_w_in, jn)], tm=256)
            xs, u, s5_out_b16 = _ple_s5_in(xs, norm_ple[i], ps, i, wg_b16, wp_b16, norm_mix[i + 1], s5_in_b16,
                                           [(s5_w_out, jn)], tm=512)
        else:
            xs, = _ffn_down_ple(a, w2_b16, xs, norm_ple[i], ps, i, wg_b16, wp_b16,
                                norm_final if i == depth - 1 else None, tm=256)
    return xs.reshape(bsz, seq, d)
```
